```python
import jax, jax.numpy as jnp
from jax import lax
import numpy as np

D_MODEL = 1024
BATCH = 16
SEQ = 2048
DEPTH = 2

GDN_HEADS = 8
GDN_DK = 128
GDN_DV = 128
CONV_K = 5
CHUNK = 64
MLA_HEADS = 8
Q_LORA = 384
KV_LORA = 256
QK_NOPE = 128
QK_ROPE = 64
V_HEAD = 128
ROPE_BASE = 10000.0
Q_BLOCK = 128
N_GROUPS = 8
EXPERTS_PER_GROUP = 8
N_EXPERTS = N_GROUPS * EXPERTS_PER_GROUP
TOP_K_IN_GROUP = 2
D_EXPERT = 512
MOE_BLOCK = 128
DN_ALPHA = (2 * DEPTH) ** 0.25
DN_BETA = (8 * DEPTH) ** -0.25
LN_EPS = 1e-5
RMS_EPS = 1e-6

GDN_QK_W = GDN_HEADS * GDN_DK
GDN_V_W = GDN_HEADS * GDN_DV
MLA_Q_W = MLA_HEADS * (QK_NOPE + QK_ROPE)
MLA_KV_W = MLA_HEADS * (QK_NOPE + V_HEAD)
IN_WIDTHS = (GDN_QK_W, GDN_QK_W, GDN_V_W, GDN_V_W, 2 * GDN_HEADS, 2 * GDN_HEADS,
             Q_LORA, KV_LORA, QK_ROPE, 2 * D_MODEL)
IN_DIM = sum(IN_WIDTHS)

kernel_name = "hybrid_gdn_mla_hmoe_deepnorm"


def layer_norm(x, g, b):
    xf = x.astype(jnp.float32)
    xc = xf - jnp.mean(xf, -1, keepdims=True)
    var = jnp.mean(xc * xc, -1, keepdims=True)
    return (xc * lax.rsqrt(var + LN_EPS) * g.astype(jnp.float32) + b.astype(jnp.float32)).astype(x.dtype)


def rms_norm(x, w):
    xf = x.astype(jnp.float32)
    return (xf * lax.rsqrt(jnp.mean(xf * xf, -1, keepdims=True) + RMS_EPS) * w.astype(jnp.float32)).astype(x.dtype)


def l2_normalize(t):
    return t * lax.rsqrt(jnp.sum(t * t, -1, keepdims=True) + RMS_EPS)


def rope_rotate(t, cos, sin):
    half = t.shape[-1] // 2
    t1 = t[..., :half].astype(jnp.float32)
    t2 = t[..., half:].astype(jnp.float32)
    return jnp.concatenate([t1 * cos - t2 * sin, t1 * sin + t2 * cos], -1).astype(t.dtype)


def short_conv(x, w):
    c = x.shape[-1]
    y = lax.conv_general_dilated(x, w[:, None, :].astype(x.dtype), window_strides=(1,),
                                 padding=[(CONV_K // 2, CONV_K // 2)],
                                 dimension_numbers=('NWC', 'WIO', 'NWC'),
                                 feature_group_count=c)
    return jax.nn.silu(y)


def unit_lower_inverse(n_mat):
    c = n_mat.shape[-1]
    t_mat = jnp.eye(c, dtype=n_mat.dtype) + n_mat
    p_mat = n_mat
    for _ in range(int(np.log2(c)) - 1):
        p_mat = p_mat @ p_mat
        t_mat = t_mat + t_mat @ p_mat
    return t_mat


def chunk_gated_delta(q, k, v, g, beta):
    bsz, nh, s, dk = q.shape
    dv = v.shape[-1]
    n = s // CHUNK
    q = q.reshape(bsz, nh, n, CHUNK, dk)
    k = k.reshape(bsz, nh, n, CHUNK, dk)
    v = v.reshape(bsz, nh, n, CHUNK, dv)
    g = g.reshape(bsz, nh, n, CHUNK)
    beta = beta.reshape(bsz, nh, n, CHUNK)
    b = jnp.cumsum(g, axis=-1)
    idx = jnp.arange(CHUNK)
    incl = idx[:, None] >= idx[None, :]
    strict = idx[:, None] > idx[None, :]
    gamma = jnp.exp(jnp.where(incl, b[..., :, None] - b[..., None, :], -jnp.inf))
    kk = jnp.einsum('bhncd,bhnmd->bhncm', k, k)
    neg_a = jnp.where(strict, -beta[..., :, None] * kk * gamma, 0.0)
    t_inv = unit_lower_inverse(neg_a)
    w = t_inv @ (beta[..., None] * v)
    kc = t_inv @ ((beta * jnp.exp(b))[..., None] * k)
    p = jnp.einsum('bhncd,bhnmd->bhncm', q, k) * gamma
    qd = q * jnp.exp(b)[..., None]
    kd = k * jnp.exp(b[..., -1:] - b)[..., None]
    g_end = jnp.exp(b[..., -1])
    xs = (jnp.moveaxis(w, 2, 0), jnp.moveaxis(kc, 2, 0), jnp.moveaxis(p, 2, 0),
          jnp.moveaxis(qd, 2, 0), jnp.moveaxis(kd, 2, 0), jnp.moveaxis(g_end, 2, 0))

    def step(state, inp):
        w_c, kc_c, p_c, qd_c, kd_c, ge_c = inp
        u = w_c - kc_c @ state
        o = qd_c @ state + p_c @ u
        state = ge_c[..., None, None] * state + jnp.einsum('bhck,bhcv->bhkv', kd_c, u)
        return state, o

    s0 = jnp.zeros((bsz, nh, dk, dv), q.dtype)
    _, o = lax.scan(step, s0, xs)
    return jnp.moveaxis(o, 0, 2).reshape(bsz, nh, s, dv)


def gdn_branch(q, k, v, z, a, bt, conv_w, a_log, dt_bias, norm_w, w_o):
    dtype = q.dtype
    bsz, s, _ = q.shape
    f32 = jnp.float32
    qkv = short_conv(jnp.concatenate([q, k, v], -1), conv_w)
    q, k, v = jnp.split(qkv, [GDN_QK_W, 2 * GDN_QK_W], axis=-1)

    def heads(t, d):
        return jnp.swapaxes(t.reshape(bsz, s, GDN_HEADS, d), 1, 2).astype(f32)

    q = l2_normalize(heads(q, GDN_DK)) * (GDN_DK ** -0.5)
    k = l2_normalize(heads(k, GDN_DK))
    v = heads(v, GDN_DV)
    g = -jnp.exp(a_log.astype(f32).reshape(-1)) * jax.nn.softplus(a.astype(f32) + dt_bias.astype(f32).reshape(-1))
    beta = jax.nn.sigmoid(bt.astype(f32))
    g = jnp.swapaxes(g, 1, 2)
    beta = jnp.swapaxes(beta, 1, 2)

    def both_dirs(fwd, bwd):
        return jnp.concatenate([fwd, jnp.flip(bwd, axis=2)], axis=1)

    o2 = chunk_gated_delta(both_dirs(q, q), both_dirs(k, k), both_dirs(v, v),
                           both_dirs(g[:, :GDN_HEADS], g[:, GDN_HEADS:]),
                           both_dirs(beta[:, :GDN_HEADS], beta[:, GDN_HEADS:]))
    o = o2[:, :GDN_HEADS] + jnp.flip(o2[:, GDN_HEADS:], axis=2)
    o = jnp.swapaxes(o, 1, 2)
    o = rms_norm(o, norm_w) * jax.nn.silu(z.reshape(bsz, s, GDN_HEADS, GDN_DV).astype(f32))
    return o.reshape(bsz, s, GDN_V_W).astype(dtype) @ w_o


def mla_branch(c_q, c_kv, k_rope, cos, sin, q_norm_w, w_uq, kv_norm_w, w_ukv, w_o):
    bsz, s, _ = c_q.shape
    f32 = jnp.float32
    q = (rms_norm(c_q, q_norm_w) @ w_uq).reshape(bsz, s, MLA_HEADS, QK_NOPE + QK_ROPE)
    q_nope = q[..., :QK_NOPE]
    q_rope = rope_rotate(q[..., QK_NOPE:], cos[:, :, None, :], sin[:, :, None, :])
    kv = (rms_norm(c_kv, kv_norm_w) @ w_ukv).reshape(bsz, s, MLA_HEADS, QK_NOPE + V_HEAD)
    k_nope = kv[..., :QK_NOPE]
    v = kv[..., QK_NOPE:]
    k_rope = rope_rotate(k_rope, cos, sin)
    scale = (QK_NOPE + QK_ROPE) ** -0.5
    nb = s // Q_BLOCK

    def to_blocks(t):
        return jnp.swapaxes(t.reshape((bsz, nb, Q_BLOCK) + t.shape[2:]), 0, 1)

    def attend(blk):
        qn, qr = blk
        sc = (jnp.einsum('bqhd,bkhd->bhqk', qn, k_nope, preferred_element_type=f32)
              + jnp.einsum('bqhr,bkr->bhqk', qr, k_rope, preferred_element_type=f32))
        pr = jax.nn.softmax(sc * scale, axis=-1)
        return jnp.einsum('bhqk,bkhd->bqhd', pr.astype(v.dtype), v)

    o = lax.map(attend, (to_blocks(q_nope), to_blocks(q_rope)))
    o = jnp.swapaxes(o, 0, 1).reshape(bsz, s, MLA_HEADS * V_HEAD)
    return o @ w_o


def hybrid_mixer(h, cos, sin, w_in, conv_w, a_log, dt_bias, gdn_norm_w, w_o_gdn,
                 mla_q_norm_w, w_uq, mla_kv_norm_w, w_ukv, w_o_mla, w_out):
    proj = h @ w_in
    q, k, v, z, a, bt, c_q, c_kv, k_rope, gates = jnp.split(proj, np.cumsum(IN_WIDTHS)[:-1].tolist(), axis=-1)
    y_gdn = gdn_branch(q, k, v, z, a, bt, conv_w, a_log, dt_bias, gdn_norm_w, w_o_gdn)
    y_mla = mla_branch(c_q, c_kv, k_rope, cos, sin, mla_q_norm_w, w_uq, mla_kv_norm_w, w_ukv, w_o_mla)
    g_gdn, g_mla = jnp.split(jax.nn.sigmoid(gates.astype(jnp.float32)), 2, axis=-1)
    y = g_gdn * y_gdn + g_mla * y_mla
    return y.astype(h.dtype) @ w_out


def hier_moe(h, w_rg, b_rg, w_re, b_re, w_gate, w_up, w_down):
    bsz, s, d = h.shape
    f32 = jnp.float32
    t = bsz * s
    xt = h.reshape(t, d)
    tok_idx = jnp.arange(t)
    lg = (xt @ w_rg).astype(f32) + b_rg.astype(f32)
    grp = jnp.argmax(lg, axis=-1)
    p_grp = jax.nn.softmax(lg, axis=-1)[tok_idx, grp][:, None]
    le = ((xt @ w_re).astype(f32) + b_re.astype(f32)).reshape(t, N_GROUPS, EXPERTS_PER_GROUP)
    le = le[tok_idx, grp]
    top_l, top_i = lax.top_k(le, TOP_K_IN_GROUP)
    gate = p_grp * jax.nn.softmax(top_l, axis=-1)
    expert = grp[:, None] * EXPERTS_PER_GROUP + top_i
    n_assign = t * TOP_K_IN_GROUP
    e_flat = expert.reshape(n_assign)
    tok_flat = jnp.broadcast_to(tok_idx[:, None], (t, TOP_K_IN_GROUP)).reshape(n_assign)
    gate_flat = gate.reshape(n_assign)
    order = jnp.argsort(e_flat)
    e_sorted = e_flat[order]
    counts = jnp.bincount(e_flat, length=N_EXPERTS)
    padded = (counts + MOE_BLOCK - 1) // MOE_BLOCK * MOE_BLOCK
    pad_end = jnp.cumsum(padded)
    pad_start = pad_end - padded
    start = jnp.cumsum(counts) - counts
    dest = pad_start[e_sorted] + jnp.arange(n_assign) - start[e_sorted]
    n_blocks = -(-(n_assign + N_EXPERTS * (MOE_BLOCK - 1)) // MOE_BLOCK)
    n_rows = n_blocks * MOE_BLOCK
    row_tok = jnp.full((n_rows,), t, jnp.int32).at[dest].set(tok_flat[order].astype(jnp.int32))
    row_gate = jnp.zeros((n_rows,), f32).at[dest].set(gate_flat[order])
    blk_expert = jnp.minimum(jnp.searchsorted(pad_end, jnp.arange(n_blocks) * MOE_BLOCK, side='right'), N_EXPERTS - 1)
    xpad = jnp.concatenate([xt, jnp.zeros((1, d), xt.dtype)], axis=0)
    xb = xpad[row_tok].reshape(n_blocks, MOE_BLOCK, d)

    def expert_block(args):
        xblk, e = args
        hid = jax.nn.silu(xblk @ w_gate[e]) * (xblk @ w_up[e])
        return hid @ w_down[e]

    yb = lax.map(expert_block, (xb, blk_expert)).reshape(n_rows, d)
    y = jax.ops.segment_sum(yb.astype(f32) * row_gate[:, None], row_tok, num_segments=t + 1)[:t]
    return y.astype(h.dtype).reshape(bsz, s, d)


def setup_inputs(seed: int = 0) -> dict:
    key = jax.random.key(seed)
    k = jax.random.split(key, 32)
    f32 = jnp.float32
    L = DEPTH
    D = D_MODEL

    def normal(kk, shape, scale):
        return jax.random.normal(kk, shape, f32) * scale

    x = normal(k[0], (BATCH, SEQ, D), 1.0)
    positions = jnp.arange(SEQ, dtype=jnp.int32)[None, :] + jax.random.randint(k[1], (BATCH, 1), 0, 64, dtype=jnp.int32)
    ln_in_g = 1.0 + normal(k[2], (D,), 0.02)
    ln_in_b = normal(k[3], (D,), 0.02)
    col_scale = jnp.ones((IN_DIM,), f32).at[2 * GDN_QK_W:2 * GDN_QK_W + GDN_V_W].set(DN_BETA)
    w_in = normal(k[4], (L, D, IN_DIM), D ** -0.5) * col_scale
    conv_w = normal(k[5], (L, CONV_K, 2 * GDN_QK_W + GDN_V_W), CONV_K ** -0.5)
    a_log = jnp.log(jax.random.uniform(k[6], (L, 2, GDN_HEADS), f32, 1.0, 16.0))
    dt = jnp.exp(jax.random.uniform(k[7], (L, 2, GDN_HEADS), f32, np.log(1e-3), np.log(1e-1)))
    dt_bias = dt + jnp.log(-jnp.expm1(-dt))
    gdn_norm_w = 1.0 + normal(k[8], (L, GDN_DV), 0.02)
    w_o_gdn = normal(k[9], (L, GDN_V_W, D), GDN_V_W ** -0.5)
    mla_q_norm_w = 1.0 + normal(k[10], (L, Q_LORA), 0.02)
    w_uq = normal(k[11], (L, Q_LORA, MLA_Q_W), Q_LORA ** -0.5)
    mla_kv_norm_w = 1.0 + normal(k[12], (L, KV_LORA), 0.02)
    ukv_scale = jnp.tile(jnp.concatenate([jnp.ones((QK_NOPE,), f32), jnp.full((V_HEAD,), DN_BETA, f32)]), MLA_HEADS)
    w_ukv = normal(k[13], (L, KV_LORA, MLA_KV_W), KV_LORA ** -0.5) * ukv_scale
    w_o_mla = normal(k[14], (L, MLA_HEADS * V_HEAD, D), (MLA_HEADS * V_HEAD) ** -0.5)
    w_out = normal(k[15], (L, D, D), DN_BETA * D ** -0.5)
    ln1_g = 1.0 + normal(k[16], (L, D), 0.02)
    ln1_b = normal(k[17], (L, D), 0.02)
    w_router_group = normal(k[18], (L, D, N_GROUPS), D ** -0.5)
    b_router_group = normal(k[19], (L, N_GROUPS), 0.01)
    w_router_expert = normal(k[20], (L, D, N_EXPERTS), D ** -0.5)
    b_router_expert = normal(k[21], (L, N_EXPERTS), 0.01)
    w_gate = normal(k[22], (L, N_EXPERTS, D, D_EXPERT), D ** -0.5)
    w_up = normal(k[23], (L, N_EXPERTS, D, D_EXPERT), D ** -0.5)
    w_down = normal(k[24], (L, N_EXPERTS, D_EXPERT, D), DN_BETA * D_EXPERT ** -0.5)
    ln2_g = 1.0 + normal(k[25], (L, D), 0.02)
    ln2_b = normal(k[26], (L, D), 0.02)
    return {"x": x, "positions": positions, "ln_in_g": ln_in_g, "ln_in_b": ln_in_b,
            "w_in": w_in, "conv_w": conv_w, "a_log": a_log, "dt_bias": dt_bias,
            "gdn_norm_w": gdn_norm_w, "w_o_gdn": w_o_gdn, "mla_q_norm_w": mla_q_norm_w,
            "w_uq": w_uq, "mla_kv_norm_w": mla_kv_norm_w, "w_ukv": w_ukv, "w_o_mla": w_o_mla,
            "w_out": w_out, "ln1_g": ln1_g, "ln1_b": ln1_b,
            "w_router_group": w_router_group, "b_router_group": b_router_group,
            "w_router_expert": w_router_expert, "b_router_expert": b_router_expert,
            "w_gate": w_gate, "w_up": w_up, "w_down": w_down, "ln2_g": ln2_g, "ln2_b": ln2_b}


def reference(x, positions, ln_in_g, ln_in_b, w_in, conv_w, a_log, dt_bias, gdn_norm_w, w_o_gdn,
              mla_q_norm_w, w_uq, mla_kv_norm_w, w_ukv, w_o_mla, w_out, ln1_g, ln1_b,
              w_router_group, b_router_group, w_router_expert, b_router_expert,
              w_gate, w_up, w_down, ln2_g, ln2_b):
    f32 = jnp.float32
    half = QK_ROPE // 2
    inv_freq = jnp.power(ROPE_BASE, -jnp.arange(half, dtype=f32) / half)
    ang = positions.astype(f32)[..., None] * inv_freq
    cos, sin = jnp.cos(ang), jnp.sin(ang)
    h = layer_norm(x, ln_in_g, ln_in_b)
    for l in range(DEPTH):
        m = hybrid_mixer(h, cos, sin, w_in[l], conv_w[l], a_log[l], dt_bias[l], gdn_norm_w[l], w_o_gdn[l],
                         mla_q_norm_w[l], w_uq[l], mla_kv_norm_w[l], w_ukv[l], w_o_mla[l], w_out[l])
        h = layer_norm(DN_ALPHA * h + m, ln1_g[l], ln1_b[l])
        f = hier_moe(h, w_router_group[l], b_router_group[l], w_router_expert[l], b_router_expert[l],
                     w_gate[l], w_up[l], w_down[l])
        h = layer_norm(DN_ALPHA * h + f, ln2_g[l], ln2_b[l])
    return h
```

```python
import functools

import numpy as np
import jax
import jax.numpy as jnp
from jax import lax
from jax.experimental import pallas as pl
from jax.experimental.pallas import tpu as pltpu

F32 = jnp.float32
BF16 = jnp.bfloat16

D_MODEL = 1024
DEPTH = 2
GDN_HEADS = 8
GDN_DK = 128
GDN_DV = 128
CONV_K = 5
CHUNK = 64
MLA_HEADS = 8
Q_LORA = 384
KV_LORA = 256
QK_NOPE = 128
QK_ROPE = 64
V_HEAD = 128
ROPE_BASE = 10000.0
N_GROUPS = 8
EXPERTS_PER_GROUP = 8
N_EXPERTS = N_GROUPS * EXPERTS_PER_GROUP
TOP_K = 2
D_EXPERT = 512
MOE_BLOCK = 128
DN_ALPHA = (2 * DEPTH) ** 0.25
LN_EPS = 1e-5
RMS_EPS = 1e-6

LANES = 128
VMEM_LIMIT = 56 * 1024 * 1024

QKVZ_W = 4 * GDN_HEADS * GDN_DK
KAB_W = 2 * LANES
AB_W = LANES
GATES_W = 2 * D_MODEL
PROJ_SEGS = (QKVZ_W, Q_LORA, KV_LORA, KAB_W, AB_W, GATES_W)
PROJ_W = sum(PROJ_SEGS)


def _params(sem):
    return pltpu.CompilerParams(dimension_semantics=sem, vmem_limit_bytes=VMEM_LIMIT)


def _layer_norm(x, g, b):
    mu = jnp.mean(x, -1, keepdims=True)
    xc = x - mu
    var = jnp.mean(xc * xc, -1, keepdims=True)
    return xc * lax.rsqrt(var + LN_EPS) * g + b


def _rms_norm(x, w):
    return x * lax.rsqrt(jnp.mean(x * x, -1, keepdims=True) + RMS_EPS) * w


def _silu(x):
    return x / (1.0 + jnp.exp(-x))


def _sigmoid(x):
    return 1.0 / (1.0 + jnp.exp(-x))


def _dot(a, b):
    return jnp.dot(a, b, preferred_element_type=F32)


def _dot_nt(a, b):
    return lax.dot_general(a, b, (((1,), (1,)), ((), ())), preferred_element_type=F32)


def _dot_tn(a, b):
    return lax.dot_general(a, b, (((0,), (0,)), ((), ())), preferred_element_type=F32)


def _rope_kernel(pos_ref, freq_ref, mq_ref, mk1_ref, mk2_ref):
    ang = pos_ref[...].astype(F32) * freq_ref[...]
    c = jnp.cos(ang)
    s = jnp.sin(ang)
    quarter = lax.broadcasted_iota(jnp.int32, ang.shape, 1) >> 5
    even = (quarter & 1) == 0
    mq_ref[...] = jnp.where((quarter == 0) | (quarter == 3), c, s)
    mk1_ref[...] = jnp.where(even, c, s)
    mk2_ref[...] = jnp.where(even, s, c)


def _rope_tables(positions, tm):
    t = positions.size
    half = QK_ROPE // 2
    inv_freq = jnp.power(ROPE_BASE, -jnp.arange(half, dtype=F32) / half)
    freq = jnp.tile(inv_freq, LANES // half)[None, :]
    out = jax.ShapeDtypeStruct((t, LANES), F32)
    row = pl.BlockSpec((tm, LANES), lambda i: (i, 0))
    return pl.pallas_call(
        _rope_kernel, out_shape=(out, out, out), grid=(t // tm,),
        in_specs=[pl.BlockSpec((tm, 1), lambda i: (i, 0)), pl.BlockSpec((1, LANES), lambda i: (0, 0))],
        out_specs=(row, row, row), compiler_params=_params(("parallel",)), name="rope_tables",
    )(positions.reshape(t, 1), freq)


def _ln_kernel(x_ref, g_ref, b_ref, o_ref):
    o_ref[...] = _layer_norm(x_ref[...], g_ref[...], b_ref[...])


def _ln_in(x2, g, b, tm):
    t, d = x2.shape
    row = pl.BlockSpec((tm, d), lambda i: (i, 0))
    vec = pl.BlockSpec((1, d), lambda i: (0, 0))
    return pl.pallas_call(
        _ln_kernel, out_shape=jax.ShapeDtypeStruct((t, d), F32), grid=(t // tm,),
        in_specs=[row, vec, vec], out_specs=row, compiler_params=_params(("parallel",)), name="ln_in",
    )(x2, g.reshape(1, d), b.reshape(1, d))


PROJ_CHUNK = 512


def _proj_kernel(h_ref, w_ref, *out_refs):
    a = h_ref[...].astype(BF16)
    off = 0
    for ref, width in zip(out_refs, PROJ_SEGS):
        for c in range(0, width, PROJ_CHUNK):
            wc = min(PROJ_CHUNK, width - c)
            ref[:, c:c + wc] = _dot(a, w_ref[:, off + c:off + c + wc]).astype(ref.dtype)
        off += width


def _proj(h, w_all, tm):
    t, d = h.shape
    dts = (BF16, F32, F32, F32, F32, BF16)
    outs = tuple(jax.ShapeDtypeStruct((t, w), dt) for w, dt in zip(PROJ_SEGS, dts))
    return pl.pallas_call(
        _proj_kernel, out_shape=outs, grid=(t // tm,),
        in_specs=[pl.BlockSpec((tm, d), lambda i: (i, 0)), pl.BlockSpec((d, PROJ_W), lambda i: (0, 0))],
        out_specs=tuple(pl.BlockSpec((tm, w), lambda i: (i, 0)) for w in PROJ_SEGS),
        compiler_params=_params(("parallel",)), name="in_proj",
    )(h, w_all)


def _pack_w_in(w_in):
    hq = GDN_HEADS * GDN_DK
    o_a = 4 * hq
    o_bt = o_a + 2 * GDN_HEADS
    o_cq = o_bt + 2 * GDN_HEADS
    o_ckv = o_cq + Q_LORA
    o_kr = o_ckv + KV_LORA
    o_g = o_kr + QK_ROPE
    half = QK_ROPE // 2
    k1 = w_in[:, o_kr:o_kr + half]
    k2 = w_in[:, o_kr + half:o_g]
    d = w_in.shape[0]
    ab = jnp.concatenate([w_in[:, o_a:o_cq], jnp.zeros((d, AB_W - 4 * GDN_HEADS), w_in.dtype)], 1)
    return jnp.concatenate(
        [w_in[:, :o_a], w_in[:, o_cq:o_ckv], w_in[:, o_ckv:o_kr],
         k1, k1, k1, k1, -k2, k2, -k2, k2, ab, w_in[:, o_g:]], axis=1).astype(BF16)


def _gdn_kernel(q_ref, k_ref, v_ref, z_ref, ab_ref, cwq_ref, cwk_ref, cwv_ref, alog_ref, dtb_ref, nw_ref,
                o_ref, qs, ks, vs, gs, bs, ws, kqs, ps, kds, ges, os_):
    s_len = q_ref.shape[0]
    nc = s_len // CHUNK
    head = pl.program_id(1)
    row = lax.broadcasted_iota(jnp.int32, (s_len, LANES), 0)

    def conv_silu(x_ref, cw_ref):
        x = x_ref[...].astype(F32)
        acc = x * cw_ref[CONV_K // 2:CONV_K // 2 + 1, :]
        for j in range(CONV_K):
            sh = CONV_K // 2 - j
            if sh == 0:
                continue
            xs = pltpu.roll(x, sh % s_len, axis=0)
            valid = (row >= sh) if sh > 0 else (row < s_len + sh)
            acc = acc + jnp.where(valid, xs, 0.0) * cw_ref[j:j + 1, :]
        return _silu(acc)

    def l2n(x):
        return x * lax.rsqrt(jnp.sum(x * x, -1, keepdims=True) + RMS_EPS)

    qs[...] = l2n(conv_silu(q_ref, cwq_ref)) * (GDN_DK ** -0.5)
    ks[...] = l2n(conv_silu(k_ref, cwk_ref))
    vs[...] = conv_silu(v_ref, cwv_ref)

    ab = ab_ref[...]
    x = ab + dtb_ref[...]
    softplus = jnp.maximum(x, 0.0) + jnp.log(1.0 + jnp.exp(-jnp.abs(x)))
    g_all = -jnp.exp(alog_ref[...]) * softplus
    b_all = _sigmoid(ab)
    lane = lax.broadcasted_iota(jnp.int32, (1, LANES), 1)

    def pick(vals, idx):
        col = jnp.sum(jnp.where(lane == idx, vals, 0.0), -1, keepdims=True)
        return jnp.broadcast_to(col, (s_len, LANES))

    for d in range(2):
        gs[d] = pick(g_all, head + d * GDN_HEADS)
        bs[d] = pick(b_all, head + (2 + d) * GDN_HEADS)

    ci = lax.broadcasted_iota(jnp.int32, (CHUNK, CHUNK), 0)
    cj = lax.broadcasted_iota(jnp.int32, (CHUNK, CHUNK), 1)
    eye = jnp.where(ci == cj, 1.0, 0.0)
    incl = (ci >= cj, ci <= cj)
    strict = (ci > cj, ci < cj)
    tri = tuple(jnp.where(m, 1.0, 0.0).astype(BF16) for m in incl)
    last_row = (CHUNK - 1, 0)

    def prep(c, carry):
        r0 = pl.multiple_of(c * CHUNK, CHUNK)
        rows = pl.ds(r0, CHUNK)
        q = qs[rows, :]
        k = ks[rows, :]
        v = vs[rows, :]
        kb = k.astype(BF16)
        qb = q.astype(BF16)
        kk = _dot_nt(kb, kb)
        qk = _dot_nt(qb, kb)
        for d in range(2):
            g = gs[d, rows, :]
            beta = bs[d, rows, :]
            g1 = g.astype(BF16)
            r1 = g - g1.astype(F32)
            g2 = r1.astype(BF16)
            g3 = (r1 - g2.astype(F32)).astype(BF16)
            b = _dot(tri[d], g1) + _dot(tri[d], g2) + _dot(tri[d], g3)
            b_row = jnp.transpose(b)[:CHUNK, :]
            gamma = jnp.exp(jnp.where(incl[d], b[:, :CHUNK] - b_row, -jnp.inf))
            n_mat = jnp.where(strict[d], -beta[:, :CHUNK] * kk * gamma, 0.0)
            t_mat = eye + n_mat
            p_mat = n_mat
            for _ in range(int(np.log2(CHUNK)) - 1):
                pb = p_mat.astype(BF16)
                p_mat = _dot(pb, pb)
                t_mat = t_mat + _dot(t_mat.astype(BF16), p_mat.astype(BF16))
            eb = jnp.exp(b)
            rhs = jnp.concatenate([beta * v, beta * eb * k], axis=1).astype(BF16)
            wk = _dot(t_mat.astype(BF16), rhs)
            lr = last_row[d]
            b_last = b[lr:lr + 1, :]
            ws[d, rows, :] = wk[:, :GDN_DV]
            r2 = pl.multiple_of(c * 2 * CHUNK, 2 * CHUNK)
            kqs[d, pl.ds(r2, CHUNK), :] = wk[:, GDN_DV:].astype(BF16)
            kqs[d, pl.ds(r2 + CHUNK, CHUNK), :] = (q * eb).astype(BF16)
            ps[d, rows, :] = (qk * gamma).astype(BF16)
            kds[d, rows, :] = (k * jnp.exp(b_last - b)).astype(BF16)
            r8 = pl.multiple_of(c * 8, 8)
            ges[d, pl.ds(r8, 8), :] = jnp.broadcast_to(jnp.exp(b_last), (8, LANES))
        return carry

    lax.fori_loop(0, nc, prep, 0)

    os_[...] = jnp.zeros_like(os_)

    def step(i, states):
        new = []
        for d in range(2):
            c = i if d == 0 else nc - 1 - i
            state = states[d]
            rows = pl.ds(pl.multiple_of(c * CHUNK, CHUNK), CHUNK)
            kq = kqs[d, pl.ds(pl.multiple_of(c * 2 * CHUNK, 2 * CHUNK), 2 * CHUNK), :]
            r = _dot(kq, state.astype(BF16))
            u = ws[d, rows, :] - r[:CHUNK]
            ub = u.astype(BF16)
            os_[rows, :] += r[CHUNK:] + _dot(ps[d, rows, :], ub)
            ge = ges[d, pl.ds(pl.multiple_of(c * 8, 8), 1), :]
            new.append(ge * state + _dot_tn(kds[d, rows, :], ub))
        return tuple(new)

    s0 = jnp.zeros((GDN_DK, GDN_DV), F32)
    lax.fori_loop(0, nc, step, (s0, s0))

    o_ref[...] = (_rms_norm(os_[...], nw_ref[...]) * _silu(z_ref[...].astype(F32))).astype(o_ref.dtype)


def _gdn(qkvz, ab, conv_w, a_log, dt_bias, norm_w, bsz, s_len):
    nh = GDN_HEADS
    qkvz3 = qkvz.reshape(bsz, s_len, QKVZ_W)
    ab3 = ab.reshape(bsz, s_len, AB_W)
    pad = jnp.zeros((AB_W - 2 * nh,), F32)
    alog = jnp.concatenate([a_log.reshape(-1), pad])[None, :]
    dtb = jnp.concatenate([dt_bias.reshape(-1), pad])[None, :]

    def col(k):
        return pl.BlockSpec((None, s_len, LANES), lambda b, h: (b, 0, k * nh + h))

    def cw(k):
        return pl.BlockSpec((CONV_K, LANES), lambda b, h: (0, k * nh + h))

    vec = pl.BlockSpec((1, LANES), lambda b, h: (0, 0))
    nc = s_len // CHUNK
    scratch = [
        pltpu.VMEM((s_len, LANES), F32), pltpu.VMEM((s_len, LANES), F32), pltpu.VMEM((s_len, LANES), F32),
        pltpu.VMEM((2, s_len, LANES), F32), pltpu.VMEM((2, s_len, LANES), F32),
        pltpu.VMEM((2, s_len, GDN_DV), F32), pltpu.VMEM((2, 2 * s_len, LANES), BF16),
        pltpu.VMEM((2, s_len, CHUNK), BF16), pltpu.VMEM((2, s_len, LANES), BF16),
        pltpu.VMEM((2, nc * 8, LANES), F32), pltpu.VMEM((s_len, GDN_DV), F32),
    ]
    out = pl.pallas_call(
        _gdn_kernel, out_shape=jax.ShapeDtypeStruct((bsz, s_len, nh * GDN_DV), BF16), grid=(bsz, nh),
        in_specs=[col(0), col(1), col(2), col(3),
                  pl.BlockSpec((None, s_len, AB_W), lambda b, h: (b, 0, 0)),
                  cw(0), cw(1), cw(2), vec, vec, vec],
        out_specs=pl.BlockSpec((None, s_len, GDN_DV), lambda b, h: (b, 0, h)),
        scratch_shapes=scratch, compiler_params=_params(("parallel", "parallel")), name="gdn",
    )(qkvz3, qkvz3, qkvz3, qkvz3, ab3, conv_w, conv_w, conv_w, alog, dtb, norm_w.reshape(1, GDN_DV))
    return out.reshape(bsz * s_len, nh * GDN_DV)


MLA_TQ = 256


def _mla_kernel(cq_ref, ckv_ref, kab_ref, mq_ref, mk1_ref, mk2_ref, qnw_ref, kvnw_ref, wq_ref, wkv_ref,
                o_ref, q_s, k_s, v_s):
    s_len = cq_ref.shape[0]
    scale = (QK_NOPE + QK_ROPE) ** -0.5
    cqn = _rms_norm(cq_ref[...], qnw_ref[...]).astype(BF16)
    q = _dot(cqn, wq_ref[...]) * scale
    q_s[:, :QK_NOPE] = q[:, :QK_NOPE].astype(BF16)
    q_s[:, QK_NOPE:] = (q[:, QK_NOPE:] * mq_ref[...]).astype(BF16)
    ckvn = _rms_norm(ckv_ref[...], kvnw_ref[...]).astype(BF16)
    kv = _dot(ckvn, wkv_ref[...])
    k_s[:, :QK_NOPE] = kv[:, :QK_NOPE].astype(BF16)
    kab = kab_ref[...]
    k_s[:, QK_NOPE:] = (kab[:, :LANES] * mk1_ref[...] + kab[:, LANES:] * mk2_ref[...]).astype(BF16)
    v_s[...] = kv[:, QK_NOPE:].astype(BF16)

    tq = min(MLA_TQ, s_len)

    def qblock(i, carry):
        rows = pl.ds(pl.multiple_of(i * tq, tq), tq)
        sc = _dot_nt(q_s[rows, :], k_s[...])
        p = jnp.exp(sc - jnp.max(sc, -1, keepdims=True))
        den = jnp.sum(p, -1, keepdims=True)
        o_ref[rows, :] = (_dot(p.astype(BF16), v_s[...]) / den).astype(o_ref.dtype)
        return carry

    lax.fori_loop(0, s_len // tq, qblock, 0)


def _pack_w_uq(w_uq):
    w = w_uq.reshape(Q_LORA, MLA_HEADS, QK_NOPE + QK_ROPE)
    half = QK_ROPE // 2
    r1 = w[..., QK_NOPE:QK_NOPE + half]
    r2 = w[..., QK_NOPE + half:]
    w = jnp.concatenate([w[..., :QK_NOPE], r1, r1, -r2, r2], axis=-1)
    return w.reshape(Q_LORA, MLA_HEADS * 2 * LANES).astype(BF16)


def _mla(cq, ckv, kab, tables, qnw, kvnw, wq, wkv, bsz, s_len):
    nh = MLA_HEADS

    def per_b(width):
        return pl.BlockSpec((None, s_len, width), lambda b, h: (b, 0, 0))

    def vec(width):
        return pl.BlockSpec((1, width), lambda b, h: (0, 0))

    mq, mk1, mk2 = (t.reshape(bsz, s_len, LANES) for t in tables)
    out = pl.pallas_call(
        _mla_kernel, out_shape=jax.ShapeDtypeStruct((bsz, s_len, nh * V_HEAD), BF16), grid=(bsz, nh),
        in_specs=[per_b(Q_LORA), per_b(KV_LORA), per_b(KAB_W), per_b(LANES), per_b(LANES), per_b(LANES),
                  vec(Q_LORA), vec(KV_LORA),
                  pl.BlockSpec((Q_LORA, 2 * LANES), lambda b, h: (0, h)),
                  pl.BlockSpec((KV_LORA, QK_NOPE + V_HEAD), lambda b, h: (0, h))],
        out_specs=pl.BlockSpec((None, s_len, V_HEAD), lambda b, h: (b, 0, h)),
        scratch_shapes=[pltpu.VMEM((s_len, 2 * LANES), BF16), pltpu.VMEM((s_len, 2 * LANES), BF16),
                        pltpu.VMEM((s_len, V_HEAD), BF16)],
        compiler_params=_params(("parallel", "parallel")), name="mla",
    )(cq.reshape(bsz, s_len, Q_LORA), ckv.reshape(bsz, s_len, KV_LORA), kab.reshape(bsz, s_len, KAB_W),
      mq, mk1, mk2, qnw.reshape(1, Q_LORA), kvnw.reshape(1, KV_LORA), wq, wkv)
    return out.reshape(bsz * s_len, nh * V_HEAD)


def _tail_kernel(og_ref, om_ref, gates_ref, h_ref, wog_ref, wom_ref, wout_ref, g_ref, b_ref, o_ref):
    d = h_ref.shape[1]
    y_gdn = _dot(og_ref[...], wog_ref[...])
    y_mla = _dot(om_ref[...], wom_ref[...])
    y = (_sigmoid(gates_ref[:, :d].astype(F32)) * y_gdn + _sigmoid(gates_ref[:, d:].astype(F32)) * y_mla)
    m = _dot(y.astype(BF16), wout_ref[...])
    o_ref[...] = _layer_norm(DN_ALPHA * h_ref[...] + m, g_ref[...], b_ref[...])


def _mixer_tail(og, om, gates, h, wog, wom, wout, g, b, tm):
    t, d = h.shape
    row = lambda w: pl.BlockSpec((tm, w), lambda i: (i, 0))
    full = pl.BlockSpec((d, d), lambda i: (0, 0))
    vec = pl.BlockSpec((1, d), lambda i: (0, 0))
    return pl.pallas_call(
        _tail_kernel, out_shape=jax.ShapeDtypeStruct((t, d), F32), grid=(t // tm,),
        in_specs=[row(d), row(d), row(2 * d), row(d), full, full, full, vec, vec],
        out_specs=row(d), compiler_params=_params(("parallel",)), name="mixer_tail",
    )(og, om, gates, h, wog, wom, wout, g.reshape(1, d), b.reshape(1, d))


def _router_kernel(h_ref, w_ref, b_ref, eid_ref, gate_ref):
    logits = jnp.dot(h_ref[...], w_ref[...], precision=lax.Precision.HIGHEST,
                     preferred_element_type=F32) + b_ref[...]
    lane = lax.broadcasted_iota(jnp.int32, logits.shape, 1)
    lane_f = lane.astype(F32)
    neg = -jnp.inf

    def first_max(vals):
        m = jnp.max(vals, -1, keepdims=True)
        idx = jnp.min(jnp.where(vals == m, lane_f, float(LANES)), -1, keepdims=True)
        return m, idx

    is_grp = lane < N_GROUPS
    mg, grp = first_max(jnp.where(is_grp, logits, neg))
    p_grp = 1.0 / jnp.sum(jnp.where(is_grp, jnp.exp(logits - mg), 0.0), -1, keepdims=True)
    in_grp = ((lane - N_GROUPS) >> 3).astype(F32) == grp
    le = jnp.where(in_grp, logits, neg)
    l1, i1 = first_max(le)
    l2, i2 = first_max(jnp.where(lane_f == i1, neg, le))
    t = jnp.exp(l2 - l1)
    g1 = p_grp / (1.0 + t)
    g2 = p_grp * t / (1.0 + t)
    e1 = i1 - float(N_GROUPS)
    e2 = i2 - float(N_GROUPS)
    eid_ref[...] = jnp.where(lane == 0, e1, jnp.where(lane == 1, e2, 0.0)).astype(jnp.int32)
    gate_ref[...] = jnp.where(lane == 0, g1, jnp.where(lane == 1, g2, 0.0))


def _router(h, w_rg, b_rg, w_re, b_re, tm):
    t, d = h.shape
    padw = LANES - N_GROUPS - N_EXPERTS
    w = jnp.concatenate([w_rg, w_re, jnp.zeros((d, padw), F32)], axis=1)
    b = jnp.concatenate([b_rg, b_re, jnp.zeros((padw,), F32)])[None, :]
    row = pl.BlockSpec((tm, LANES), lambda i: (i, 0))
    return pl.pallas_call(
        _router_kernel,
        out_shape=(jax.ShapeDtypeStruct((t, LANES), jnp.int32), jax.ShapeDtypeStruct((t, LANES), F32)),
        grid=(t // tm,),
        in_specs=[pl.BlockSpec((tm, d), lambda i: (i, 0)), pl.BlockSpec((d, LANES), lambda i: (0, 0)),
                  pl.BlockSpec((1, LANES), lambda i: (0, 0))],
        out_specs=(row, row), compiler_params=_params(("parallel",)), name="router",
    )(h, w, b)


def _expert_kernel(be_ref, x_ref, gate_ref, wg_ref, wu_ref, wd_ref, o_ref):
    del be_ref
    x = x_ref[...]
    hid = _silu(_dot(x, wg_ref[...].astype(BF16))) * _dot(x, wu_ref[...].astype(BF16))
    o_ref[...] = _dot(hid.astype(BF16), wd_ref[...].astype(BF16)) * gate_ref[...]


def _experts(xb, row_gate, blk_expert, w_gate, w_up, w_down, layer):
    n_rows, d = xb.shape
    n_blocks = n_rows // MOE_BLOCK
    grid_spec = pltpu.PrefetchScalarGridSpec(
        num_scalar_prefetch=1, grid=(n_blocks,),
        in_specs=[pl.BlockSpec((MOE_BLOCK, d), lambda i, be: (i, 0)),
                  pl.BlockSpec((MOE_BLOCK, 1), lambda i, be: (i, 0)),
                  pl.BlockSpec((None, None, d, D_EXPERT), lambda i, be: (layer, be[i], 0, 0)),
                  pl.BlockSpec((None, None, d, D_EXPERT), lambda i, be: (layer, be[i], 0, 0)),
                  pl.BlockSpec((None, None, D_EXPERT, d), lambda i, be: (layer, be[i], 0, 0))],
        out_specs=pl.BlockSpec((MOE_BLOCK, d), lambda i, be: (i, 0)))
    return pl.pallas_call(
        _expert_kernel, out_shape=jax.ShapeDtypeStruct((n_rows, d), F32), grid_spec=grid_spec,
        compiler_params=_params(("arbitrary",)), name="experts",
    )(blk_expert, xb, row_gate, w_gate, w_up, w_down)


def _combine_kernel(h_ref, y1_ref, y2_ref, g_ref, b_ref, o_ref):
    o_ref[...] = _layer_norm(DN_ALPHA * h_ref[...] + (y1_ref[...] + y2_ref[...]), g_ref[...], b_ref[...])


def _combine(h, y1, y2, g, b, tm):
    t, d = h.shape
    row = pl.BlockSpec((tm, d), lambda i: (i, 0))
    vec = pl.BlockSpec((1, d), lambda i: (0, 0))
    return pl.pallas_call(
        _combine_kernel, out_shape=jax.ShapeDtypeStruct((t, d), F32), grid=(t // tm,),
        in_specs=[row, row, row, vec, vec], out_specs=row,
        compiler_params=_params(("parallel",)), name="combine_ln",
    )(h, y1, y2, g.reshape(1, d), b.reshape(1, d))


def _dispatch(eid, gate):
    t = eid.shape[0]
    n_assign = t * TOP_K
    e_flat = eid.reshape(n_assign)
    order = jnp.argsort(e_flat)
    e_sorted = e_flat[order]
    counts = jnp.bincount(e_flat, length=N_EXPERTS)
    padded = (counts + MOE_BLOCK - 1) // MOE_BLOCK * MOE_BLOCK
    pad_end = jnp.cumsum(padded)
    pad_start = pad_end - padded
    start = jnp.cumsum(counts) - counts
    dest_sorted = (pad_start[e_sorted] + jnp.arange(n_assign) - start[e_sorted]).astype(jnp.int32)
    n_blocks = -(-(n_assign + N_EXPERTS * (MOE_BLOCK - 1)) // MOE_BLOCK)
    n_rows = n_blocks * MOE_BLOCK
    row_tok = jnp.zeros((n_rows,), jnp.int32).at[dest_sorted].set((order // TOP_K).astype(jnp.int32))
    row_gate = jnp.zeros((n_rows,), F32).at[dest_sorted].set(gate.reshape(n_assign)[order])
    dest = jnp.zeros((n_assign,), jnp.int32).at[order].set(dest_sorted).reshape(t, TOP_K)
    blk_expert = jnp.minimum(
        jnp.searchsorted(pad_end, jnp.arange(n_blocks) * MOE_BLOCK, side='right'), N_EXPERTS - 1).astype(jnp.int32)
    return row_tok, row_gate, dest, blk_expert


def _moe(h, w_rg, b_rg, w_re, b_re, w_gate, w_up, w_down, layer, g, b, tm):
    eid, gate = _router(h, w_rg, b_rg, w_re, b_re, tm)
    row_tok, row_gate, dest, blk_expert = _dispatch(eid[:, :TOP_K], gate[:, :TOP_K])
    xb = jnp.take(h.astype(BF16), row_tok, axis=0)
    yb = _experts(xb, row_gate[:, None], blk_expert, w_gate, w_up, w_down, layer)
    y1 = jnp.take(yb, dest[:, 0], axis=0)
    y2 = jnp.take(yb, dest[:, 1], axis=0)
    return _combine(h, y1, y2, g, b, tm)


def kernel(x, positions, ln_in_g, ln_in_b, w_in, conv_w, a_log, dt_bias, gdn_norm_w, w_o_gdn, mla_q_norm_w, w_uq, mla_kv_norm_w, w_ukv, w_o_mla, w_out, ln1_g, ln1_b, w_router_group, b_router_group, w_router_expert, b_router_expert, w_gate, w_up, w_down, ln2_g, ln2_b):
    bsz, s_len, d = x.shape
    t = bsz * s_len
    tm = min(256, t)
    tables = _rope_tables(positions, tm)
    h = _ln_in(x.reshape(t, d), ln_in_g, ln_in_b, tm)
    for l in range(DEPTH):
        qkvz, cq, ckv, kab, ab, gates = _proj(h, _pack_w_in(w_in[l]), tm)
        og = _gdn(qkvz, ab, conv_w[l], a_log[l], dt_bias[l], gdn_norm_w[l], bsz, s_len)
        om = _mla(cq, ckv, kab, tables, mla_q_norm_w[l], mla_kv_norm_w[l],
                  _pack_w_uq(w_uq[l]), w_ukv[l].astype(BF16), bsz, s_len)
        h = _mixer_tail(og, om, gates, h, w_o_gdn[l].astype(BF16), w_o_mla[l].astype(BF16),
                        w_out[l].astype(BF16), ln1_g[l], ln1_b[l], tm)
        h = _moe(h, w_router_group[l], b_router_group[l], w_router_expert[l], b_router_expert[l],
                 w_gate, w_up, w_down, l, ln2_g[l], ln2_b[l], tm)
    return h.reshape(bsz, s_len, d)
```

```python
import functools

import numpy as np
import jax
import jax.numpy as jnp
from jax import lax
from jax.experimental import pallas as pl
from jax.experimental.pallas import tpu as pltpu

F32 = jnp.float32
BF16 = jnp.bfloat16

D_MODEL = 1024
DEPTH = 2
GDN_HEADS = 8
GDN_DK = 128
GDN_DV = 128
CONV_K = 5
CHUNK = 64
MLA_HEADS = 8
Q_LORA = 384
KV_LORA = 256
QK_NOPE = 128
QK_ROPE = 64
V_HEAD = 128
ROPE_BASE = 10000.0
N_GROUPS = 8
EXPERTS_PER_GROUP = 8
N_EXPERTS = N_GROUPS * EXPERTS_PER_GROUP
TOP_K = 2
D_EXPERT = 512
MOE_BLOCK = 128
DN_ALPHA = (2 * DEPTH) ** 0.25
LN_EPS = 1e-5
RMS_EPS = 1e-6

LANES = 128
VMEM_LIMIT = 56 * 1024 * 1024

QKVZ_W = 4 * GDN_HEADS * GDN_DK
KAB_W = 2 * LANES
AB_W = LANES
GATES_W = 2 * D_MODEL
PROJ_SEGS = (QKVZ_W, Q_LORA, KV_LORA, KAB_W, AB_W, GATES_W)
PROJ_W = sum(PROJ_SEGS)


def _params(sem):
    return pltpu.CompilerParams(dimension_semantics=sem, vmem_limit_bytes=VMEM_LIMIT)


def _layer_norm(x, g, b):
    mu = jnp.mean(x, -1, keepdims=True)
    xc = x - mu
    var = jnp.mean(xc * xc, -1, keepdims=True)
    return xc * lax.rsqrt(var + LN_EPS) * g + b


def _rms_norm(x, w):
    return x * lax.rsqrt(jnp.mean(x * x, -1, keepdims=True) + RMS_EPS) * w


def _silu(x):
    return x / (1.0 + jnp.exp(-x))


def _sigmoid(x):
    return 1.0 / (1.0 + jnp.exp(-x))


def _dot(a, b):
    return jnp.dot(a, b, preferred_element_type=F32)


def _dot_nt(a, b):
    return lax.dot_general(a, b, (((1,), (1,)), ((), ())), preferred_element_type=F32)


def _ds(i, n):
    return pl.ds(i * n, n) if isinstance(i, int) else pl.ds(pl.multiple_of(i * n, n), n)


def _dot_tn(a, b):
    return lax.dot_general(a, b, (((0,), (0,)), ((), ())), preferred_element_type=F32)


def _rope_kernel(pos_ref, freq_ref, mq_ref, mk1_ref, mk2_ref):
    ang = pos_ref[...].astype(F32) * freq_ref[...]
    c = jnp.cos(ang)
    s = jnp.sin(ang)
    quarter = lax.broadcasted_iota(jnp.int32, ang.shape, 1) >> 5
    even = (quarter & 1) == 0
    mq_ref[...] = jnp.where((quarter == 0) | (quarter == 3), c, s)
    mk1_ref[...] = jnp.where(even, c, s)
    mk2_ref[...] = jnp.where(even, s, c)


def _rope_tables(positions, tm):
    t = positions.size
    half = QK_ROPE // 2
    inv_freq = jnp.power(ROPE_BASE, -jnp.arange(half, dtype=F32) / half)
    freq = jnp.tile(inv_freq, LANES // half)[None, :]
    out = jax.ShapeDtypeStruct((t, LANES), F32)
    row = pl.BlockSpec((tm, LANES), lambda i: (i, 0))
    return pl.pallas_call(
        _rope_kernel, out_shape=(out, out, out), grid=(t // tm,),
        in_specs=[pl.BlockSpec((tm, 1), lambda i: (i, 0)), pl.BlockSpec((1, LANES), lambda i: (0, 0))],
        out_specs=(row, row, row), compiler_params=_params(("parallel",)), name="rope_tables",
    )(positions.reshape(t, 1), freq)


def _ln_kernel(x_ref, g_ref, b_ref, o_ref):
    o_ref[...] = _layer_norm(x_ref[...], g_ref[...], b_ref[...])


def _ln_in(x2, g, b, tm):
    t, d = x2.shape
    row = pl.BlockSpec((tm, d), lambda i: (i, 0))
    vec = pl.BlockSpec((1, d), lambda i: (0, 0))
    return pl.pallas_call(
        _ln_kernel, out_shape=jax.ShapeDtypeStruct((t, d), F32), grid=(t // tm,),
        in_specs=[row, vec, vec], out_specs=row, compiler_params=_params(("parallel",)), name="ln_in",
    )(x2, g.reshape(1, d), b.reshape(1, d))


PROJ_CHUNK = 512


def _proj_kernel(h_ref, w_ref, *out_refs):
    a = h_ref[...].astype(BF16)
    off = 0
    for ref, width in zip(out_refs, PROJ_SEGS):
        for c in range(0, width, PROJ_CHUNK):
            wc = min(PROJ_CHUNK, width - c)
            ref[:, c:c + wc] = _dot(a, w_ref[:, off + c:off + c + wc]).astype(ref.dtype)
        off += width


def _proj(h, w_all, tm):
    t, d = h.shape
    dts = (BF16, F32, F32, F32, F32, BF16)
    outs = tuple(jax.ShapeDtypeStruct((t, w), dt) for w, dt in zip(PROJ_SEGS, dts))
    return pl.pallas_call(
        _proj_kernel, out_shape=outs, grid=(t // tm,),
        in_specs=[pl.BlockSpec((tm, d), lambda i: (i, 0)), pl.BlockSpec((d, PROJ_W), lambda i: (0, 0))],
        out_specs=tuple(pl.BlockSpec((tm, w), lambda i: (i, 0)) for w in PROJ_SEGS),
        compiler_params=_params(("parallel",)), name="in_proj",
    )(h, w_all)


def _pack_w_in(w_in):
    hq = GDN_HEADS * GDN_DK
    o_a = 4 * hq
    o_bt = o_a + 2 * GDN_HEADS
    o_cq = o_bt + 2 * GDN_HEADS
    o_ckv = o_cq + Q_LORA
    o_kr = o_ckv + KV_LORA
    o_g = o_kr + QK_ROPE
    half = QK_ROPE // 2
    k1 = w_in[:, o_kr:o_kr + half]
    k2 = w_in[:, o_kr + half:o_g]
    d = w_in.shape[0]
    ab = jnp.concatenate([w_in[:, o_a:o_cq], jnp.zeros((d, AB_W - 4 * GDN_HEADS), w_in.dtype)], 1)
    return jnp.concatenate(
        [w_in[:, :o_a], w_in[:, o_cq:o_ckv], w_in[:, o_ckv:o_kr],
         k1, k1, k1, k1, -k2, k2, -k2, k2, ab, w_in[:, o_g:]], axis=1).astype(BF16)


def _gdn_kernel(q_ref, k_ref, v_ref, z_ref, ab_ref, cwq_ref, cwk_ref, cwv_ref, alog_ref, dtb_ref, nw_ref,
                o_ref, qs, ks, vs, gs, bs, brs, bns, mqs, ges, os_):
    s_len = q_ref.shape[0]
    nc = s_len // CHUNK
    head = pl.program_id(1)
    row = lax.broadcasted_iota(jnp.int32, (s_len, LANES), 0)

    def conv_silu(x_ref, cw_ref):
        x = x_ref[...].astype(F32)
        acc = x * cw_ref[CONV_K // 2:CONV_K // 2 + 1, :]
        for j in range(CONV_K):
            sh = CONV_K // 2 - j
            if sh == 0:
                continue
            xs = pltpu.roll(x, sh % s_len, axis=0)
            valid = (row >= sh) if sh > 0 else (row < s_len + sh)
            acc = acc + jnp.where(valid, xs, 0.0) * cw_ref[j:j + 1, :]
        return _silu(acc)

    def l2n(x):
        return x * lax.rsqrt(jnp.sum(x * x, -1, keepdims=True) + RMS_EPS)

    qs[...] = l2n(conv_silu(q_ref, cwq_ref)) * (GDN_DK ** -0.5)
    ks[...] = l2n(conv_silu(k_ref, cwk_ref))
    vs[...] = conv_silu(v_ref, cwv_ref)

    ab = ab_ref[...]
    x = ab + dtb_ref[...]
    softplus = jnp.maximum(x, 0.0) + jnp.log(1.0 + jnp.exp(-jnp.abs(x)))
    g_all = -jnp.exp(alog_ref[...]) * softplus
    b_all = _sigmoid(ab)
    lane = lax.broadcasted_iota(jnp.int32, (1, LANES), 1)

    def pick(vals, idx):
        col = jnp.sum(jnp.where(lane == idx, vals, 0.0), -1, keepdims=True)
        return jnp.broadcast_to(col, (s_len, LANES))

    for d in range(2):
        gs[d] = pick(g_all, head + d * GDN_HEADS)
        bs[d] = pick(b_all, head + (2 + d) * GDN_HEADS)

    ci = lax.broadcasted_iota(jnp.int32, (CHUNK, CHUNK), 0)
    cj = lax.broadcasted_iota(jnp.int32, (CHUNK, CHUNK), 1)
    eye = jnp.where(ci == cj, 1.0, 0.0)
    incl = (ci >= cj, ci <= cj)
    strict = (ci > cj, ci < cj)
    tri = tuple(jnp.where(m, 1.0, 0.0).astype(BF16) for m in incl)
    last_row = (CHUNK - 1, 0)

    unroll = 4 if nc % 4 == 0 else 1

    def cumulate(c, carry):
        rows = [pl.ds(pl.multiple_of((c * unroll + j) * CHUNK, CHUNK), CHUNK) for j in range(unroll)]
        chains = [(j, d) for j in range(unroll) for d in range(2)]
        parts = []
        for j, d in chains:
            g = gs[d, rows[j], :]
            g1 = g.astype(BF16)
            r1 = g - g1.astype(F32)
            g2 = r1.astype(BF16)
            parts.append((g1, g2, (r1 - g2.astype(F32)).astype(BF16)))
        b = [_dot(tri[d], p[0]) + _dot(tri[d], p[1]) + _dot(tri[d], p[2])
             for (j, d), p in zip(chains, parts)]
        for (j, d), x in zip(chains, b):
            gs[d, rows[j], :] = x
            brs[d, rows[j], :] = jnp.transpose(x)[:CHUNK, :]
        return carry

    lax.fori_loop(0, nc // unroll, cumulate, 0)

    def prep(it, carry):
        cs = [it * unroll + j for j in range(unroll)]
        rows = [pl.ds(pl.multiple_of(c * CHUNK, CHUNK), CHUNK) for c in cs]
        q = [qs[r, :] for r in rows]
        k = [ks[r, :] for r in rows]
        kb = [x.astype(BF16) for x in k]
        kk = [_dot_nt(x, x) for x in kb]
        qk = [_dot_nt(x.astype(BF16), y) for x, y in zip(q, kb)]
        chains = [(j, d) for j in range(unroll) for d in range(2)]
        b = [gs[d, rows[j], :] for j, d in chains]
        beta = [bs[d, rows[j], :] for j, d in chains]
        gamma = [jnp.exp(jnp.where(incl[d], b[n][:, :CHUNK] - brs[d, rows[j], :], -jnp.inf))
                 for n, (j, d) in enumerate(chains)]
        p_mat = [jnp.where(strict[d], -beta[n][:, :CHUNK] * kk[j] * gamma[n], 0.0)
                 for n, (j, d) in enumerate(chains)]
        t_mat = [eye + x for x in p_mat]
        for _ in range(int(np.log2(CHUNK)) - 1):
            pb = [x.astype(BF16) for x in p_mat]
            p_mat = [_dot(x, x) for x in pb]
            tb = [x.astype(BF16) for x in t_mat]
            pb = [x.astype(BF16) for x in p_mat]
            t_mat = [t + _dot(x, y) for t, x, y in zip(t_mat, tb, pb)]
        eb = [jnp.exp(x) for x in b]
        rhs = [jnp.concatenate([beta[n] * vs[rows[j], :], beta[n] * eb[n] * k[j]], axis=1).astype(BF16)
               for n, (j, d) in enumerate(chains)]
        wk = [_dot(t.astype(BF16), r).astype(BF16) for t, r in zip(t_mat, rhs)]
        b_last = [b[n][last_row[d]:last_row[d] + 1, :] for n, (j, d) in enumerate(chains)]
        kdt = [jnp.transpose(k[j] * jnp.exp(b_last[n] - b[n])).astype(BF16)
               for n, (j, d) in enumerate(chains)]
        bm = [_dot(x, y) for x, y in zip(kdt, wk)]
        pw = [_dot((qk[j] * gamma[n]).astype(BF16), wk[n]) for n, (j, d) in enumerate(chains)]
        for n, (j, d) in enumerate(chains):
            c = cs[j]
            bns[d, pl.ds(pl.multiple_of(c * GDN_DK, GDN_DK), GDN_DK), :] = bm[n][:, :GDN_DV]
            r3 = pl.multiple_of(c * (GDN_DK + CHUNK), CHUNK)
            mqs[d, pl.ds(r3, GDN_DK), :] = bm[n][:, GDN_DV:].astype(BF16)
            mqs[d, pl.ds(r3 + GDN_DK, CHUNK), :] = (q[j] * eb[n] - pw[n][:, GDN_DV:]).astype(BF16)
            ges[d, pl.ds(pl.multiple_of(c * 8, 8), 8), :] = jnp.broadcast_to(jnp.exp(b_last[n]), (8, LANES))
        for j in range(unroll):
            os_[rows[j], :] = pw[2 * j][:, :GDN_DV] + pw[2 * j + 1][:, :GDN_DV]
        return carry

    lax.fori_loop(0, nc // unroll, prep, 0)

    def step(i, states):
        cs = (i, nc - 1 - i)
        sb = [x.astype(BF16) for x in states]
        r = [_dot(mqs[d, pl.ds(pl.multiple_of(cs[d] * (GDN_DK + CHUNK), CHUNK), GDN_DK + CHUNK), :], sb[d])
             for d in range(2)]
        new = []
        for d in range(2):
            c = cs[d]
            os_[pl.ds(pl.multiple_of(c * CHUNK, CHUNK), CHUNK), :] += r[d][GDN_DK:]
            ge = ges[d, pl.ds(pl.multiple_of(c * 8, 8), 1), :]
            bn = bns[d, pl.ds(pl.multiple_of(c * GDN_DK, GDN_DK), GDN_DK), :]
            new.append(ge * states[d] - r[d][:GDN_DK] + bn)
        return tuple(new)

    s0 = jnp.zeros((GDN_DK, GDN_DV), F32)
    lax.fori_loop(0, nc, step, (s0, s0))

    o_ref[...] = (_rms_norm(os_[...], nw_ref[...]) * _silu(z_ref[...].astype(F32))).astype(o_ref.dtype)


def _gdn(qkvz, ab, conv_w, a_log, dt_bias, norm_w, bsz, s_len):
    nh = GDN_HEADS
    qkvz3 = qkvz.reshape(bsz, s_len, QKVZ_W)
    ab3 = ab.reshape(bsz, s_len, AB_W)
    pad = jnp.zeros((AB_W - 2 * nh,), F32)
    alog = jnp.concatenate([a_log.reshape(-1), pad])[None, :]
    dtb = jnp.concatenate([dt_bias.reshape(-1), pad])[None, :]

    def col(k):
        return pl.BlockSpec((None, s_len, LANES), lambda b, h: (b, 0, k * nh + h))

    def cw(k):
        return pl.BlockSpec((CONV_K, LANES), lambda b, h: (0, k * nh + h))

    vec = pl.BlockSpec((1, LANES), lambda b, h: (0, 0))
    nc = s_len // CHUNK
    scratch = [
        pltpu.VMEM((s_len, LANES), F32), pltpu.VMEM((s_len, LANES), F32), pltpu.VMEM((s_len, LANES), F32),
        pltpu.VMEM((2, s_len, LANES), F32), pltpu.VMEM((2, s_len, LANES), F32),
        pltpu.VMEM((2, s_len, CHUNK), F32), pltpu.VMEM((2, nc * GDN_DK, GDN_DV), F32),
        pltpu.VMEM((2, nc * (GDN_DK + CHUNK), GDN_DV), BF16),
        pltpu.VMEM((2, nc * 8, LANES), F32), pltpu.VMEM((s_len, GDN_DV), F32),
    ]
    out = pl.pallas_call(
        _gdn_kernel, out_shape=jax.ShapeDtypeStruct((bsz, s_len, nh * GDN_DV), BF16), grid=(bsz, nh),
        in_specs=[col(0), col(1), col(2), col(3),
                  pl.BlockSpec((None, s_len, AB_W), lambda b, h: (b, 0, 0)),
                  cw(0), cw(1), cw(2), vec, vec, vec],
        out_specs=pl.BlockSpec((None, s_len, GDN_DV), lambda b, h: (b, 0, h)),
        scratch_shapes=scratch, compiler_params=_params(("parallel", "parallel")), name="gdn",
    )(qkvz3, qkvz3, qkvz3, qkvz3, ab3, conv_w, conv_w, conv_w, alog, dtb, norm_w.reshape(1, GDN_DV))
    return out.reshape(bsz * s_len, nh * GDN_DV)


MLA_TQ = 256
MLA_KC = 512


def _mla_kernel(cq_ref, ckv_ref, kab_ref, mq_ref, mk1_ref, mk2_ref, qnw_ref, kvnw_ref, wq_ref, wkv_ref,
                o_ref, q_s, k_s, vt_s, sta_s, stb_s):
    s_len = cq_ref.shape[0]
    scale = (QK_NOPE + QK_ROPE) ** -0.5
    cqn = _rms_norm(cq_ref[...], qnw_ref[...]).astype(BF16)
    q = _dot(cqn, wq_ref[...]) * scale
    q_s[:, :QK_NOPE] = q[:, :QK_NOPE].astype(BF16)
    q_s[:, QK_NOPE:] = (q[:, QK_NOPE:] * mq_ref[...]).astype(BF16)
    ckvn = _rms_norm(ckv_ref[...], kvnw_ref[...]).astype(BF16)
    kv = _dot(ckvn, wkv_ref[...])
    k_s[:, :QK_NOPE] = kv[:, :QK_NOPE].astype(BF16)
    kab = kab_ref[...]
    k_s[:, QK_NOPE:] = (kab[:, :LANES] * mk1_ref[...] + kab[:, LANES:] * mk2_ref[...]).astype(BF16)
    vt_s[:V_HEAD, :] = jnp.transpose(kv[:, QK_NOPE:]).astype(BF16)
    vt_s[V_HEAD:, :] = jnp.ones((vt_s.shape[0] - V_HEAD, s_len), BF16)

    tq = min(MLA_TQ, s_len)

    nq = s_len // tq
    kc = min(MLA_KC, s_len)

    def scores(i, st):
        st[...] = _dot_nt(k_s[...], q_s[_ds(i, tq), :])

    def attend(i, st):
        m = jnp.max(st[...], 0, keepdims=True)
        ot = jnp.zeros((vt_s.shape[0], tq), F32)
        for c in range(0, s_len, kc):
            p = jnp.exp(st[c:c + kc, :] - m).astype(BF16)
            ot = ot + _dot(vt_s[:, c:c + kc], p)
        o = jnp.transpose(ot[:V_HEAD] / ot[V_HEAD:V_HEAD + 1])
        o_ref[_ds(i, tq), :] = o.astype(o_ref.dtype)

    def pair(j, prefetch):
        scores(2 * j + 1, stb_s)
        attend(2 * j, sta_s)
        if prefetch:
            scores(2 * j + 2, sta_s)
        attend(2 * j + 1, stb_s)

    assert nq == 1 or nq % 2 == 0
    scores(0, sta_s)
    if nq == 1:
        attend(0, sta_s)
    else:
        def body(j, carry):
            pair(j, True)
            return carry
        lax.fori_loop(0, nq // 2 - 1, body, 0)
        pair(nq // 2 - 1, False)


def _pack_w_uq(w_uq):
    w = w_uq.reshape(Q_LORA, MLA_HEADS, QK_NOPE + QK_ROPE)
    half = QK_ROPE // 2
    r1 = w[..., QK_NOPE:QK_NOPE + half]
    r2 = w[..., QK_NOPE + half:]
    w = jnp.concatenate([w[..., :QK_NOPE], r1, r1, -r2, r2], axis=-1)
    return w.reshape(Q_LORA, MLA_HEADS * 2 * LANES).astype(BF16)


def _mla(cq, ckv, kab, tables, qnw, kvnw, wq, wkv, bsz, s_len):
    nh = MLA_HEADS

    def per_b(width):
        return pl.BlockSpec((None, s_len, width), lambda b, h: (b, 0, 0))

    def vec(width):
        return pl.BlockSpec((1, width), lambda b, h: (0, 0))

    mq, mk1, mk2 = (t.reshape(bsz, s_len, LANES) for t in tables)
    out = pl.pallas_call(
        _mla_kernel, out_shape=jax.ShapeDtypeStruct((bsz, s_len, nh * V_HEAD), BF16), grid=(bsz, nh),
        in_specs=[per_b(Q_LORA), per_b(KV_LORA), per_b(KAB_W), per_b(LANES), per_b(LANES), per_b(LANES),
                  vec(Q_LORA), vec(KV_LORA),
                  pl.BlockSpec((Q_LORA, 2 * LANES), lambda b, h: (0, h)),
                  pl.BlockSpec((KV_LORA, QK_NOPE + V_HEAD), lambda b, h: (0, h))],
        out_specs=pl.BlockSpec((None, s_len, V_HEAD), lambda b, h: (b, 0, h)),
        scratch_shapes=[pltpu.VMEM((s_len, 2 * LANES), BF16), pltpu.VMEM((s_len, 2 * LANES), BF16),
                        pltpu.VMEM((V_HEAD + 16, s_len), BF16),
                        pltpu.VMEM((s_len, min(MLA_TQ, s_len)), F32),
                        pltpu.VMEM((s_len, min(MLA_TQ, s_len)), F32)],
        compiler_params=_params(("parallel", "parallel")), name="mla",
    )(cq.reshape(bsz, s_len, Q_LORA), ckv.reshape(bsz, s_len, KV_LORA), kab.reshape(bsz, s_len, KAB_W),
      mq, mk1, mk2, qnw.reshape(1, Q_LORA), kvnw.reshape(1, KV_LORA), wq, wkv)
    return out.reshape(bsz * s_len, nh * V_HEAD)


def _tail_kernel(og_ref, om_ref, gates_ref, h_ref, wog_ref, wom_ref, wout_ref, g_ref, b_ref, o_ref, ob_ref):
    d = h_ref.shape[1]
    y_gdn = _dot(og_ref[...], wog_ref[...])
    y_mla = _dot(om_ref[...], wom_ref[...])
    y = (_sigmoid(gates_ref[:, :d].astype(F32)) * y_gdn + _sigmoid(gates_ref[:, d:].astype(F32)) * y_mla)
    m = _dot(y.astype(BF16), wout_ref[...])
    out = _layer_norm(DN_ALPHA * h_ref[...] + m, g_ref[...], b_ref[...])
    o_ref[...] = out
    ob_ref[...] = out.astype(BF16)


def _mixer_tail(og, om, gates, h, wog, wom, wout, g, b, tm):
    t, d = h.shape
    row = lambda w: pl.BlockSpec((tm, w), lambda i: (i, 0))
    full = pl.BlockSpec((d, d), lambda i: (0, 0))
    vec = pl.BlockSpec((1, d), lambda i: (0, 0))
    return pl.pallas_call(
        _tail_kernel, out_shape=(jax.ShapeDtypeStruct((t, d), F32), jax.ShapeDtypeStruct((t, d), BF16)),
        grid=(t // tm,),
        in_specs=[row(d), row(d), row(2 * d), row(d), full, full, full, vec, vec],
        out_specs=(row(d), row(d)), compiler_params=_params(("parallel",)), name="mixer_tail",
    )(og, om, gates, h, wog, wom, wout, g.reshape(1, d), b.reshape(1, d))


def _router_kernel(h_ref, w_ref, b_ref, eid_ref, gate_ref, cnt_ref, run):
    @pl.when(pl.program_id(0) == 0)
    def _():
        run[...] = jnp.zeros_like(run)

    logits = jnp.dot(h_ref[...], w_ref[...], precision=lax.Precision.HIGHEST,
                     preferred_element_type=F32) + b_ref[...]
    tm = logits.shape[0]
    lane = lax.broadcasted_iota(jnp.int32, logits.shape, 1)
    lane_f = lane.astype(F32)
    neg = -jnp.inf

    def first_max(vals):
        m = jnp.max(vals, -1, keepdims=True)
        idx = jnp.min(jnp.where(vals == m, lane_f, float(LANES)), -1, keepdims=True)
        return m, idx

    is_grp = lane < N_GROUPS
    mg, grp = first_max(jnp.where(is_grp, logits, neg))
    p_grp = 1.0 / jnp.sum(jnp.where(is_grp, jnp.exp(logits - mg), 0.0), -1, keepdims=True)
    in_grp = ((lane - N_GROUPS) >> 3).astype(F32) == grp
    le = jnp.where(in_grp, logits, neg)
    l1, i1 = first_max(le)
    l2, i2 = first_max(jnp.where(lane_f == i1, neg, le))
    t = jnp.exp(l2 - l1)
    g1 = p_grp / (1.0 + t)
    g2 = p_grp * t / (1.0 + t)
    hit1 = lane_f == i1
    hit2 = lane_f == i2
    onehot = jnp.where(hit1 | hit2, 1.0, 0.0)
    ri = lax.broadcasted_iota(jnp.int32, (tm, tm), 0)
    rj = lax.broadcasted_iota(jnp.int32, (tm, tm), 1)
    before = jnp.where(ri > rj, 1.0, 0.0).astype(BF16)
    prefix = _dot(before, onehot.astype(BF16)) + run[...]
    rank1 = jnp.sum(jnp.where(hit1, prefix, 0.0), -1, keepdims=True)
    rank2 = jnp.sum(jnp.where(hit2, prefix, 0.0), -1, keepdims=True)
    run[...] += jnp.sum(onehot, 0, keepdims=True)
    cnt_ref[...] = run[...]
    ids = jnp.where(lane == 0, i1 - float(N_GROUPS), jnp.where(lane == 1, i2 - float(N_GROUPS),
                    jnp.where(lane == 2, rank1, jnp.where(lane == 3, rank2, 0.0))))
    eid_ref[...] = ids.astype(jnp.int32)
    gate_ref[...] = jnp.where(lane == 0, g1, jnp.where(lane == 1, g2, 0.0))


def _router(h, w_rg, b_rg, w_re, b_re, tm):
    t, d = h.shape
    padw = LANES - N_GROUPS - N_EXPERTS
    w = jnp.concatenate([w_rg, w_re, jnp.zeros((d, padw), F32)], axis=1)
    b = jnp.concatenate([b_rg, b_re, jnp.zeros((padw,), F32)])[None, :]
    row = pl.BlockSpec((tm, LANES), lambda i: (i, 0))
    one = pl.BlockSpec((1, LANES), lambda i: (0, 0))
    return pl.pallas_call(
        _router_kernel,
        out_shape=(jax.ShapeDtypeStruct((t, LANES), jnp.int32), jax.ShapeDtypeStruct((t, LANES), F32),
                   jax.ShapeDtypeStruct((1, LANES), F32)),
        grid=(t // tm,),
        in_specs=[pl.BlockSpec((tm, d), lambda i: (i, 0)), pl.BlockSpec((d, LANES), lambda i: (0, 0)), one],
        out_specs=(row, row, one), scratch_shapes=[pltpu.VMEM((1, LANES), F32)],
        compiler_params=_params(("arbitrary",)), name="router",
    )(h, w, b)


def _expert_kernel(be_ref, x_ref, wg_ref, wu_ref, wd_ref, o_ref):
    del be_ref
    x = x_ref[...]
    hid = _silu(_dot(x, wg_ref[...].astype(BF16))) * _dot(x, wu_ref[...].astype(BF16))
    o_ref[...] = _dot(hid.astype(BF16), wd_ref[...].astype(BF16)).astype(o_ref.dtype)


def _experts(xb, blk_expert, w_gate, w_up, w_down, layer):
    n_rows, d = xb.shape
    n_blocks = n_rows // MOE_BLOCK
    grid_spec = pltpu.PrefetchScalarGridSpec(
        num_scalar_prefetch=1, grid=(n_blocks,),
        in_specs=[pl.BlockSpec((MOE_BLOCK, d), lambda i, be: (i, 0)),
                  pl.BlockSpec((None, None, d, D_EXPERT), lambda i, be: (layer, be[i], 0, 0)),
                  pl.BlockSpec((None, None, d, D_EXPERT), lambda i, be: (layer, be[i], 0, 0)),
                  pl.BlockSpec((None, None, D_EXPERT, d), lambda i, be: (layer, be[i], 0, 0))],
        out_specs=pl.BlockSpec((MOE_BLOCK, d), lambda i, be: (i, 0)))
    return pl.pallas_call(
        _expert_kernel, out_shape=jax.ShapeDtypeStruct((n_rows, d), BF16), grid_spec=grid_spec,
        compiler_params=_params(("arbitrary",)), name="experts",
    )(blk_expert, xb, w_gate, w_up, w_down)


def _combine_kernel(h_ref, y1_ref, y2_ref, gate_ref, g_ref, b_ref, o_ref):
    f = gate_ref[:, 0:1] * y1_ref[...].astype(F32) + gate_ref[:, 1:2] * y2_ref[...].astype(F32)
    o_ref[...] = _layer_norm(DN_ALPHA * h_ref[...] + f, g_ref[...], b_ref[...])


def _combine(h, y1, y2, gate, g, b, tm):
    t, d = h.shape
    row = pl.BlockSpec((tm, d), lambda i: (i, 0))
    vec = pl.BlockSpec((1, d), lambda i: (0, 0))
    return pl.pallas_call(
        _combine_kernel, out_shape=jax.ShapeDtypeStruct((t, d), F32), grid=(t // tm,),
        in_specs=[row, row, row, pl.BlockSpec((tm, LANES), lambda i: (i, 0)), vec, vec], out_specs=row,
        compiler_params=_params(("parallel",)), name="combine_ln",
    )(h, y1, y2, gate, g.reshape(1, d), b.reshape(1, d))


def _dispatch(ids, cnt):
    t = ids.shape[0]
    n_assign = t * TOP_K
    counts = cnt[0, N_GROUPS:N_GROUPS + N_EXPERTS].astype(jnp.int32)
    padded = (counts + MOE_BLOCK - 1) // MOE_BLOCK * MOE_BLOCK
    pad_end = jnp.cumsum(padded)
    pad_start = pad_end - padded
    start = jnp.cumsum(counts) - counts
    eid = ids[:, :TOP_K]
    dest = jnp.take(pad_start, eid) + ids[:, TOP_K:2 * TOP_K]
    order = jnp.argsort(eid.reshape(n_assign))
    n_blocks = -(-(n_assign + N_EXPERTS * (MOE_BLOCK - 1)) // MOE_BLOCK)
    blk_expert = jnp.minimum(
        jnp.searchsorted(pad_end, jnp.arange(n_blocks) * MOE_BLOCK, side='right'), N_EXPERTS - 1).astype(jnp.int32)
    shift = jnp.repeat(jnp.take(pad_start - start, blk_expert), MOE_BLOCK)
    end = jnp.repeat(jnp.take(start + counts, blk_expert), MOE_BLOCK)
    src = jnp.arange(n_blocks * MOE_BLOCK) - shift
    row_tok = jnp.where(src < end, jnp.take(order, jnp.clip(src, 0, n_assign - 1)) // TOP_K, 0).astype(jnp.int32)
    return row_tok, dest, blk_expert


def _moe(h, hb, w_rg, b_rg, w_re, b_re, w_gate, w_up, w_down, layer, g, b, tm):
    ids, gate, cnt = _router(h, w_rg, b_rg, w_re, b_re, tm)
    row_tok, dest, blk_expert = _dispatch(ids, cnt)
    xb = jnp.take(hb, row_tok, axis=0)
    yb = _experts(xb, blk_expert, w_gate, w_up, w_down, layer)
    y1 = jnp.take(yb, dest[:, 0], axis=0)
    y2 = jnp.take(yb, dest[:, 1], axis=0)
    return _combine(h, y1, y2, gate, g, b, tm)


def kernel(x, positions, ln_in_g, ln_in_b, w_in, conv_w, a_log, dt_bias, gdn_norm_w, w_o_gdn, mla_q_norm_w, w_uq, mla_kv_norm_w, w_ukv, w_o_mla, w_out, ln1_g, ln1_b, w_router_group, b_router_group, w_router_expert, b_router_expert, w_gate, w_up, w_down, ln2_g, ln2_b):
    bsz, s_len, d = x.shape
    t = bsz * s_len
    tm = min(256, t)
    tables = _rope_tables(positions, tm)
    h = _ln_in(x.reshape(t, d), ln_in_g, ln_in_b, tm)
    for l in range(DEPTH):
        qkvz, cq, ckv, kab, ab, gates = _proj(h, _pack_w_in(w_in[l]), tm)
        og = _gdn(qkvz, ab, conv_w[l], a_log[l], dt_bias[l], gdn_norm_w[l], bsz, s_len)
        om = _mla(cq, ckv, kab, tables, mla_q_norm_w[l], mla_kv_norm_w[l],
                  _pack_w_uq(w_uq[l]), w_ukv[l].astype(BF16), bsz, s_len)
        h, hb = _mixer_tail(og, om, gates, h, w_o_gdn[l].astype(BF16), w_o_mla[l].astype(BF16),
                            w_out[l].astype(BF16), ln1_g[l], ln1_b[l], tm)
        h = _moe(h, hb, w_router_group[l], b_router_group[l], w_router_expert[l], b_router_expert[l],
                 w_gate, w_up, w_down, l, ln2_g[l], ln2_b[l], tm)
    return h.reshape(bsz, s_len, d)
```

```python
import functools

import numpy as np
import jax
import jax.numpy as jnp
from jax import lax
from jax.experimental import pallas as pl
from jax.experimental.pallas import tpu as pltpu

F32 = jnp.float32
BF16 = jnp.bfloat16

D_MODEL = 1024
DEPTH = 2
GDN_HEADS = 8
GDN_DK = 128
GDN_DV = 128
CONV_K = 5
CHUNK = 64
MLA_HEADS = 8
Q_LORA = 384
KV_LORA = 256
QK_NOPE = 128
QK_ROPE = 64
V_HEAD = 128
ROPE_BASE = 10000.0
N_GROUPS = 8
EXPERTS_PER_GROUP = 8
N_EXPERTS = N_GROUPS * EXPERTS_PER_GROUP
TOP_K = 2
D_EXPERT = 512
MOE_BLOCK = 256
DN_ALPHA = (2 * DEPTH) ** 0.25
LN_EPS = 1e-5
RMS_EPS = 1e-6

LANES = 128
VMEM_LIMIT = 56 * 1024 * 1024

QKVZ_W = 4 * GDN_HEADS * GDN_DK
KAB_W = 2 * LANES
AB_W = LANES
GATES_W = 2 * D_MODEL
PROJ_SEGS = (QKVZ_W, Q_LORA, KV_LORA, KAB_W, AB_W, GATES_W)
PROJ_W = sum(PROJ_SEGS)


def _params(sem):
    return pltpu.CompilerParams(dimension_semantics=sem, vmem_limit_bytes=VMEM_LIMIT)


def _layer_norm(x, g, b):
    mu = jnp.mean(x, -1, keepdims=True)
    xc = x - mu
    var = jnp.mean(xc * xc, -1, keepdims=True)
    return xc * lax.rsqrt(var + LN_EPS) * g + b


def _rms_norm(x, w):
    return x * lax.rsqrt(jnp.mean(x * x, -1, keepdims=True) + RMS_EPS) * w


def _silu(x):
    return x / (1.0 + jnp.exp(-x))


def _sigmoid(x):
    return 1.0 / (1.0 + jnp.exp(-x))


def _dot(a, b):
    return jnp.dot(a, b, preferred_element_type=F32)


def _dot_nt(a, b):
    return lax.dot_general(a, b, (((1,), (1,)), ((), ())), preferred_element_type=F32)


def _ds(i, n):
    return pl.ds(i * n, n) if isinstance(i, int) else pl.ds(pl.multiple_of(i * n, n), n)


def _dot_tn(a, b):
    return lax.dot_general(a, b, (((0,), (0,)), ((), ())), preferred_element_type=F32)


def _rope_kernel(pos_ref, freq_ref, mq_ref, mk1_ref, mk2_ref):
    ang = pos_ref[...].astype(F32) * freq_ref[...]
    c = jnp.cos(ang)
    s = jnp.sin(ang)
    quarter = lax.broadcasted_iota(jnp.int32, ang.shape, 1) >> 5
    even = (quarter & 1) == 0
    mq_ref[...] = jnp.where((quarter == 0) | (quarter == 3), c, s)
    mk1_ref[...] = jnp.where(even, c, s)
    mk2_ref[...] = jnp.where(even, s, c)


def _rope_tables(positions, tm):
    t = positions.size
    half = QK_ROPE // 2
    inv_freq = jnp.power(ROPE_BASE, -jnp.arange(half, dtype=F32) / half)
    freq = jnp.tile(inv_freq, LANES // half)[None, :]
    out = jax.ShapeDtypeStruct((t, LANES), F32)
    row = pl.BlockSpec((tm, LANES), lambda i: (i, 0))
    return pl.pallas_call(
        _rope_kernel, out_shape=(out, out, out), grid=(t // tm,),
        in_specs=[pl.BlockSpec((tm, 1), lambda i: (i, 0)), pl.BlockSpec((1, LANES), lambda i: (0, 0))],
        out_specs=(row, row, row), compiler_params=_params(("parallel",)), name="rope_tables",
    )(positions.reshape(t, 1), freq)


def _ln_kernel(x_ref, g_ref, b_ref, o_ref):
    o_ref[...] = _layer_norm(x_ref[...], g_ref[...], b_ref[...])


def _ln_in(x2, g, b, tm):
    t, d = x2.shape
    row = pl.BlockSpec((tm, d), lambda i: (i, 0))
    vec = pl.BlockSpec((1, d), lambda i: (0, 0))
    return pl.pallas_call(
        _ln_kernel, out_shape=jax.ShapeDtypeStruct((t, d), F32), grid=(t // tm,),
        in_specs=[row, vec, vec], out_specs=row, compiler_params=_params(("parallel",)), name="ln_in",
    )(x2, g.reshape(1, d), b.reshape(1, d))


PROJ_CHUNK = 512


def _proj_kernel(h_ref, w_ref, *out_refs):
    a = h_ref[...].astype(BF16)
    off = 0
    for ref, width in zip(out_refs, PROJ_SEGS):
        for c in range(0, width, PROJ_CHUNK):
            wc = min(PROJ_CHUNK, width - c)
            ref[:, c:c + wc] = _dot(a, w_ref[:, off + c:off + c + wc]).astype(ref.dtype)
        off += width


def _proj(h, w_all, tm):
    t, d = h.shape
    dts = (BF16, F32, F32, F32, F32, BF16)
    outs = tuple(jax.ShapeDtypeStruct((t, w), dt) for w, dt in zip(PROJ_SEGS, dts))
    return pl.pallas_call(
        _proj_kernel, out_shape=outs, grid=(t // tm,),
        in_specs=[pl.BlockSpec((tm, d), lambda i: (i, 0)), pl.BlockSpec((d, PROJ_W), lambda i: (0, 0))],
        out_specs=tuple(pl.BlockSpec((tm, w), lambda i: (i, 0)) for w in PROJ_SEGS),
        compiler_params=_params(("parallel",)), name="in_proj",
    )(h, w_all)


def _pack_w_in(w_in):
    hq = GDN_HEADS * GDN_DK
    o_a = 4 * hq
    o_bt = o_a + 2 * GDN_HEADS
    o_cq = o_bt + 2 * GDN_HEADS
    o_ckv = o_cq + Q_LORA
    o_kr = o_ckv + KV_LORA
    o_g = o_kr + QK_ROPE
    half = QK_ROPE // 2
    k1 = w_in[:, o_kr:o_kr + half]
    k2 = w_in[:, o_kr + half:o_g]
    d = w_in.shape[0]
    ab = jnp.concatenate([w_in[:, o_a:o_cq], jnp.zeros((d, AB_W - 4 * GDN_HEADS), w_in.dtype)], 1)
    return jnp.concatenate(
        [w_in[:, :o_a], w_in[:, o_cq:o_ckv], w_in[:, o_ckv:o_kr],
         k1, k1, k1, k1, -k2, k2, -k2, k2, ab, w_in[:, o_g:]], axis=1).astype(BF16)


PREP_CHUNKS = 8

def _gdn_kernel(q_ref, k_ref, v_ref, z_ref, ab_ref, cwq_ref, cwk_ref, cwv_ref, alog_ref, dtb_ref, nw_ref,
                o_ref, qs, ks, vs, gs, bs, brs, bns, mqs, ges, os_):
    s_len = q_ref.shape[0]
    nc = s_len // CHUNK
    head = pl.program_id(1)
    edge = lax.broadcasted_iota(jnp.int32, (8, LANES), 0)

    def conv_silu(x_ref, cw_ref):
        x = x_ref[...].astype(F32)
        acc = x * cw_ref[CONV_K // 2:CONV_K // 2 + 1, :]
        for j in range(CONV_K):
            sh = CONV_K // 2 - j
            if sh == 0:
                continue
            xs = pltpu.roll(x, sh % s_len, axis=0)
            if sh > 0:
                xs = jnp.concatenate([jnp.where(edge >= sh, xs[:8], 0.0), xs[8:]], axis=0)
            else:
                xs = jnp.concatenate([xs[:-8], jnp.where(edge < 8 + sh, xs[-8:], 0.0)], axis=0)
            acc = acc + xs * cw_ref[j:j + 1, :]
        return _silu(acc)

    def l2n(x):
        return x * lax.rsqrt(jnp.sum(x * x, -1, keepdims=True) + RMS_EPS)

    qs[...] = l2n(conv_silu(q_ref, cwq_ref)) * (GDN_DK ** -0.5)
    ks[...] = l2n(conv_silu(k_ref, cwk_ref))
    vs[...] = conv_silu(v_ref, cwv_ref)

    ab = ab_ref[...]
    x = ab + dtb_ref[...]
    softplus = jnp.maximum(x, 0.0) + jnp.log(1.0 + jnp.exp(-jnp.abs(x)))
    g_all = -jnp.exp(alog_ref[...]) * softplus
    b_all = _sigmoid(ab)
    lane = lax.broadcasted_iota(jnp.int32, (1, LANES), 1)

    def pick(vals, idx):
        col = jnp.sum(jnp.where(lane == idx, vals, 0.0), -1, keepdims=True)
        return jnp.broadcast_to(col, (s_len, LANES))

    for d in range(2):
        gs[d] = pick(g_all, head + d * GDN_HEADS)
        bs[d] = pick(b_all, head + (2 + d) * GDN_HEADS)

    ci = lax.broadcasted_iota(jnp.int32, (CHUNK, CHUNK), 0)
    cj = lax.broadcasted_iota(jnp.int32, (CHUNK, CHUNK), 1)
    eye = jnp.where(ci == cj, 1.0, 0.0)
    incl = (ci >= cj, ci <= cj)
    strict = (ci > cj, ci < cj)
    tri = tuple(jnp.where(m, 1.0, 0.0).astype(BF16) for m in incl)
    last_row = (CHUNK - 1, 0)

    unroll = 4 if nc % 4 == 0 else 1

    def cumulate(c, carry):
        rows = [pl.ds(pl.multiple_of((c * unroll + j) * CHUNK, CHUNK), CHUNK) for j in range(unroll)]
        chains = [(j, d) for j in range(unroll) for d in range(2)]
        parts = []
        for j, d in chains:
            g = gs[d, rows[j], :]
            g1 = g.astype(BF16)
            r1 = g - g1.astype(F32)
            g2 = r1.astype(BF16)
            parts.append((g1, g2, (r1 - g2.astype(F32)).astype(BF16)))
        b = [_dot(tri[d], p[0]) + _dot(tri[d], p[1]) + _dot(tri[d], p[2])
             for (j, d), p in zip(chains, parts)]
        for (j, d), x in zip(chains, b):
            gs[d, rows[j], :] = x
            brs[d, rows[j], :] = jnp.transpose(x)[:CHUNK, :]
        return carry

    lax.fori_loop(0, nc // unroll, cumulate, 0)

    def prep(it, carry):
        unroll = PREP_CHUNKS if nc % PREP_CHUNKS == 0 else 1
        cs = [it * unroll + j for j in range(unroll)]
        rows = [pl.ds(pl.multiple_of(c * CHUNK, CHUNK), CHUNK) for c in cs]
        q = [qs[r, :] for r in rows]
        k = [ks[r, :] for r in rows]
        kb = [x.astype(BF16) for x in k]
        kk = [_dot_nt(x, x) for x in kb]
        qk = [_dot_nt(x.astype(BF16), y) for x, y in zip(q, kb)]
        chains = [(j, d) for j in range(unroll) for d in range(2)]
        b = [gs[d, rows[j], :] for j, d in chains]
        beta = [bs[d, rows[j], :] for j, d in chains]
        gamma = [jnp.exp(jnp.where(incl[d], b[n][:, :CHUNK] - brs[d, rows[j], :], -jnp.inf))
                 for n, (j, d) in enumerate(chains)]
        p_mat = [jnp.where(strict[d], -beta[n][:, :CHUNK] * kk[j] * gamma[n], 0.0)
                 for n, (j, d) in enumerate(chains)]
        t_mat = [eye + x for x in p_mat]
        for _ in range(int(np.log2(CHUNK)) - 1):
            pb = [x.astype(BF16) for x in p_mat]
            p_mat = [_dot(x, x) for x in pb]
            tb = [x.astype(BF16) for x in t_mat]
            pb = [x.astype(BF16) for x in p_mat]
            t_mat = [t + _dot(x, y) for t, x, y in zip(t_mat, tb, pb)]
        eb = [jnp.exp(x) for x in b]
        rhs = [jnp.concatenate([beta[n] * vs[rows[j], :], beta[n] * eb[n] * k[j]], axis=1).astype(BF16)
               for n, (j, d) in enumerate(chains)]
        wk = [_dot(t.astype(BF16), r).astype(BF16) for t, r in zip(t_mat, rhs)]
        b_last = [b[n][last_row[d]:last_row[d] + 1, :] for n, (j, d) in enumerate(chains)]
        kdt = [jnp.transpose(k[j] * jnp.exp(b_last[n] - b[n])).astype(BF16)
               for n, (j, d) in enumerate(chains)]
        bm = [_dot(x, y) for x, y in zip(kdt, wk)]
        pw = [_dot((qk[j] * gamma[n]).astype(BF16), wk[n]) for n, (j, d) in enumerate(chains)]
        for n, (j, d) in enumerate(chains):
            c = cs[j]
            bns[d, pl.ds(pl.multiple_of(c * GDN_DK, GDN_DK), GDN_DK), :] = bm[n][:, :GDN_DV]
            r3 = pl.multiple_of(c * (GDN_DK + CHUNK), CHUNK)
            mqs[d, pl.ds(r3, GDN_DK), :] = bm[n][:, GDN_DV:].astype(BF16)
            mqs[d, pl.ds(r3 + GDN_DK, CHUNK), :] = (q[j] * eb[n] - pw[n][:, GDN_DV:]).astype(BF16)
            ges[d, pl.ds(pl.multiple_of(c * 8, 8), 8), :] = jnp.broadcast_to(jnp.exp(b_last[n]), (8, LANES))
        for j in range(unroll):
            os_[rows[j], :] = pw[2 * j][:, :GDN_DV] + pw[2 * j + 1][:, :GDN_DV]
        return carry

    lax.fori_loop(0, nc // (PREP_CHUNKS if nc % PREP_CHUNKS == 0 else 1), prep, 0)

    def step(i, states):
        cs = (i, nc - 1 - i)
        sb = [x.astype(BF16) for x in states]
        r = [_dot(mqs[d, pl.ds(pl.multiple_of(cs[d] * (GDN_DK + CHUNK), CHUNK), GDN_DK + CHUNK), :], sb[d])
             for d in range(2)]
        new = []
        for d in range(2):
            c = cs[d]
            os_[pl.ds(pl.multiple_of(c * CHUNK, CHUNK), CHUNK), :] += r[d][GDN_DK:]
            ge = ges[d, pl.ds(pl.multiple_of(c * 8, 8), 1), :]
            bn = bns[d, pl.ds(pl.multiple_of(c * GDN_DK, GDN_DK), GDN_DK), :]
            new.append(ge * states[d] - r[d][:GDN_DK] + bn)
        return tuple(new)

    s0 = jnp.zeros((GDN_DK, GDN_DV), F32)
    lax.fori_loop(0, nc, step, (s0, s0))

    o_ref[...] = (_rms_norm(os_[...], nw_ref[...]) * _silu(z_ref[...].astype(F32))).astype(o_ref.dtype)


def _gdn(qkvz, ab, conv_w, a_log, dt_bias, norm_w, bsz, s_len):
    nh = GDN_HEADS
    qkvz3 = qkvz.reshape(bsz, s_len, QKVZ_W)
    ab3 = ab.reshape(bsz, s_len, AB_W)
    pad = jnp.zeros((AB_W - 2 * nh,), F32)
    alog = jnp.concatenate([a_log.reshape(-1), pad])[None, :]
    dtb = jnp.concatenate([dt_bias.reshape(-1), pad])[None, :]

    def col(k):
        return pl.BlockSpec((None, s_len, LANES), lambda b, h: (b, 0, k * nh + h))

    def cw(k):
        return pl.BlockSpec((CONV_K, LANES), lambda b, h: (0, k * nh + h))

    vec = pl.BlockSpec((1, LANES), lambda b, h: (0, 0))
    nc = s_len // CHUNK
    scratch = [
        pltpu.VMEM((s_len, LANES), F32), pltpu.VMEM((s_len, LANES), F32), pltpu.VMEM((s_len, LANES), F32),
        pltpu.VMEM((2, s_len, LANES), F32), pltpu.VMEM((2, s_len, LANES), F32),
        pltpu.VMEM((2, s_len, CHUNK), F32), pltpu.VMEM((2, nc * GDN_DK, GDN_DV), F32),
        pltpu.VMEM((2, nc * (GDN_DK + CHUNK), GDN_DV), BF16),
        pltpu.VMEM((2, nc * 8, LANES), F32), pltpu.VMEM((s_len, GDN_DV), F32),
    ]
    out = pl.pallas_call(
        _gdn_kernel, out_shape=jax.ShapeDtypeStruct((bsz, s_len, nh * GDN_DV), BF16), grid=(bsz, nh),
        in_specs=[col(0), col(1), col(2), col(3),
                  pl.BlockSpec((None, s_len, AB_W), lambda b, h: (b, 0, 0)),
                  cw(0), cw(1), cw(2), vec, vec, vec],
        out_specs=pl.BlockSpec((None, s_len, GDN_DV), lambda b, h: (b, 0, h)),
        scratch_shapes=scratch, compiler_params=_params(("parallel", "parallel")), name="gdn",
    )(qkvz3, qkvz3, qkvz3, qkvz3, ab3, conv_w, conv_w, conv_w, alog, dtb, norm_w.reshape(1, GDN_DV))
    return out.reshape(bsz * s_len, nh * GDN_DV)


MLA_TQ = 256
MLA_KC = 512


def _mla_kernel(cq_ref, ckv_ref, kab_ref, mq_ref, mk1_ref, mk2_ref, qnw_ref, kvnw_ref, wq_ref, wkv_ref,
                o_ref, q_s, k_s, vt_s, sta_s, stb_s):
    s_len = cq_ref.shape[0]
    scale = (QK_NOPE + QK_ROPE) ** -0.5
    cqn = _rms_norm(cq_ref[...], qnw_ref[...]).astype(BF16)
    q = _dot(cqn, wq_ref[...]) * scale
    q_s[:, :QK_NOPE] = q[:, :QK_NOPE].astype(BF16)
    q_s[:, QK_NOPE:] = (q[:, QK_NOPE:] * mq_ref[...]).astype(BF16)
    ckvn = _rms_norm(ckv_ref[...], kvnw_ref[...]).astype(BF16)
    kv = _dot(ckvn, wkv_ref[...])
    k_s[:, :QK_NOPE] = kv[:, :QK_NOPE].astype(BF16)
    kab = kab_ref[...]
    k_s[:, QK_NOPE:] = (kab[:, :LANES] * mk1_ref[...] + kab[:, LANES:] * mk2_ref[...]).astype(BF16)
    vt_s[:V_HEAD, :] = jnp.transpose(kv[:, QK_NOPE:]).astype(BF16)
    vt_s[V_HEAD:, :] = jnp.ones((vt_s.shape[0] - V_HEAD, s_len), BF16)

    tq = min(MLA_TQ, s_len)

    nq = s_len // tq
    kc = min(MLA_KC, s_len)

    def scores(i, st):
        st[...] = _dot_nt(k_s[...], q_s[_ds(i, tq), :])

    def attend(i, st):
        m = jnp.max(st[...], 0, keepdims=True)
        ot = jnp.zeros((vt_s.shape[0], tq), F32)
        for c in range(0, s_len, kc):
            p = jnp.exp(st[c:c + kc, :] - m).astype(BF16)
            ot = ot + _dot(vt_s[:, c:c + kc], p)
        o = jnp.transpose(ot[:V_HEAD] / ot[V_HEAD:V_HEAD + 1])
        o_ref[_ds(i, tq), :] = o.astype(o_ref.dtype)

    def pair(j, prefetch):
        scores(2 * j + 1, stb_s)
        attend(2 * j, sta_s)
        if prefetch:
            scores(2 * j + 2, sta_s)
        attend(2 * j + 1, stb_s)

    assert nq == 1 or nq % 2 == 0
    scores(0, sta_s)
    if nq == 1:
        attend(0, sta_s)
    else:
        def body(j, carry):
            pair(j, True)
            return carry
        lax.fori_loop(0, nq // 2 - 1, body, 0)
        pair(nq // 2 - 1, False)


def _pack_w_uq(w_uq):
    w = w_uq.reshape(Q_LORA, MLA_HEADS, QK_NOPE + QK_ROPE)
    half = QK_ROPE // 2
    r1 = w[..., QK_NOPE:QK_NOPE + half]
    r2 = w[..., QK_NOPE + half:]
    w = jnp.concatenate([w[..., :QK_NOPE], r1, r1, -r2, r2], axis=-1)
    return w.reshape(Q_LORA, MLA_HEADS * 2 * LANES).astype(BF16)


def _mla(cq, ckv, kab, tables, qnw, kvnw, wq, wkv, bsz, s_len):
    nh = MLA_HEADS

    def per_b(width):
        return pl.BlockSpec((None, s_len, width), lambda b, h: (b, 0, 0))

    def vec(width):
        return pl.BlockSpec((1, width), lambda b, h: (0, 0))

    mq, mk1, mk2 = (t.reshape(bsz, s_len, LANES) for t in tables)
    out = pl.pallas_call(
        _mla_kernel, out_shape=jax.ShapeDtypeStruct((bsz, s_len, nh * V_HEAD), BF16), grid=(bsz, nh),
        in_specs=[per_b(Q_LORA), per_b(KV_LORA), per_b(KAB_W), per_b(LANES), per_b(LANES), per_b(LANES),
                  vec(Q_LORA), vec(KV_LORA),
                  pl.BlockSpec((Q_LORA, 2 * LANES), lambda b, h: (0, h)),
                  pl.BlockSpec((KV_LORA, QK_NOPE + V_HEAD), lambda b, h: (0, h))],
        out_specs=pl.BlockSpec((None, s_len, V_HEAD), lambda b, h: (b, 0, h)),
        scratch_shapes=[pltpu.VMEM((s_len, 2 * LANES), BF16), pltpu.VMEM((s_len, 2 * LANES), BF16),
                        pltpu.VMEM((V_HEAD + 16, s_len), BF16),
                        pltpu.VMEM((s_len, min(MLA_TQ, s_len)), F32),
                        pltpu.VMEM((s_len, min(MLA_TQ, s_len)), F32)],
        compiler_params=_params(("parallel", "parallel")), name="mla",
    )(cq.reshape(bsz, s_len, Q_LORA), ckv.reshape(bsz, s_len, KV_LORA), kab.reshape(bsz, s_len, KAB_W),
      mq, mk1, mk2, qnw.reshape(1, Q_LORA), kvnw.reshape(1, KV_LORA), wq, wkv)
    return out.reshape(bsz * s_len, nh * V_HEAD)


def _tail_kernel(og_ref, om_ref, gates_ref, h_ref, wog_ref, wom_ref, wout_ref, g_ref, b_ref, o_ref, ob_ref):
    d = h_ref.shape[1]
    y_gdn = _dot(og_ref[...], wog_ref[...])
    y_mla = _dot(om_ref[...], wom_ref[...])
    y = (_sigmoid(gates_ref[:, :d].astype(F32)) * y_gdn + _sigmoid(gates_ref[:, d:].astype(F32)) * y_mla)
    m = _dot(y.astype(BF16), wout_ref[...])
    out = _layer_norm(DN_ALPHA * h_ref[...] + m, g_ref[...], b_ref[...])
    o_ref[...] = out
    ob_ref[...] = out.astype(BF16)


def _mixer_tail(og, om, gates, h, wog, wom, wout, g, b, tm):
    t, d = h.shape
    row = lambda w: pl.BlockSpec((tm, w), lambda i: (i, 0))
    full = pl.BlockSpec((d, d), lambda i: (0, 0))
    vec = pl.BlockSpec((1, d), lambda i: (0, 0))
    return pl.pallas_call(
        _tail_kernel, out_shape=(jax.ShapeDtypeStruct((t, d), F32), jax.ShapeDtypeStruct((t, d), BF16)),
        grid=(t // tm,),
        in_specs=[row(d), row(d), row(2 * d), row(d), full, full, full, vec, vec],
        out_specs=(row(d), row(d)), compiler_params=_params(("parallel",)), name="mixer_tail",
    )(og, om, gates, h, wog, wom, wout, g.reshape(1, d), b.reshape(1, d))


def _router_kernel(h_ref, w_ref, b_ref, eid_ref, gate_ref, cnt_ref, run):
    @pl.when(pl.program_id(0) == 0)
    def _():
        run[...] = jnp.zeros_like(run)

    logits = jnp.dot(h_ref[...], w_ref[...], precision=lax.Precision.HIGHEST,
                     preferred_element_type=F32) + b_ref[...]
    tm = logits.shape[0]
    lane = lax.broadcasted_iota(jnp.int32, logits.shape, 1)
    lane_f = lane.astype(F32)
    neg = -jnp.inf

    def first_max(vals):
        m = jnp.max(vals, -1, keepdims=True)
        idx = jnp.min(jnp.where(vals == m, lane_f, float(LANES)), -1, keepdims=True)
        return m, idx

    is_grp = lane < N_GROUPS
    mg, grp = first_max(jnp.where(is_grp, logits, neg))
    p_grp = 1.0 / jnp.sum(jnp.where(is_grp, jnp.exp(logits - mg), 0.0), -1, keepdims=True)
    in_grp = ((lane - N_GROUPS) >> 3).astype(F32) == grp
    le = jnp.where(in_grp, logits, neg)
    l1, i1 = first_max(le)
    l2, i2 = first_max(jnp.where(lane_f == i1, neg, le))
    t = jnp.exp(l2 - l1)
    g1 = p_grp / (1.0 + t)
    g2 = p_grp * t / (1.0 + t)
    hit1 = lane_f == i1
    hit2 = lane_f == i2
    onehot = jnp.where(hit1 | hit2, 1.0, 0.0)
    ri = lax.broadcasted_iota(jnp.int32, (tm, tm), 0)
    rj = lax.broadcasted_iota(jnp.int32, (tm, tm), 1)
    before = jnp.where(ri > rj, 1.0, 0.0).astype(BF16)
    prefix = _dot(before, onehot.astype(BF16)) + run[...]
    rank1 = jnp.sum(jnp.where(hit1, prefix, 0.0), -1, keepdims=True)
    rank2 = jnp.sum(jnp.where(hit2, prefix, 0.0), -1, keepdims=True)
    run[...] += jnp.sum(onehot, 0, keepdims=True)
    cnt_ref[...] = run[...]
    ids = jnp.where(lane == 0, i1 - float(N_GROUPS), jnp.where(lane == 1, i2 - float(N_GROUPS),
                    jnp.where(lane == 2, rank1, jnp.where(lane == 3, rank2, 0.0))))
    eid_ref[...] = ids.astype(jnp.int32)
    gate_ref[...] = jnp.where(lane == 0, g1, jnp.where(lane == 1, g2, 0.0))


def _router(h, w_rg, b_rg, w_re, b_re, tm):
    t, d = h.shape
    padw = LANES - N_GROUPS - N_EXPERTS
    w = jnp.concatenate([w_rg, w_re, jnp.zeros((d, padw), F32)], axis=1)
    b = jnp.concatenate([b_rg, b_re, jnp.zeros((padw,), F32)])[None, :]
    row = pl.BlockSpec((tm, LANES), lambda i: (i, 0))
    one = pl.BlockSpec((1, LANES), lambda i: (0, 0))
    return pl.pallas_call(
        _router_kernel,
        out_shape=(jax.ShapeDtypeStruct((t, LANES), jnp.int32), jax.ShapeDtypeStruct((t, LANES), F32),
                   jax.ShapeDtypeStruct((1, LANES), F32)),
        grid=(t // tm,),
        in_specs=[pl.BlockSpec((tm, d), lambda i: (i, 0)), pl.BlockSpec((d, LANES), lambda i: (0, 0)), one],
        out_specs=(row, row, one), scratch_shapes=[pltpu.VMEM((1, LANES), F32)],
        compiler_params=_params(("arbitrary",)), name="router",
    )(h, w, b)


def _expert_kernel(be_ref, nu_ref, x_ref, wg_ref, wu_ref, wd_ref, o_ref, wgb, wub, wdb):
    i = pl.program_id(0)

    @pl.when((i == 0) | (be_ref[i] != be_ref[jnp.maximum(i - 1, 0)]))
    def _():
        wgb[...] = wg_ref[...].astype(BF16)
        wub[...] = wu_ref[...].astype(BF16)
        wdb[...] = wd_ref[...].astype(BF16)

    @pl.when(i < nu_ref[0])
    def _():
        x = x_ref[...]
        hid = _silu(_dot(x, wgb[...])) * _dot(x, wub[...])
        o_ref[...] = _dot(hid.astype(BF16), wdb[...]).astype(o_ref.dtype)

    @pl.when(i >= nu_ref[0])
    def _():
        o_ref[...] = jnp.zeros_like(o_ref)


def _experts(xb, blk_expert, n_used, w_gate, w_up, w_down, layer):
    n_rows, d = xb.shape
    n_blocks = n_rows // MOE_BLOCK
    wspec = lambda shape: pl.BlockSpec((None, None) + shape, lambda i, be, nu: (layer, be[i], 0, 0))
    grid_spec = pltpu.PrefetchScalarGridSpec(
        num_scalar_prefetch=2, grid=(n_blocks,),
        in_specs=[pl.BlockSpec((MOE_BLOCK, d), lambda i, be, nu: (i, 0)),
                  wspec((d, D_EXPERT)), wspec((d, D_EXPERT)), wspec((D_EXPERT, d))],
        out_specs=pl.BlockSpec((MOE_BLOCK, d), lambda i, be, nu: (i, 0)),
        scratch_shapes=[pltpu.VMEM((d, D_EXPERT), BF16), pltpu.VMEM((d, D_EXPERT), BF16),
                        pltpu.VMEM((D_EXPERT, d), BF16)])
    return pl.pallas_call(
        _expert_kernel, out_shape=jax.ShapeDtypeStruct((n_rows, d), BF16), grid_spec=grid_spec,
        compiler_params=_params(("arbitrary",)), name="experts",
    )(blk_expert, n_used, xb, w_gate, w_up, w_down)


def _combine_kernel(h_ref, y1_ref, y2_ref, gate_ref, g_ref, b_ref, o_ref):
    f = gate_ref[:, 0:1] * y1_ref[...].astype(F32) + gate_ref[:, 1:2] * y2_ref[...].astype(F32)
    o_ref[...] = _layer_norm(DN_ALPHA * h_ref[...] + f, g_ref[...], b_ref[...])


def _combine(h, y1, y2, gate, g, b, tm):
    t, d = h.shape
    row = pl.BlockSpec((tm, d), lambda i: (i, 0))
    vec = pl.BlockSpec((1, d), lambda i: (0, 0))
    return pl.pallas_call(
        _combine_kernel, out_shape=jax.ShapeDtypeStruct((t, d), F32), grid=(t // tm,),
        in_specs=[row, row, row, pl.BlockSpec((tm, LANES), lambda i: (i, 0)), vec, vec], out_specs=row,
        compiler_params=_params(("parallel",)), name="combine_ln",
    )(h, y1, y2, gate, g.reshape(1, d), b.reshape(1, d))


def _dispatch(ids, cnt):
    t = ids.shape[0]
    n_assign = t * TOP_K
    counts = cnt[0, N_GROUPS:N_GROUPS + N_EXPERTS].astype(jnp.int32)
    padded = (counts + MOE_BLOCK - 1) // MOE_BLOCK * MOE_BLOCK
    pad_end = jnp.cumsum(padded)
    pad_start = pad_end - padded
    start = jnp.cumsum(counts) - counts
    eid = ids[:, :TOP_K]
    dest = jnp.take(pad_start, eid) + ids[:, TOP_K:2 * TOP_K]
    order = jnp.argsort(eid.reshape(n_assign))
    n_blocks = -(-(n_assign + N_EXPERTS * (MOE_BLOCK - 1)) // MOE_BLOCK)
    blk_first = (jnp.arange(n_blocks) * MOE_BLOCK)[:, None]
    blk_expert = jnp.minimum(jnp.sum(pad_end[None, :] <= blk_first, axis=1), N_EXPERTS - 1).astype(jnp.int32)
    shift = jnp.repeat(jnp.take(pad_start - start, blk_expert), MOE_BLOCK)
    end = jnp.repeat(jnp.take(start + counts, blk_expert), MOE_BLOCK)
    src = jnp.arange(n_blocks * MOE_BLOCK) - shift
    row_tok = jnp.where(src < end, jnp.take(order, jnp.clip(src, 0, n_assign - 1)) // TOP_K, 0).astype(jnp.int32)
    n_used = (pad_end[-1:] // MOE_BLOCK).astype(jnp.int32)
    return row_tok, dest, blk_expert, n_used


def _moe(h, hb, w_rg, b_rg, w_re, b_re, w_gate, w_up, w_down, layer, g, b, tm):
    ids, gate, cnt = _router(h, w_rg, b_rg, w_re, b_re, tm)
    row_tok, dest, blk_expert, n_used = _dispatch(ids, cnt)
    xb = jnp.take(hb, row_tok, axis=0)
    yb = _experts(xb, blk_expert, n_used, w_gate, w_up, w_down, layer)
    y1 = jnp.take(yb, dest[:, 0], axis=0)
    y2 = jnp.take(yb, dest[:, 1], axis=0)
    return _combine(h, y1, y2, gate, g, b, tm)


def kernel(x, positions, ln_in_g, ln_in_b, w_in, conv_w, a_log, dt_bias, gdn_norm_w, w_o_gdn, mla_q_norm_w, w_uq, mla_kv_norm_w, w_ukv, w_o_mla, w_out, ln1_g, ln1_b, w_router_group, b_router_group, w_router_expert, b_router_expert, w_gate, w_up, w_down, ln2_g, ln2_b):
    bsz, s_len, d = x.shape
    t = bsz * s_len
    tm = min(256, t)
    tables = _rope_tables(positions, tm)
    h = _ln_in(x.reshape(t, d), ln_in_g, ln_in_b, tm)
    for l in range(DEPTH):
        qkvz, cq, ckv, kab, ab, gates = _proj(h, _pack_w_in(w_in[l]), tm)
        og = _gdn(qkvz, ab, conv_w[l], a_log[l], dt_bias[l], gdn_norm_w[l], bsz, s_len)
        om = _mla(cq, ckv, kab, tables, mla_q_norm_w[l], mla_kv_norm_w[l],
                  _pack_w_uq(w_uq[l]), w_ukv[l].astype(BF16), bsz, s_len)
        h, hb = _mixer_tail(og, om, gates, h, w_o_gdn[l].astype(BF16), w_o_mla[l].astype(BF16),
                            w_out[l].astype(BF16), ln1_g[l], ln1_b[l], tm)
        h = _moe(h, hb, w_router_group[l], b_router_group[l], w_router_expert[l], b_router_expert[l],
                 w_gate, w_up, w_down, l, ln2_g[l], ln2_b[l], tm)
    return h.reshape(bsz, s_len, d)
```

```python
import functools

import numpy as np
import jax
import jax.numpy as jnp
from jax import lax
from jax.experimental import pallas as pl
from jax.experimental.pallas import tpu as pltpu

F32 = jnp.float32
BF16 = jnp.bfloat16

D_MODEL = 1024
DEPTH = 2
GDN_HEADS = 8
GDN_DK = 128
GDN_DV = 128
CONV_K = 5
CHUNK = 64
MLA_HEADS = 8
Q_LORA = 384
KV_LORA = 256
QK_NOPE = 128
QK_ROPE = 64
V_HEAD = 128
ROPE_BASE = 10000.0
N_GROUPS = 8
EXPERTS_PER_GROUP = 8
N_EXPERTS = N_GROUPS * EXPERTS_PER_GROUP
TOP_K = 2
D_EXPERT = 512
MOE_BLOCK = 256
DN_ALPHA = (2 * DEPTH) ** 0.25
LN_EPS = 1e-5
RMS_EPS = 1e-6

LANES = 128
VMEM_LIMIT = 56 * 1024 * 1024

QKVZ_W = 4 * GDN_HEADS * GDN_DK
KAB_W = 2 * LANES
AB_W = LANES
GATES_W = 2 * D_MODEL
PROJ_SEGS = (QKVZ_W, Q_LORA, KV_LORA, KAB_W, AB_W, GATES_W)
PROJ_W = sum(PROJ_SEGS)


def _params(sem):
    return pltpu.CompilerParams(dimension_semantics=sem, vmem_limit_bytes=VMEM_LIMIT)


def _layer_norm(x, g, b):
    mu = jnp.mean(x, -1, keepdims=True)
    xc = x - mu
    var = jnp.mean(xc * xc, -1, keepdims=True)
    return xc * lax.rsqrt(var + LN_EPS) * g + b


def _rms_norm(x, w):
    return x * lax.rsqrt(jnp.mean(x * x, -1, keepdims=True) + RMS_EPS) * w


def _silu(x):
    return x / (1.0 + jnp.exp(-x))


def _sigmoid(x):
    return 1.0 / (1.0 + jnp.exp(-x))


def _dot(a, b):
    return jnp.dot(a, b, preferred_element_type=F32)


def _dot_nt(a, b):
    return lax.dot_general(a, b, (((1,), (1,)), ((), ())), preferred_element_type=F32)


def _ds(i, n):
    return pl.ds(i * n, n) if isinstance(i, int) else pl.ds(pl.multiple_of(i * n, n), n)


def _dot_tn(a, b):
    return lax.dot_general(a, b, (((0,), (0,)), ((), ())), preferred_element_type=F32)


def _rope_kernel(pos_ref, freq_ref, mq_ref, mk1_ref, mk2_ref):
    ang = pos_ref[...].astype(F32) * freq_ref[...]
    c = jnp.cos(ang)
    s = jnp.sin(ang)
    quarter = lax.broadcasted_iota(jnp.int32, ang.shape, 1) >> 5
    even = (quarter & 1) == 0
    mq_ref[...] = jnp.where((quarter == 0) | (quarter == 3), c, s)
    mk1_ref[...] = jnp.where(even, c, s)
    mk2_ref[...] = jnp.where(even, s, c)


def _rope_tables(positions, tm):
    t = positions.size
    half = QK_ROPE // 2
    inv_freq = jnp.power(ROPE_BASE, -jnp.arange(half, dtype=F32) / half)
    freq = jnp.tile(inv_freq, LANES // half)[None, :]
    out = jax.ShapeDtypeStruct((t, LANES), F32)
    row = pl.BlockSpec((tm, LANES), lambda i: (i, 0))
    return pl.pallas_call(
        _rope_kernel, out_shape=(out, out, out), grid=(t // tm,),
        in_specs=[pl.BlockSpec((tm, 1), lambda i: (i, 0)), pl.BlockSpec((1, LANES), lambda i: (0, 0))],
        out_specs=(row, row, row), compiler_params=_params(("parallel",)), name="rope_tables",
    )(positions.reshape(t, 1), freq)


def _ln_kernel(x_ref, g_ref, b_ref, o_ref):
    o_ref[...] = _layer_norm(x_ref[...], g_ref[...], b_ref[...])


def _ln_in(x2, g, b, tm):
    t, d = x2.shape
    row = pl.BlockSpec((tm, d), lambda i: (i, 0))
    vec = pl.BlockSpec((1, d), lambda i: (0, 0))
    return pl.pallas_call(
        _ln_kernel, out_shape=jax.ShapeDtypeStruct((t, d), F32), grid=(t // tm,),
        in_specs=[row, vec, vec], out_specs=row, compiler_params=_params(("parallel",)), name="ln_in",
    )(x2, g.reshape(1, d), b.reshape(1, d))


PROJ_CHUNK = 512


def _proj_kernel(h_ref, w_ref, *out_refs):
    a = h_ref[...].astype(BF16)
    off = 0
    for ref, width in zip(out_refs, PROJ_SEGS):
        for c in range(0, width, PROJ_CHUNK):
            wc = min(PROJ_CHUNK, width - c)
            ref[:, c:c + wc] = _dot(a, w_ref[:, off + c:off + c + wc]).astype(ref.dtype)
        off += width


def _proj(h, w_all, tm):
    t, d = h.shape
    dts = (BF16, F32, F32, F32, F32, BF16)
    outs = tuple(jax.ShapeDtypeStruct((t, w), dt) for w, dt in zip(PROJ_SEGS, dts))
    return pl.pallas_call(
        _proj_kernel, out_shape=outs, grid=(t // tm,),
        in_specs=[pl.BlockSpec((tm, d), lambda i: (i, 0)), pl.BlockSpec((d, PROJ_W), lambda i: (0, 0))],
        out_specs=tuple(pl.BlockSpec((tm, w), lambda i: (i, 0)) for w in PROJ_SEGS),
        compiler_params=_params(("parallel",)), name="in_proj",
    )(h, w_all)


def _pack_w_in(w_in):
    hq = GDN_HEADS * GDN_DK
    o_a = 4 * hq
    o_bt = o_a + 2 * GDN_HEADS
    o_cq = o_bt + 2 * GDN_HEADS
    o_ckv = o_cq + Q_LORA
    o_kr = o_ckv + KV_LORA
    o_g = o_kr + QK_ROPE
    half = QK_ROPE // 2
    k1 = w_in[:, o_kr:o_kr + half]
    k2 = w_in[:, o_kr + half:o_g]
    d = w_in.shape[0]
    ab = jnp.concatenate([w_in[:, o_a:o_cq], jnp.zeros((d, AB_W - 4 * GDN_HEADS), w_in.dtype)], 1)
    return jnp.concatenate(
        [w_in[:, :o_a], w_in[:, o_cq:o_ckv], w_in[:, o_ckv:o_kr],
         k1, k1, k1, k1, -k2, k2, -k2, k2, ab, w_in[:, o_g:]], axis=1).astype(BF16)


GDN_HPS = 2
CONV_PAD = 8
PREP_CHUNKS = 8

def _gdn_kernel(q_ref, k_ref, v_ref, z_ref, ab_ref, cwq_ref, cwk_ref, cwv_ref, alog_ref, dtb_ref, nw_ref,
                o_ref, qs, ks, vs, gs, bs, brs, bns, mqs, ges, os_, xp):
    s_len = q_ref.shape[0]
    nc = s_len // CHUNK
    first_head = pl.program_id(1) * GDN_HPS

    xp[:CONV_PAD, :] = jnp.zeros((CONV_PAD, LANES), F32)
    xp[CONV_PAD + s_len:, :] = jnp.zeros((CONV_PAD, LANES), F32)

    def conv_silu(x_ref, cw_ref, cols):
        xp[CONV_PAD:CONV_PAD + s_len, :] = x_ref[:, cols].astype(F32)
        acc = None
        for j in range(CONV_K):
            term = xp[CONV_PAD + j - CONV_K // 2:CONV_PAD + j - CONV_K // 2 + s_len, :] * cw_ref[j:j + 1, cols]
            acc = term if acc is None else acc + term
        return _silu(acc)

    def l2n(x):
        return x * lax.rsqrt(jnp.sum(x * x, -1, keepdims=True) + RMS_EPS)

    lane = lax.broadcasted_iota(jnp.int32, (1, LANES), 1)

    def pick(vals, idx):
        col = jnp.sum(jnp.where(lane == idx, vals, 0.0), -1, keepdims=True)
        return jnp.broadcast_to(col, (s_len, LANES))

    def stage(hh):
        cols = slice(hh * LANES, (hh + 1) * LANES)
        qs[...] = l2n(conv_silu(q_ref, cwq_ref, cols)) * (GDN_DK ** -0.5)
        ks[...] = l2n(conv_silu(k_ref, cwk_ref, cols))
        vs[...] = conv_silu(v_ref, cwv_ref, cols)
        ab = ab_ref[...]
        x = ab + dtb_ref[...]
        softplus = jnp.maximum(x, 0.0) + jnp.log(1.0 + jnp.exp(-jnp.abs(x)))
        g_all = -jnp.exp(alog_ref[...]) * softplus
        b_all = _sigmoid(ab)
        for d in range(2):
            gs[d] = pick(g_all, first_head + hh + d * GDN_HEADS)
            bs[d] = pick(b_all, first_head + hh + (2 + d) * GDN_HEADS)

    ci = lax.broadcasted_iota(jnp.int32, (CHUNK, CHUNK), 0)
    cj = lax.broadcasted_iota(jnp.int32, (CHUNK, CHUNK), 1)
    eye = jnp.where(ci == cj, 1.0, 0.0)
    incl = (ci >= cj, ci <= cj)
    strict = (ci > cj, ci < cj)
    tri = tuple(jnp.where(m, 1.0, 0.0).astype(BF16) for m in incl)
    last_row = (CHUNK - 1, 0)

    unroll = 4 if nc % 4 == 0 else 1

    def cumulate(c, carry):
        rows = [pl.ds(pl.multiple_of((c * unroll + j) * CHUNK, CHUNK), CHUNK) for j in range(unroll)]
        chains = [(j, d) for j in range(unroll) for d in range(2)]
        parts = []
        for j, d in chains:
            g = gs[d, rows[j], :]
            g1 = g.astype(BF16)
            r1 = g - g1.astype(F32)
            g2 = r1.astype(BF16)
            parts.append((g1, g2, (r1 - g2.astype(F32)).astype(BF16)))
        b = [_dot(tri[d], p[0]) + _dot(tri[d], p[1]) + _dot(tri[d], p[2])
             for (j, d), p in zip(chains, parts)]
        for (j, d), x in zip(chains, b):
            gs[d, rows[j], :] = x
            brs[d, rows[j], :] = jnp.transpose(x)[:CHUNK, :]
        return carry

    def prep(hh, it, carry):
        unroll = PREP_CHUNKS if nc % PREP_CHUNKS == 0 else 1
        cs = [it * unroll + j for j in range(unroll)]
        rows = [pl.ds(pl.multiple_of(c * CHUNK, CHUNK), CHUNK) for c in cs]
        q = [qs[r, :] for r in rows]
        k = [ks[r, :] for r in rows]
        kb = [x.astype(BF16) for x in k]
        kk = [_dot_nt(x, x) for x in kb]
        qk = [_dot_nt(x.astype(BF16), y) for x, y in zip(q, kb)]
        chains = [(j, d) for j in range(unroll) for d in range(2)]
        b = [gs[d, rows[j], :] for j, d in chains]
        beta = [bs[d, rows[j], :] for j, d in chains]
        gamma = [jnp.exp(jnp.where(incl[d], b[n][:, :CHUNK] - brs[d, rows[j], :], -jnp.inf))
                 for n, (j, d) in enumerate(chains)]
        p_mat = [jnp.where(strict[d], -beta[n][:, :CHUNK] * kk[j] * gamma[n], 0.0)
                 for n, (j, d) in enumerate(chains)]
        t_mat = [eye + x for x in p_mat]
        for _ in range(int(np.log2(CHUNK)) - 1):
            pb = [x.astype(BF16) for x in p_mat]
            p_mat = [_dot(x, x) for x in pb]
            tb = [x.astype(BF16) for x in t_mat]
            pb = [x.astype(BF16) for x in p_mat]
            t_mat = [t + _dot(x, y) for t, x, y in zip(t_mat, tb, pb)]
        eb = [jnp.exp(x) for x in b]
        rhs = [jnp.concatenate([beta[n] * vs[rows[j], :], beta[n] * eb[n] * k[j]], axis=1).astype(BF16)
               for n, (j, d) in enumerate(chains)]
        wk = [_dot(t.astype(BF16), r).astype(BF16) for t, r in zip(t_mat, rhs)]
        b_last = [b[n][last_row[d]:last_row[d] + 1, :] for n, (j, d) in enumerate(chains)]
        kdt = [jnp.transpose(k[j] * jnp.exp(b_last[n] - b[n])).astype(BF16)
               for n, (j, d) in enumerate(chains)]
        bm = [_dot(x, y) for x, y in zip(kdt, wk)]
        pw = [_dot((qk[j] * gamma[n]).astype(BF16), wk[n]) for n, (j, d) in enumerate(chains)]
        for n, (j, d) in enumerate(chains):
            c = cs[j]
            bns[hh, d, pl.ds(pl.multiple_of(c * GDN_DK, GDN_DK), GDN_DK), :] = bm[n][:, :GDN_DV]
            r3 = pl.multiple_of(c * (GDN_DK + CHUNK), CHUNK)
            mqs[hh, d, pl.ds(r3, GDN_DK), :] = bm[n][:, GDN_DV:].astype(BF16)
            mqs[hh, d, pl.ds(r3 + GDN_DK, CHUNK), :] = (q[j] * eb[n] - pw[n][:, GDN_DV:]).astype(BF16)
            ges[hh, d, pl.ds(pl.multiple_of(c * 8, 8), 8), :] = jnp.broadcast_to(jnp.exp(b_last[n]), (8, LANES))
        for j in range(unroll):
            os_[hh, rows[j], :] = pw[2 * j][:, :GDN_DV] + pw[2 * j + 1][:, :GDN_DV]
        return carry

    for hh in range(GDN_HPS):
        stage(hh)
        lax.fori_loop(0, nc // unroll, cumulate, 0)
        lax.fori_loop(0, nc // (PREP_CHUNKS if nc % PREP_CHUNKS == 0 else 1), functools.partial(prep, hh), 0)

    chains = [(hh, d) for hh in range(GDN_HPS) for d in range(2)]

    def step(i, states):
        cs = (i, nc - 1 - i)
        sb = [x.astype(BF16) for x in states]
        r = [_dot(mqs[hh, d, pl.ds(pl.multiple_of(cs[d] * (GDN_DK + CHUNK), CHUNK), GDN_DK + CHUNK), :], sb[n])
             for n, (hh, d) in enumerate(chains)]
        new = []
        for n, (hh, d) in enumerate(chains):
            c = cs[d]
            os_[hh, pl.ds(pl.multiple_of(c * CHUNK, CHUNK), CHUNK), :] += r[n][GDN_DK:]
            ge = ges[hh, d, pl.ds(pl.multiple_of(c * 8, 8), 1), :]
            bn = bns[hh, d, pl.ds(pl.multiple_of(c * GDN_DK, GDN_DK), GDN_DK), :]
            new.append(ge * states[n] - r[n][:GDN_DK] + bn)
        return tuple(new)

    s0 = jnp.zeros((GDN_DK, GDN_DV), F32)
    lax.fori_loop(0, nc, step, (s0,) * len(chains))

    for hh in range(GDN_HPS):
        cols = slice(hh * LANES, (hh + 1) * LANES)
        o_ref[:, cols] = (_rms_norm(os_[hh], nw_ref[...]) * _silu(z_ref[:, cols].astype(F32))).astype(o_ref.dtype)


def _gdn(qkvz, ab, conv_w, a_log, dt_bias, norm_w, bsz, s_len):
    nh = GDN_HEADS
    qkvz3 = qkvz.reshape(bsz, s_len, QKVZ_W)
    ab3 = ab.reshape(bsz, s_len, AB_W)
    pad = jnp.zeros((AB_W - 2 * nh,), F32)
    alog = jnp.concatenate([a_log.reshape(-1), pad])[None, :]
    dtb = jnp.concatenate([dt_bias.reshape(-1), pad])[None, :]

    hps = GDN_HPS
    ng = nh // hps

    def col(k):
        return pl.BlockSpec((None, s_len, hps * LANES), lambda b, h: (b, 0, k * ng + h))

    def cw(k):
        return pl.BlockSpec((CONV_K, hps * LANES), lambda b, h: (0, k * ng + h))

    vec = pl.BlockSpec((1, LANES), lambda b, h: (0, 0))
    nc = s_len // CHUNK
    scratch = [
        pltpu.VMEM((s_len, LANES), F32), pltpu.VMEM((s_len, LANES), F32), pltpu.VMEM((s_len, LANES), F32),
        pltpu.VMEM((2, s_len, LANES), F32), pltpu.VMEM((2, s_len, LANES), F32),
        pltpu.VMEM((2, s_len, CHUNK), F32), pltpu.VMEM((hps, 2, nc * GDN_DK, GDN_DV), F32),
        pltpu.VMEM((hps, 2, nc * (GDN_DK + CHUNK), GDN_DV), BF16),
        pltpu.VMEM((hps, 2, nc * 8, LANES), F32), pltpu.VMEM((hps, s_len, GDN_DV), F32),
        pltpu.VMEM((s_len + 2 * CONV_PAD, LANES), F32),
    ]
    out = pl.pallas_call(
        _gdn_kernel, out_shape=jax.ShapeDtypeStruct((bsz, s_len, nh * GDN_DV), BF16), grid=(bsz, ng),
        in_specs=[col(0), col(1), col(2), col(3),
                  pl.BlockSpec((None, s_len, AB_W), lambda b, h: (b, 0, 0)),
                  cw(0), cw(1), cw(2), vec, vec, vec],
        out_specs=pl.BlockSpec((None, s_len, hps * GDN_DV), lambda b, h: (b, 0, h)),
        scratch_shapes=scratch, compiler_params=_params(("parallel", "parallel")), name="gdn",
    )(qkvz3, qkvz3, qkvz3, qkvz3, ab3, conv_w, conv_w, conv_w, alog, dtb, norm_w.reshape(1, GDN_DV))
    return out.reshape(bsz * s_len, nh * GDN_DV)


MLA_TQ = 256
MLA_KC = 512


def _mla_kernel(cq_ref, ckv_ref, kab_ref, mq_ref, mk1_ref, mk2_ref, qnw_ref, kvnw_ref, wq_ref, wkv_ref,
                o_ref, q_s, k_s, vt_s, sta_s, stb_s, cqn_s, ckvn_s):
    s_len = cq_ref.shape[0]
    scale = (QK_NOPE + QK_ROPE) ** -0.5

    @pl.when(pl.program_id(1) == 0)
    def _():
        cqn_s[...] = _rms_norm(cq_ref[...], qnw_ref[...]).astype(BF16)
        ckvn_s[...] = _rms_norm(ckv_ref[...], kvnw_ref[...]).astype(BF16)
        kab = kab_ref[...]
        k_s[:, QK_NOPE:] = (kab[:, :LANES] * mk1_ref[...] + kab[:, LANES:] * mk2_ref[...]).astype(BF16)
        vt_s[V_HEAD:, :] = jnp.ones((vt_s.shape[0] - V_HEAD, s_len), BF16)

    q = _dot(cqn_s[...], wq_ref[...]) * scale
    q_s[:, :QK_NOPE] = q[:, :QK_NOPE].astype(BF16)
    q_s[:, QK_NOPE:] = (q[:, QK_NOPE:] * mq_ref[...]).astype(BF16)
    kv = _dot(ckvn_s[...], wkv_ref[...])
    k_s[:, :QK_NOPE] = kv[:, :QK_NOPE].astype(BF16)
    vt_s[:V_HEAD, :] = jnp.transpose(kv[:, QK_NOPE:]).astype(BF16)

    tq = min(MLA_TQ, s_len)

    nq = s_len // tq
    kc = min(MLA_KC, s_len)

    def scores(i, st):
        st[...] = _dot_nt(k_s[...], q_s[_ds(i, tq), :])

    def attend(i, st):
        m = jnp.max(st[...], 0, keepdims=True)
        ot = jnp.zeros((vt_s.shape[0], tq), F32)
        for c in range(0, s_len, kc):
            p = jnp.exp(st[c:c + kc, :] - m).astype(BF16)
            ot = ot + _dot(vt_s[:, c:c + kc], p)
        o = jnp.transpose(ot[:V_HEAD] / ot[V_HEAD:V_HEAD + 1])
        o_ref[_ds(i, tq), :] = o.astype(o_ref.dtype)

    def pair(j, prefetch):
        scores(2 * j + 1, stb_s)
        attend(2 * j, sta_s)
        if prefetch:
            scores(2 * j + 2, sta_s)
        attend(2 * j + 1, stb_s)

    assert nq == 1 or nq % 2 == 0
    scores(0, sta_s)
    if nq == 1:
        attend(0, sta_s)
    else:
        def body(j, carry):
            pair(j, True)
            return carry
        lax.fori_loop(0, nq // 2 - 1, body, 0)
        pair(nq // 2 - 1, False)


def _pack_w_uq(w_uq):
    w = w_uq.reshape(Q_LORA, MLA_HEADS, QK_NOPE + QK_ROPE)
    half = QK_ROPE // 2
    r1 = w[..., QK_NOPE:QK_NOPE + half]
    r2 = w[..., QK_NOPE + half:]
    w = jnp.concatenate([w[..., :QK_NOPE], r1, r1, -r2, r2], axis=-1)
    return w.reshape(Q_LORA, MLA_HEADS * 2 * LANES).astype(BF16)


def _mla(cq, ckv, kab, tables, qnw, kvnw, wq, wkv, bsz, s_len):
    nh = MLA_HEADS

    def per_b(width):
        return pl.BlockSpec((None, s_len, width), lambda b, h: (b, 0, 0))

    def vec(width):
        return pl.BlockSpec((1, width), lambda b, h: (0, 0))

    mq, mk1, mk2 = (t.reshape(bsz, s_len, LANES) for t in tables)
    out = pl.pallas_call(
        _mla_kernel, out_shape=jax.ShapeDtypeStruct((bsz, s_len, nh * V_HEAD), BF16), grid=(bsz, nh),
        in_specs=[per_b(Q_LORA), per_b(KV_LORA), per_b(KAB_W), per_b(LANES), per_b(LANES), per_b(LANES),
                  vec(Q_LORA), vec(KV_LORA),
                  pl.BlockSpec((Q_LORA, 2 * LANES), lambda b, h: (0, h)),
                  pl.BlockSpec((KV_LORA, QK_NOPE + V_HEAD), lambda b, h: (0, h))],
        out_specs=pl.BlockSpec((None, s_len, V_HEAD), lambda b, h: (b, 0, h)),
        scratch_shapes=[pltpu.VMEM((s_len, 2 * LANES), BF16), pltpu.VMEM((s_len, 2 * LANES), BF16),
                        pltpu.VMEM((V_HEAD + 16, s_len), BF16),
                        pltpu.VMEM((s_len, min(MLA_TQ, s_len)), F32),
                        pltpu.VMEM((s_len, min(MLA_TQ, s_len)), F32),
                        pltpu.VMEM((s_len, Q_LORA), BF16), pltpu.VMEM((s_len, KV_LORA), BF16)],
        compiler_params=_params(("parallel", "arbitrary")), name="mla",
    )(cq.reshape(bsz, s_len, Q_LORA), ckv.reshape(bsz, s_len, KV_LORA), kab.reshape(bsz, s_len, KAB_W),
      mq, mk1, mk2, qnw.reshape(1, Q_LORA), kvnw.reshape(1, KV_LORA), wq, wkv)
    return out.reshape(bsz * s_len, nh * V_HEAD)


def _tail_kernel(og_ref, om_ref, gates_ref, h_ref, wog_ref, wom_ref, wout_ref, g_ref, b_ref, o_ref, ob_ref):
    d = h_ref.shape[1]
    y_gdn = _dot(og_ref[...], wog_ref[...])
    y_mla = _dot(om_ref[...], wom_ref[...])
    y = (_sigmoid(gates_ref[:, :d].astype(F32)) * y_gdn + _sigmoid(gates_ref[:, d:].astype(F32)) * y_mla)
    m = _dot(y.astype(BF16), wout_ref[...])
    out = _layer_norm(DN_ALPHA * h_ref[...] + m, g_ref[...], b_ref[...])
    o_ref[...] = out
    ob_ref[...] = out.astype(BF16)


def _mixer_tail(og, om, gates, h, wog, wom, wout, g, b, tm):
    t, d = h.shape
    row = lambda w: pl.BlockSpec((tm, w), lambda i: (i, 0))
    full = pl.BlockSpec((d, d), lambda i: (0, 0))
    vec = pl.BlockSpec((1, d), lambda i: (0, 0))
    return pl.pallas_call(
        _tail_kernel, out_shape=(jax.ShapeDtypeStruct((t, d), F32), jax.ShapeDtypeStruct((t, d), BF16)),
        grid=(t // tm,),
        in_specs=[row(d), row(d), row(2 * d), row(d), full, full, full, vec, vec],
        out_specs=(row(d), row(d)), compiler_params=_params(("parallel",)), name="mixer_tail",
    )(og, om, gates, h, wog, wom, wout, g.reshape(1, d), b.reshape(1, d))


def _router_kernel(h_ref, w_ref, b_ref, eid_ref, gate_ref, cnt_ref, run):
    @pl.when(pl.program_id(0) == 0)
    def _():
        run[...] = jnp.zeros_like(run)

    logits = jnp.dot(h_ref[...], w_ref[...], precision=lax.Precision.HIGHEST,
                     preferred_element_type=F32) + b_ref[...]
    tm = logits.shape[0]
    lane = lax.broadcasted_iota(jnp.int32, logits.shape, 1)
    lane_f = lane.astype(F32)
    neg = -jnp.inf

    def first_max(vals):
        m = jnp.max(vals, -1, keepdims=True)
        idx = jnp.min(jnp.where(vals == m, lane_f, float(LANES)), -1, keepdims=True)
        return m, idx

    is_grp = lane < N_GROUPS
    mg, grp = first_max(jnp.where(is_grp, logits, neg))
    p_grp = 1.0 / jnp.sum(jnp.where(is_grp, jnp.exp(logits - mg), 0.0), -1, keepdims=True)
    in_grp = ((lane - N_GROUPS) >> 3).astype(F32) == grp
    le = jnp.where(in_grp, logits, neg)
    l1, i1 = first_max(le)
    l2, i2 = first_max(jnp.where(lane_f == i1, neg, le))
    t = jnp.exp(l2 - l1)
    g1 = p_grp / (1.0 + t)
    g2 = p_grp * t / (1.0 + t)
    hit1 = lane_f == i1
    hit2 = lane_f == i2
    onehot = jnp.where(hit1 | hit2, 1.0, 0.0)
    ri = lax.broadcasted_iota(jnp.int32, (tm, tm), 0)
    rj = lax.broadcasted_iota(jnp.int32, (tm, tm), 1)
    before = jnp.where(ri > rj, 1.0, 0.0).astype(BF16)
    prefix = _dot(before, onehot.astype(BF16)) + run[...]
    rank1 = jnp.sum(jnp.where(hit1, prefix, 0.0), -1, keepdims=True)
    rank2 = jnp.sum(jnp.where(hit2, prefix, 0.0), -1, keepdims=True)
    run[...] += jnp.sum(onehot, 0, keepdims=True)
    cnt_ref[...] = run[...]
    ids = jnp.where(lane == 0, i1 - float(N_GROUPS), jnp.where(lane == 1, i2 - float(N_GROUPS),
                    jnp.where(lane == 2, rank1, jnp.where(lane == 3, rank2, 0.0))))
    eid_ref[...] = ids.astype(jnp.int32)
    gate_ref[...] = jnp.where(lane == 0, g1, jnp.where(lane == 1, g2, 0.0))


def _router(h, w_rg, b_rg, w_re, b_re, tm):
    t, d = h.shape
    padw = LANES - N_GROUPS - N_EXPERTS
    w = jnp.concatenate([w_rg, w_re, jnp.zeros((d, padw), F32)], axis=1)
    b = jnp.concatenate([b_rg, b_re, jnp.zeros((padw,), F32)])[None, :]
    row = pl.BlockSpec((tm, LANES), lambda i: (i, 0))
    one = pl.BlockSpec((1, LANES), lambda i: (0, 0))
    return pl.pallas_call(
        _router_kernel,
        out_shape=(jax.ShapeDtypeStruct((t, LANES), jnp.int32), jax.ShapeDtypeStruct((t, LANES), F32),
                   jax.ShapeDtypeStruct((1, LANES), F32)),
        grid=(t // tm,),
        in_specs=[pl.BlockSpec((tm, d), lambda i: (i, 0)), pl.BlockSpec((d, LANES), lambda i: (0, 0)), one],
        out_specs=(row, row, one), scratch_shapes=[pltpu.VMEM((1, LANES), F32)],
        compiler_params=_params(("arbitrary",)), name="router",
    )(h, w, b)


def _expert_kernel(be_ref, nu_ref, x_ref, wg_ref, wu_ref, wd_ref, o_ref, wgb, wub, wdb):
    i = pl.program_id(0)

    @pl.when((i == 0) | (be_ref[i] != be_ref[jnp.maximum(i - 1, 0)]))
    def _():
        wgb[...] = wg_ref[...].astype(BF16)
        wub[...] = wu_ref[...].astype(BF16)
        wdb[...] = wd_ref[...].astype(BF16)

    @pl.when(i < nu_ref[0])
    def _():
        x = x_ref[...]
        hid = _silu(_dot(x, wgb[...])) * _dot(x, wub[...])
        o_ref[...] = _dot(hid.astype(BF16), wdb[...]).astype(o_ref.dtype)

    @pl.when(i >= nu_ref[0])
    def _():
        o_ref[...] = jnp.zeros_like(o_ref)


def _experts(xb, blk_expert, n_used, w_gate, w_up, w_down, layer):
    n_rows, d = xb.shape
    n_blocks = n_rows // MOE_BLOCK
    wspec = lambda shape: pl.BlockSpec((None, None) + shape, lambda i, be, nu: (layer, be[i], 0, 0))
    grid_spec = pltpu.PrefetchScalarGridSpec(
        num_scalar_prefetch=2, grid=(n_blocks,),
        in_specs=[pl.BlockSpec((MOE_BLOCK, d), lambda i, be, nu: (i, 0)),
                  wspec((d, D_EXPERT)), wspec((d, D_EXPERT)), wspec((D_EXPERT, d))],
        out_specs=pl.BlockSpec((MOE_BLOCK, d), lambda i, be, nu: (i, 0)),
        scratch_shapes=[pltpu.VMEM((d, D_EXPERT), BF16), pltpu.VMEM((d, D_EXPERT), BF16),
                        pltpu.VMEM((D_EXPERT, d), BF16)])
    return pl.pallas_call(
        _expert_kernel, out_shape=jax.ShapeDtypeStruct((n_rows, d), BF16), grid_spec=grid_spec,
        compiler_params=_params(("arbitrary",)), name="experts",
    )(blk_expert, n_used, xb, w_gate, w_up, w_down)


def _combine_kernel(h_ref, y1_ref, y2_ref, gate_ref, g_ref, b_ref, o_ref):
    f = gate_ref[:, 0:1] * y1_ref[...].astype(F32) + gate_ref[:, 1:2] * y2_ref[...].astype(F32)
    o_ref[...] = _layer_norm(DN_ALPHA * h_ref[...] + f, g_ref[...], b_ref[...])


def _combine(h, y1, y2, gate, g, b, tm):
    t, d = h.shape
    row = pl.BlockSpec((tm, d), lambda i: (i, 0))
    vec = pl.BlockSpec((1, d), lambda i: (0, 0))
    return pl.pallas_call(
        _combine_kernel, out_shape=jax.ShapeDtypeStruct((t, d), F32), grid=(t // tm,),
        in_specs=[row, row, row, pl.BlockSpec((tm, LANES), lambda i: (i, 0)), vec, vec], out_specs=row,
        compiler_params=_params(("parallel",)), name="combine_ln",
    )(h, y1, y2, gate, g.reshape(1, d), b.reshape(1, d))


def _dispatch(ids, cnt):
    t = ids.shape[0]
    n_assign = t * TOP_K
    counts = cnt[0, N_GROUPS:N_GROUPS + N_EXPERTS].astype(jnp.int32)
    padded = (counts + MOE_BLOCK - 1) // MOE_BLOCK * MOE_BLOCK
    pad_end = jnp.cumsum(padded)
    pad_start = pad_end - padded
    start = jnp.cumsum(counts) - counts
    eid = ids[:, :TOP_K]
    dest = jnp.take(pad_start, eid) + ids[:, TOP_K:2 * TOP_K]
    order = jnp.argsort(eid.reshape(n_assign))
    n_blocks = -(-(n_assign + N_EXPERTS * (MOE_BLOCK - 1)) // MOE_BLOCK)
    blk_first = (jnp.arange(n_blocks) * MOE_BLOCK)[:, None]
    blk_expert = jnp.minimum(jnp.sum(pad_end[None, :] <= blk_first, axis=1), N_EXPERTS - 1).astype(jnp.int32)
    shift = jnp.repeat(jnp.take(pad_start - start, blk_expert), MOE_BLOCK)
    end = jnp.repeat(jnp.take(start + counts, blk_expert), MOE_BLOCK)
    rows = jnp.arange(n_blocks * MOE_BLOCK)
    src = rows - shift
    row_tok = jnp.where(src < end, jnp.take(order, jnp.clip(src, 0, n_assign - 1)) // TOP_K, rows % t).astype(jnp.int32)
    n_used = (pad_end[-1:] // MOE_BLOCK).astype(jnp.int32)
    return row_tok, dest, blk_expert, n_used


def _moe(h, hb, w_rg, b_rg, w_re, b_re, w_gate, w_up, w_down, layer, g, b, tm):
    ids, gate, cnt = _router(h, w_rg, b_rg, w_re, b_re, tm)
    row_tok, dest, blk_expert, n_used = _dispatch(ids, cnt)
    xb = jnp.take(hb, row_tok, axis=0)
    yb = _experts(xb, blk_expert, n_used, w_gate, w_up, w_down, layer)
    y1 = jnp.take(yb, dest[:, 0], axis=0)
    y2 = jnp.take(yb, dest[:, 1], axis=0)
    return _combine(h, y1, y2, gate, g, b, tm)


def kernel(x, positions, ln_in_g, ln_in_b, w_in, conv_w, a_log, dt_bias, gdn_norm_w, w_o_gdn, mla_q_norm_w, w_uq, mla_kv_norm_w, w_ukv, w_o_mla, w_out, ln1_g, ln1_b, w_router_group, b_router_group, w_router_expert, b_router_expert, w_gate, w_up, w_down, ln2_g, ln2_b):
    bsz, s_len, d = x.shape
    t = bsz * s_len
    tm = min(256, t)
    tables = _rope_tables(positions, tm)
    h = _ln_in(x.reshape(t, d), ln_in_g, ln_in_b, tm)
    for l in range(DEPTH):
        qkvz, cq, ckv, kab, ab, gates = _proj(h, _pack_w_in(w_in[l]), tm)
        og = _gdn(qkvz, ab, conv_w[l], a_log[l], dt_bias[l], gdn_norm_w[l], bsz, s_len)
        om = _mla(cq, ckv, kab, tables, mla_q_norm_w[l], mla_kv_norm_w[l],
                  _pack_w_uq(w_uq[l]), w_ukv[l].astype(BF16), bsz, s_len)
        h, hb = _mixer_tail(og, om, gates, h, w_o_gdn[l].astype(BF16), w_o_mla[l].astype(BF16),
                            w_out[l].astype(BF16), ln1_g[l], ln1_b[l], tm)
        h = _moe(h, hb, w_router_group[l], b_router_group[l], w_router_expert[l], b_router_expert[l],
                 w_gate, w_up, w_down, l, ln2_g[l], ln2_b[l], tm)
    return h.reshape(bsz, s_len, d)
```

```python
import functools

import numpy as np
import jax
import jax.numpy as jnp
from jax import lax
from jax.experimental import pallas as pl
from jax.experimental.pallas import tpu as pltpu

F32 = jnp.float32
BF16 = jnp.bfloat16

D_MODEL = 1024
DEPTH = 2
GDN_HEADS = 8
GDN_DK = 128
GDN_DV = 128
CONV_K = 5
CHUNK = 64
MLA_HEADS = 8
Q_LORA = 384
KV_LORA = 256
QK_NOPE = 128
QK_ROPE = 64
V_HEAD = 128
ROPE_BASE = 10000.0
N_GROUPS = 8
EXPERTS_PER_GROUP = 8
N_EXPERTS = N_GROUPS * EXPERTS_PER_GROUP
TOP_K = 2
D_EXPERT = 512
MOE_BLOCK = 256
DN_ALPHA = (2 * DEPTH) ** 0.25
LN_EPS = 1e-5
RMS_EPS = 1e-6

LANES = 128
VMEM_LIMIT = 56 * 1024 * 1024

QKVZ_W = 4 * GDN_HEADS * GDN_DK
KAB_W = 2 * LANES
AB_W = LANES
GATES_W = 2 * D_MODEL
PROJ_SEGS = (QKVZ_W, Q_LORA, KV_LORA, KAB_W, AB_W, GATES_W)
PROJ_W = sum(PROJ_SEGS)


def _params(sem):
    return pltpu.CompilerParams(dimension_semantics=sem, vmem_limit_bytes=VMEM_LIMIT)


def _layer_norm(x, g, b):
    mu = jnp.mean(x, -1, keepdims=True)
    xc = x - mu
    var = jnp.mean(xc * xc, -1, keepdims=True)
    return xc * lax.rsqrt(var + LN_EPS) * g + b


def _rms_norm(x, w):
    return x * lax.rsqrt(jnp.mean(x * x, -1, keepdims=True) + RMS_EPS) * w


def _silu(x):
    return x / (1.0 + jnp.exp(-x))


def _sigmoid(x):
    return 1.0 / (1.0 + jnp.exp(-x))


def _dot(a, b):
    return jnp.dot(a, b, preferred_element_type=F32)


def _dot_nt(a, b):
    return lax.dot_general(a, b, (((1,), (1,)), ((), ())), preferred_element_type=F32)


def _ds(i, n):
    return pl.ds(i * n, n) if isinstance(i, int) else pl.ds(pl.multiple_of(i * n, n), n)


def _dot_tn(a, b):
    return lax.dot_general(a, b, (((0,), (0,)), ((), ())), preferred_element_type=F32)


def _rope_kernel(pos_ref, freq_ref, mq_ref, mk1_ref, mk2_ref):
    ang = pos_ref[...].astype(F32) * freq_ref[...]
    c = jnp.cos(ang)
    s = jnp.sin(ang)
    quarter = lax.broadcasted_iota(jnp.int32, ang.shape, 1) >> 5
    even = (quarter & 1) == 0
    mq_ref[...] = jnp.where((quarter == 0) | (quarter == 3), c, s)
    mk1_ref[...] = jnp.where(even, c, s)
    mk2_ref[...] = jnp.where(even, s, c)


def _rope_tables(positions, tm):
    t = positions.size
    half = QK_ROPE // 2
    inv_freq = jnp.power(ROPE_BASE, -jnp.arange(half, dtype=F32) / half)
    freq = jnp.tile(inv_freq, LANES // half)[None, :]
    out = jax.ShapeDtypeStruct((t, LANES), F32)
    row = pl.BlockSpec((tm, LANES), lambda i: (i, 0))
    return pl.pallas_call(
        _rope_kernel, out_shape=(out, out, out), grid=(t // tm,),
        in_specs=[pl.BlockSpec((tm, 1), lambda i: (i, 0)), pl.BlockSpec((1, LANES), lambda i: (0, 0))],
        out_specs=(row, row, row), compiler_params=_params(("parallel",)), name="rope_tables",
    )(positions.reshape(t, 1), freq)


def _ln_kernel(x_ref, g_ref, b_ref, o_ref):
    o_ref[...] = _layer_norm(x_ref[...], g_ref[...], b_ref[...])


def _ln_in(x2, g, b, tm):
    t, d = x2.shape
    row = pl.BlockSpec((tm, d), lambda i: (i, 0))
    vec = pl.BlockSpec((1, d), lambda i: (0, 0))
    return pl.pallas_call(
        _ln_kernel, out_shape=jax.ShapeDtypeStruct((t, d), F32), grid=(t // tm,),
        in_specs=[row, vec, vec], out_specs=row, compiler_params=_params(("parallel",)), name="ln_in",
    )(x2, g.reshape(1, d), b.reshape(1, d))


PROJ_CHUNK = 512


def _proj_kernel(h_ref, w_ref, *out_refs):
    a = h_ref[...].astype(BF16)
    off = 0
    for ref, width in zip(out_refs, PROJ_SEGS):
        for c in range(0, width, PROJ_CHUNK):
            wc = min(PROJ_CHUNK, width - c)
            ref[:, c:c + wc] = _dot(a, w_ref[:, off + c:off + c + wc]).astype(ref.dtype)
        off += width


def _proj(h, w_all, tm):
    t, d = h.shape
    dts = (BF16, F32, F32, F32, F32, BF16)
    outs = tuple(jax.ShapeDtypeStruct((t, w), dt) for w, dt in zip(PROJ_SEGS, dts))
    return pl.pallas_call(
        _proj_kernel, out_shape=outs, grid=(t // tm,),
        in_specs=[pl.BlockSpec((tm, d), lambda i: (i, 0)), pl.BlockSpec((d, PROJ_W), lambda i: (0, 0))],
        out_specs=tuple(pl.BlockSpec((tm, w), lambda i: (i, 0)) for w in PROJ_SEGS),
        compiler_params=_params(("parallel",)), name="in_proj",
    )(h, w_all)


def _pack_w_in(w_in):
    hq = GDN_HEADS * GDN_DK
    o_a = 4 * hq
    o_bt = o_a + 2 * GDN_HEADS
    o_cq = o_bt + 2 * GDN_HEADS
    o_ckv = o_cq + Q_LORA
    o_kr = o_ckv + KV_LORA
    o_g = o_kr + QK_ROPE
    half = QK_ROPE // 2
    k1 = w_in[:, o_kr:o_kr + half]
    k2 = w_in[:, o_kr + half:o_g]
    d = w_in.shape[0]
    ab = jnp.concatenate([w_in[:, o_a:o_cq], jnp.zeros((d, AB_W - 4 * GDN_HEADS), w_in.dtype)], 1)
    return jnp.concatenate(
        [w_in[:, :o_a], w_in[:, o_cq:o_ckv], w_in[:, o_ckv:o_kr],
         k1, k1, k1, k1, -k2, k2, -k2, k2, ab, w_in[:, o_g:]], axis=1).astype(BF16)


GDN_HPS = 2
CONV_PAD = 8
PREP_CHUNKS = 8


def _gdn_kernel(q_ref, k_ref, v_ref, z_ref, ab_ref, cwq_ref, cwk_ref, cwv_ref, alog_ref, dtb_ref, nw_ref,
                o_ref, qs, ks, vs, gs, bs, brs, bns, mqs, ges, os_, xp):
    s_len = q_ref.shape[0]
    nc = s_len // CHUNK
    first_head = pl.program_id(1) * GDN_HPS

    xp[:CONV_PAD, :] = jnp.zeros((CONV_PAD, LANES), F32)
    xp[CONV_PAD + s_len:, :] = jnp.zeros((CONV_PAD, LANES), F32)

    def conv_silu(x_ref, cw_ref, cols):
        xp[CONV_PAD:CONV_PAD + s_len, :] = x_ref[:, cols].astype(F32)
        acc = None
        for j in range(CONV_K):
            term = xp[CONV_PAD + j - CONV_K // 2:CONV_PAD + j - CONV_K // 2 + s_len, :] * cw_ref[j:j + 1, cols]
            acc = term if acc is None else acc + term
        return _silu(acc)

    def l2n(x):
        return x * lax.rsqrt(jnp.sum(x * x, -1, keepdims=True) + RMS_EPS)

    lane = lax.broadcasted_iota(jnp.int32, (1, LANES), 1)

    def pick(vals, idx):
        col = jnp.sum(jnp.where(lane == idx, vals, 0.0), -1, keepdims=True)
        return jnp.broadcast_to(col, (s_len, LANES))

    def stage(hh):
        cols = slice(hh * LANES, (hh + 1) * LANES)
        qs[...] = l2n(conv_silu(q_ref, cwq_ref, cols)) * (GDN_DK ** -0.5)
        ks[...] = l2n(conv_silu(k_ref, cwk_ref, cols))
        vs[...] = conv_silu(v_ref, cwv_ref, cols)
        ab = ab_ref[...]
        x = ab + dtb_ref[...]
        softplus = jnp.maximum(x, 0.0) + jnp.log(1.0 + jnp.exp(-jnp.abs(x)))
        g_all = -jnp.exp(alog_ref[...]) * softplus
        b_all = _sigmoid(ab)
        for d in range(2):
            gs[d] = pick(g_all, first_head + hh + d * GDN_HEADS)
            bs[d] = pick(b_all, first_head + hh + (2 + d) * GDN_HEADS)

    ci = lax.broadcasted_iota(jnp.int32, (CHUNK, 2 * CHUNK), 0)
    cj2 = lax.broadcasted_iota(jnp.int32, (CHUNK, 2 * CHUNK), 1)
    right = cj2 >= CHUNK
    cj = cj2 & (CHUNK - 1)
    eye = jnp.where(ci == cj, 1.0, 0.0)
    incl = (ci >= cj, ci <= cj)
    strict = (ci > cj, ci < cj)
    tri3 = tuple(jnp.concatenate([jnp.where(m[:, :CHUNK], 1.0, 0.0).astype(BF16)] * 3, axis=1) for m in incl)
    ones3 = jnp.ones((CHUNK, 3 * CHUNK), BF16)
    last_row = (CHUNK - 1, 0)

    unroll = PREP_CHUNKS if nc % PREP_CHUNKS == 0 else 1

    def cumulate(c, carry):
        rows = [pl.ds(pl.multiple_of((c * unroll + j) * CHUNK, CHUNK), CHUNK) for j in range(unroll)]
        chains = [(j, d) for j in range(unroll) for d in range(2)]
        def split3(x):
            x1 = x.astype(BF16)
            r1 = x - x1.astype(F32)
            x2 = r1.astype(BF16)
            return jnp.concatenate([x1, x2, (r1 - x2.astype(F32)).astype(BF16)], axis=0)

        parts = [split3(gs[d, rows[j], :]) for j, d in chains]
        b = [_dot(tri3[d], p) for (j, d), p in zip(chains, parts)]
        diag = [split3(x * eye) for x in b]
        b_row = [_dot(ones3, x) for x in diag]
        for (j, d), x, y in zip(chains, b, b_row):
            gs[d, rows[j], :] = x
            brs[d, rows[j], :] = y
        return carry

    def prep(hh, it, carry):
        unroll = PREP_CHUNKS if nc % PREP_CHUNKS == 0 else 1
        cs = [it * unroll + j for j in range(unroll)]
        rows = [pl.ds(pl.multiple_of(c * CHUNK, CHUNK), CHUNK) for c in cs]
        q = [qs[r, :] for r in rows]
        k = [ks[r, :] for r in rows]
        kb = [x.astype(BF16) for x in k]
        kk = [_dot_nt(x, jnp.concatenate([x, x], axis=0)) for x in kb]
        qk = [_dot_nt(x.astype(BF16), y) for x, y in zip(q, kb)]
        chains = [(j, d) for j in range(unroll) for d in range(2)]
        b = [gs[d, rows[j], :] for j, d in chains]
        beta = [bs[d, rows[j], :] for j, d in chains]
        gamma = [jnp.exp(jnp.where(incl[d], b[n] - brs[d, rows[j], :], -jnp.inf))
                 for n, (j, d) in enumerate(chains)]
        pt = [jnp.where(right, eye, jnp.where(strict[d], -beta[n] * kk[j] * gamma[n], 0.0))
              for n, (j, d) in enumerate(chains)]
        for _ in range(int(np.log2(CHUNK))):
            ptb = [x.astype(BF16) for x in pt]
            prod = [_dot(x[:, :CHUNK], x) for x in ptb]
            pt = [y + jnp.where(right, x, 0.0) for x, y in zip(pt, prod)]
        eb = [jnp.exp(x) for x in b]
        rhs = [jnp.concatenate([jnp.zeros((CHUNK, GDN_DV + GDN_DK), F32), jnp.concatenate(
                   [beta[n] * vs[rows[j], :], beta[n] * eb[n] * k[j]], axis=1)], axis=0).astype(BF16)
               for n, (j, d) in enumerate(chains)]
        wk = [_dot(t.astype(BF16), r).astype(BF16) for t, r in zip(pt, rhs)]
        b_last = [b[n][last_row[d]:last_row[d] + 1, :] for n, (j, d) in enumerate(chains)]
        kdt = [jnp.transpose(k[j] * jnp.exp(b_last[n] - b[n])).astype(BF16)
               for n, (j, d) in enumerate(chains)]
        bm = [_dot(x, y) for x, y in zip(kdt, wk)]
        pw = [_dot((qk[j] * gamma[n][:, :CHUNK]).astype(BF16), wk[n])
              for n, (j, d) in enumerate(chains)]
        for n, (j, d) in enumerate(chains):
            c = cs[j]
            bns[hh, d, pl.ds(pl.multiple_of(c * GDN_DK, GDN_DK), GDN_DK), :] = bm[n][:, :GDN_DV]
            r3 = pl.multiple_of(c * (GDN_DK + CHUNK), CHUNK)
            mqs[hh, d, pl.ds(r3, GDN_DK), :] = bm[n][:, GDN_DV:].astype(BF16)
            mqs[hh, d, pl.ds(r3 + GDN_DK, CHUNK), :] = (q[j] * eb[n] - pw[n][:, GDN_DV:]).astype(BF16)
            ges[hh, d, pl.ds(pl.multiple_of(c * 8, 8), 8), :] = jnp.broadcast_to(jnp.exp(b_last[n]), (8, LANES))
        for j in range(unroll):
            os_[hh, rows[j], :] = pw[2 * j][:, :GDN_DV] + pw[2 * j + 1][:, :GDN_DV]
        return carry

    for hh in range(GDN_HPS):
        stage(hh)
        lax.fori_loop(0, nc // unroll, cumulate, 0)
        lax.fori_loop(0, nc // (PREP_CHUNKS if nc % PREP_CHUNKS == 0 else 1), functools.partial(prep, hh), 0)

    chains = [(hh, d) for hh in range(GDN_HPS) for d in range(2)]

    def step(i, states):
        cs = (i, nc - 1 - i)
        sb = [x.astype(BF16) for x in states]
        r = [_dot(mqs[hh, d, pl.ds(pl.multiple_of(cs[d] * (GDN_DK + CHUNK), CHUNK), GDN_DK + CHUNK), :], sb[n])
             for n, (hh, d) in enumerate(chains)]
        new = []
        for n, (hh, d) in enumerate(chains):
            c = cs[d]
            os_[hh, pl.ds(pl.multiple_of(c * CHUNK, CHUNK), CHUNK), :] += r[n][GDN_DK:]
            ge = ges[hh, d, pl.ds(pl.multiple_of(c * 8, 8), 1), :]
            bn = bns[hh, d, pl.ds(pl.multiple_of(c * GDN_DK, GDN_DK), GDN_DK), :]
            new.append(ge * states[n] - r[n][:GDN_DK] + bn)
        return tuple(new)

    s0 = jnp.zeros((GDN_DK, GDN_DV), F32)
    lax.fori_loop(0, nc, step, (s0,) * len(chains))

    for hh in range(GDN_HPS):
        cols = slice(hh * LANES, (hh + 1) * LANES)
        o_ref[:, cols] = (_rms_norm(os_[hh], nw_ref[...]) * _silu(z_ref[:, cols].astype(F32))).astype(o_ref.dtype)


def _gdn(qkvz, ab, conv_w, a_log, dt_bias, norm_w, bsz, s_len):
    nh = GDN_HEADS
    qkvz3 = qkvz.reshape(bsz, s_len, QKVZ_W)
    ab3 = ab.reshape(bsz, s_len, AB_W)
    pad = jnp.zeros((AB_W - 2 * nh,), F32)
    alog = jnp.concatenate([a_log.reshape(-1), pad])[None, :]
    dtb = jnp.concatenate([dt_bias.reshape(-1), pad])[None, :]

    hps = GDN_HPS
    ng = nh // hps

    def col(k):
        return pl.BlockSpec((None, s_len, hps * LANES), lambda b, h: (b, 0, k * ng + h))

    def cw(k):
        return pl.BlockSpec((CONV_K, hps * LANES), lambda b, h: (0, k * ng + h))

    vec = pl.BlockSpec((1, LANES), lambda b, h: (0, 0))
    nc = s_len // CHUNK
    scratch = [
        pltpu.VMEM((s_len, LANES), F32), pltpu.VMEM((s_len, LANES), F32), pltpu.VMEM((s_len, LANES), F32),
        pltpu.VMEM((2, s_len, LANES), F32), pltpu.VMEM((2, s_len, LANES), F32),
        pltpu.VMEM((2, s_len, 2 * CHUNK), F32), pltpu.VMEM((hps, 2, nc * GDN_DK, GDN_DV), F32),
        pltpu.VMEM((hps, 2, nc * (GDN_DK + CHUNK), GDN_DV), BF16),
        pltpu.VMEM((hps, 2, nc * 8, LANES), F32), pltpu.VMEM((hps, s_len, GDN_DV), F32),
        pltpu.VMEM((s_len + 2 * CONV_PAD, LANES), F32),
    ]
    out = pl.pallas_call(
        _gdn_kernel, out_shape=jax.ShapeDtypeStruct((bsz, s_len, nh * GDN_DV), BF16), grid=(bsz, ng),
        in_specs=[col(0), col(1), col(2), col(3),
                  pl.BlockSpec((None, s_len, AB_W), lambda b, h: (b, 0, 0)),
                  cw(0), cw(1), cw(2), vec, vec, vec],
        out_specs=pl.BlockSpec((None, s_len, hps * GDN_DV), lambda b, h: (b, 0, h)),
        scratch_shapes=scratch, compiler_params=_params(("parallel", "parallel")), name="gdn",
    )(qkvz3, qkvz3, qkvz3, qkvz3, ab3, conv_w, conv_w, conv_w, alog, dtb, norm_w.reshape(1, GDN_DV))
    return out.reshape(bsz * s_len, nh * GDN_DV)


MLA_TQ = 256
MLA_KC = 512


def _mla_kernel(cq_ref, ckv_ref, kab_ref, mq_ref, mk1_ref, mk2_ref, qnw_ref, kvnw_ref, wq_ref, wkv_ref,
                o_ref, q_s, k_s, vt_s, sta_s, stb_s, cqn_s, ckvn_s):
    s_len = cq_ref.shape[0]
    scale = (QK_NOPE + QK_ROPE) ** -0.5

    @pl.when(pl.program_id(1) == 0)
    def _():
        cqn_s[...] = _rms_norm(cq_ref[...], qnw_ref[...]).astype(BF16)
        ckvn_s[...] = _rms_norm(ckv_ref[...], kvnw_ref[...]).astype(BF16)
        kab = kab_ref[...]
        k_s[:, QK_NOPE:] = (kab[:, :LANES] * mk1_ref[...] + kab[:, LANES:] * mk2_ref[...]).astype(BF16)
        vt_s[V_HEAD:, :] = jnp.ones((vt_s.shape[0] - V_HEAD, s_len), BF16)

    q = _dot(cqn_s[...], wq_ref[...]) * scale
    q_s[:, :QK_NOPE] = q[:, :QK_NOPE].astype(BF16)
    q_s[:, QK_NOPE:] = (q[:, QK_NOPE:] * mq_ref[...]).astype(BF16)
    kv = _dot(ckvn_s[...], wkv_ref[...])
    k_s[:, :QK_NOPE] = kv[:, :QK_NOPE].astype(BF16)
    vt_s[:V_HEAD, :] = jnp.transpose(kv[:, QK_NOPE:]).astype(BF16)

    tq = min(MLA_TQ, s_len)

    nq = s_len // tq
    kc = min(MLA_KC, s_len)

    def scores(i, st):
        st[...] = _dot_nt(k_s[...], q_s[_ds(i, tq), :])

    def attend(i, st):
        m = jnp.max(st[...], 0, keepdims=True)
        ot = jnp.zeros((vt_s.shape[0], tq), F32)
        for c in range(0, s_len, kc):
            p = jnp.exp(st[c:c + kc, :] - m).astype(BF16)
            ot = ot + _dot(vt_s[:, c:c + kc], p)
        o = jnp.transpose(ot[:V_HEAD] / ot[V_HEAD:V_HEAD + 1])
        o_ref[_ds(i, tq), :] = o.astype(o_ref.dtype)

    def pair(j, prefetch):
        scores(2 * j + 1, stb_s)
        attend(2 * j, sta_s)
        if prefetch:
            scores(2 * j + 2, sta_s)
        attend(2 * j + 1, stb_s)

    assert nq == 1 or nq % 2 == 0
    scores(0, sta_s)
    if nq == 1:
        attend(0, sta_s)
    else:
        def body(j, carry):
            pair(j, True)
            return carry
        lax.fori_loop(0, nq // 2 - 1, body, 0)
        pair(nq // 2 - 1, False)


def _pack_w_uq(w_uq):
    w = w_uq.reshape(Q_LORA, MLA_HEADS, QK_NOPE + QK_ROPE)
    half = QK_ROPE // 2
    r1 = w[..., QK_NOPE:QK_NOPE + half]
    r2 = w[..., QK_NOPE + half:]
    w = jnp.concatenate([w[..., :QK_NOPE], r1, r1, -r2, r2], axis=-1)
    return w.reshape(Q_LORA, MLA_HEADS * 2 * LANES).astype(BF16)


def _mla(cq, ckv, kab, tables, qnw, kvnw, wq, wkv, bsz, s_len):
    nh = MLA_HEADS

    def per_b(width):
        return pl.BlockSpec((None, s_len, width), lambda b, h: (b, 0, 0))

    def vec(width):
        return pl.BlockSpec((1, width), lambda b, h: (0, 0))

    mq, mk1, mk2 = (t.reshape(bsz, s_len, LANES) for t in tables)
    out = pl.pallas_call(
        _mla_kernel, out_shape=jax.ShapeDtypeStruct((bsz, s_len, nh * V_HEAD), BF16), grid=(bsz, nh),
        in_specs=[per_b(Q_LORA), per_b(KV_LORA), per_b(KAB_W), per_b(LANES), per_b(LANES), per_b(LANES),
                  vec(Q_LORA), vec(KV_LORA),
                  pl.BlockSpec((Q_LORA, 2 * LANES), lambda b, h: (0, h)),
                  pl.BlockSpec((KV_LORA, QK_NOPE + V_HEAD), lambda b, h: (0, h))],
        out_specs=pl.BlockSpec((None, s_len, V_HEAD), lambda b, h: (b, 0, h)),
        scratch_shapes=[pltpu.VMEM((s_len, 2 * LANES), BF16), pltpu.VMEM((s_len, 2 * LANES), BF16),
                        pltpu.VMEM((V_HEAD + 16, s_len), BF16),
                        pltpu.VMEM((s_len, min(MLA_TQ, s_len)), F32),
                        pltpu.VMEM((s_len, min(MLA_TQ, s_len)), F32),
                        pltpu.VMEM((s_len, Q_LORA), BF16), pltpu.VMEM((s_len, KV_LORA), BF16)],
        compiler_params=_params(("parallel", "arbitrary")), name="mla",
    )(cq.reshape(bsz, s_len, Q_LORA), ckv.reshape(bsz, s_len, KV_LORA), kab.reshape(bsz, s_len, KAB_W),
      mq, mk1, mk2, qnw.reshape(1, Q_LORA), kvnw.reshape(1, KV_LORA), wq, wkv)
    return out.reshape(bsz * s_len, nh * V_HEAD)


def _tail_kernel(og_ref, om_ref, gates_ref, h_ref, wog_ref, wom_ref, wout_ref, g_ref, b_ref, o_ref, ob_ref):
    d = h_ref.shape[1]
    y_gdn = _dot(og_ref[...], wog_ref[...])
    y_mla = _dot(om_ref[...], wom_ref[...])
    y = (_sigmoid(gates_ref[:, :d].astype(F32)) * y_gdn + _sigmoid(gates_ref[:, d:].astype(F32)) * y_mla)
    m = _dot(y.astype(BF16), wout_ref[...])
    out = _layer_norm(DN_ALPHA * h_ref[...] + m, g_ref[...], b_ref[...])
    o_ref[...] = out
    ob_ref[...] = out.astype(BF16)


def _mixer_tail(og, om, gates, h, wog, wom, wout, g, b, tm):
    t, d = h.shape
    row = lambda w: pl.BlockSpec((tm, w), lambda i: (i, 0))
    full = pl.BlockSpec((d, d), lambda i: (0, 0))
    vec = pl.BlockSpec((1, d), lambda i: (0, 0))
    return pl.pallas_call(
        _tail_kernel, out_shape=(jax.ShapeDtypeStruct((t, d), F32), jax.ShapeDtypeStruct((t, d), BF16)),
        grid=(t // tm,),
        in_specs=[row(d), row(d), row(2 * d), row(d), full, full, full, vec, vec],
        out_specs=(row(d), row(d)), compiler_params=_params(("parallel",)), name="mixer_tail",
    )(og, om, gates, h, wog, wom, wout, g.reshape(1, d), b.reshape(1, d))


def _router_kernel(h_ref, w_ref, b_ref, eid_ref, gate_ref, cnt_ref, run):
    @pl.when(pl.program_id(0) == 0)
    def _():
        run[...] = jnp.zeros_like(run)

    logits = jnp.dot(h_ref[...], w_ref[...], precision=lax.Precision.HIGHEST,
                     preferred_element_type=F32) + b_ref[...]
    tm = logits.shape[0]
    lane = lax.broadcasted_iota(jnp.int32, logits.shape, 1)
    lane_f = lane.astype(F32)
    neg = -jnp.inf

    def first_max(vals):
        m = jnp.max(vals, -1, keepdims=True)
        idx = jnp.min(jnp.where(vals == m, lane_f, float(LANES)), -1, keepdims=True)
        return m, idx

    is_grp = lane < N_GROUPS
    mg, grp = first_max(jnp.where(is_grp, logits, neg))
    p_grp = 1.0 / jnp.sum(jnp.where(is_grp, jnp.exp(logits - mg), 0.0), -1, keepdims=True)
    in_grp = ((lane - N_GROUPS) >> 3).astype(F32) == grp
    le = jnp.where(in_grp, logits, neg)
    l1, i1 = first_max(le)
    l2, i2 = first_max(jnp.where(lane_f == i1, neg, le))
    t = jnp.exp(l2 - l1)
    g1 = p_grp / (1.0 + t)
    g2 = p_grp * t / (1.0 + t)
    hit1 = lane_f == i1
    hit2 = lane_f == i2
    onehot = jnp.where(hit1 | hit2, 1.0, 0.0)
    ri = lax.broadcasted_iota(jnp.int32, (tm, tm), 0)
    rj = lax.broadcasted_iota(jnp.int32, (tm, tm), 1)
    before = jnp.where(ri > rj, 1.0, 0.0).astype(BF16)
    prefix = _dot(before, onehot.astype(BF16)) + run[...]
    rank1 = jnp.sum(jnp.where(hit1, prefix, 0.0), -1, keepdims=True)
    rank2 = jnp.sum(jnp.where(hit2, prefix, 0.0), -1, keepdims=True)
    run[...] += jnp.sum(onehot, 0, keepdims=True)
    cnt_ref[...] = run[...]
    ids = jnp.where(lane == 0, i1 - float(N_GROUPS), jnp.where(lane == 1, i2 - float(N_GROUPS),
                    jnp.where(lane == 2, rank1, jnp.where(lane == 3, rank2, 0.0))))
    eid_ref[...] = ids.astype(jnp.int32)
    gate_ref[...] = jnp.where(lane == 0, g1, jnp.where(lane == 1, g2, 0.0))


def _router(h, w_rg, b_rg, w_re, b_re, tm):
    t, d = h.shape
    padw = LANES - N_GROUPS - N_EXPERTS
    w = jnp.concatenate([w_rg, w_re, jnp.zeros((d, padw), F32)], axis=1)
    b = jnp.concatenate([b_rg, b_re, jnp.zeros((padw,), F32)])[None, :]
    row = pl.BlockSpec((tm, LANES), lambda i: (i, 0))
    one = pl.BlockSpec((1, LANES), lambda i: (0, 0))
    return pl.pallas_call(
        _router_kernel,
        out_shape=(jax.ShapeDtypeStruct((t, LANES), jnp.int32), jax.ShapeDtypeStruct((t, LANES), F32),
                   jax.ShapeDtypeStruct((1, LANES), F32)),
        grid=(t // tm,),
        in_specs=[pl.BlockSpec((tm, d), lambda i: (i, 0)), pl.BlockSpec((d, LANES), lambda i: (0, 0)), one],
        out_specs=(row, row, one), scratch_shapes=[pltpu.VMEM((1, LANES), F32)],
        compiler_params=_params(("arbitrary",)), name="router",
    )(h, w, b)


def _expert_kernel(be_ref, nu_ref, x_ref, wg_ref, wu_ref, wd_ref, o_ref, wgb, wub, wdb):
    i = pl.program_id(0)

    @pl.when((i == 0) | (be_ref[i] != be_ref[jnp.maximum(i - 1, 0)]))
    def _():
        wgb[...] = wg_ref[...].astype(BF16)
        wub[...] = wu_ref[...].astype(BF16)
        wdb[...] = wd_ref[...].astype(BF16)

    @pl.when(i < nu_ref[0])
    def _():
        x = x_ref[...]
        hid = _silu(_dot(x, wgb[...])) * _dot(x, wub[...])
        o_ref[...] = _dot(hid.astype(BF16), wdb[...]).astype(o_ref.dtype)

    @pl.when(i >= nu_ref[0])
    def _():
        o_ref[...] = jnp.zeros_like(o_ref)


def _experts(xb, blk_expert, n_used, w_gate, w_up, w_down, layer):
    n_rows, d = xb.shape
    n_blocks = n_rows // MOE_BLOCK
    wspec = lambda shape: pl.BlockSpec((None, None) + shape, lambda i, be, nu: (layer, be[i], 0, 0))
    grid_spec = pltpu.PrefetchScalarGridSpec(
        num_scalar_prefetch=2, grid=(n_blocks,),
        in_specs=[pl.BlockSpec((MOE_BLOCK, d), lambda i, be, nu: (i, 0)),
                  wspec((d, D_EXPERT)), wspec((d, D_EXPERT)), wspec((D_EXPERT, d))],
        out_specs=pl.BlockSpec((MOE_BLOCK, d), lambda i, be, nu: (i, 0)),
        scratch_shapes=[pltpu.VMEM((d, D_EXPERT), BF16), pltpu.VMEM((d, D_EXPERT), BF16),
                        pltpu.VMEM((D_EXPERT, d), BF16)])
    return pl.pallas_call(
        _expert_kernel, out_shape=jax.ShapeDtypeStruct((n_rows, d), BF16), grid_spec=grid_spec,
        compiler_params=_params(("arbitrary",)), name="experts",
    )(blk_expert, n_used, xb, w_gate, w_up, w_down)


def _combine_kernel(h_ref, y1_ref, y2_ref, gate_ref, g_ref, b_ref, o_ref):
    f = gate_ref[:, 0:1] * y1_ref[...].astype(F32) + gate_ref[:, 1:2] * y2_ref[...].astype(F32)
    o_ref[...] = _layer_norm(DN_ALPHA * h_ref[...] + f, g_ref[...], b_ref[...])


def _combine(h, y1, y2, gate, g, b, tm):
    t, d = h.shape
    row = pl.BlockSpec((tm, d), lambda i: (i, 0))
    vec = pl.BlockSpec((1, d), lambda i: (0, 0))
    return pl.pallas_call(
        _combine_kernel, out_shape=jax.ShapeDtypeStruct((t, d), F32), grid=(t // tm,),
        in_specs=[row, row, row, pl.BlockSpec((tm, LANES), lambda i: (i, 0)), vec, vec], out_specs=row,
        compiler_params=_params(("parallel",)), name="combine_ln",
    )(h, y1, y2, gate, g.reshape(1, d), b.reshape(1, d))


def _dispatch(ids, cnt):
    t = ids.shape[0]
    n_assign = t * TOP_K
    counts = cnt[0, N_GROUPS:N_GROUPS + N_EXPERTS].astype(jnp.int32)
    padded = (counts + MOE_BLOCK - 1) // MOE_BLOCK * MOE_BLOCK
    pad_end = jnp.cumsum(padded)
    pad_start = pad_end - padded
    start = jnp.cumsum(counts) - counts
    eid = ids[:, :TOP_K]
    dest = jnp.take(pad_start, eid) + ids[:, TOP_K:2 * TOP_K]
    order = jnp.argsort(eid.reshape(n_assign))
    n_blocks = -(-(n_assign + N_EXPERTS * (MOE_BLOCK - 1)) // MOE_BLOCK)
    blk_first = (jnp.arange(n_blocks) * MOE_BLOCK)[:, None]
    blk_expert = jnp.minimum(jnp.sum(pad_end[None, :] <= blk_first, axis=1), N_EXPERTS - 1).astype(jnp.int32)
    shift = jnp.repeat(jnp.take(pad_start - start, blk_expert), MOE_BLOCK)
    end = jnp.repeat(jnp.take(start + counts, blk_expert), MOE_BLOCK)
    rows = jnp.arange(n_blocks * MOE_BLOCK)
    src = rows - shift
    row_tok = jnp.where(src < end, jnp.take(order, jnp.clip(src, 0, n_assign - 1)) // TOP_K, rows % t).astype(jnp.int32)
    n_used = (pad_end[-1:] // MOE_BLOCK).astype(jnp.int32)
    return row_tok, dest, blk_expert, n_used


def _moe(h, hb, w_rg, b_rg, w_re, b_re, w_gate, w_up, w_down, layer, g, b, tm):
    ids, gate, cnt = _router(h, w_rg, b_rg, w_re, b_re, tm)
    row_tok, dest, blk_expert, n_used = _dispatch(ids, cnt)
    xb = jnp.take(hb, row_tok, axis=0)
    yb = _experts(xb, blk_expert, n_used, w_gate, w_up, w_down, layer)
    y1 = jnp.take(yb, dest[:, 0], axis=0)
    y2 = jnp.take(yb, dest[:, 1], axis=0)
    return _combine(h, y1, y2, gate, g, b, tm)


def kernel(x, positions, ln_in_g, ln_in_b, w_in, conv_w, a_log, dt_bias, gdn_norm_w, w_o_gdn, mla_q_norm_w, w_uq, mla_kv_norm_w, w_ukv, w_o_mla, w_out, ln1_g, ln1_b, w_router_group, b_router_group, w_router_expert, b_router_expert, w_gate, w_up, w_down, ln2_g, ln2_b):
    bsz, s_len, d = x.shape
    t = bsz * s_len
    tm = min(256, t)
    tables = _rope_tables(positions, tm)
    h = _ln_in(x.reshape(t, d), ln_in_g, ln_in_b, tm)
    for l in range(DEPTH):
        qkvz, cq, ckv, kab, ab, gates = _proj(h, _pack_w_in(w_in[l]), tm)
        og = _gdn(qkvz, ab, conv_w[l], a_log[l], dt_bias[l], gdn_norm_w[l], bsz, s_len)
        om = _mla(cq, ckv, kab, tables, mla_q_norm_w[l], mla_kv_norm_w[l],
                  _pack_w_uq(w_uq[l]), w_ukv[l].astype(BF16), bsz, s_len)
        h, hb = _mixer_tail(og, om, gates, h, w_o_gdn[l].astype(BF16), w_o_mla[l].astype(BF16),
                            w_out[l].astype(BF16), ln1_g[l], ln1_b[l], tm)
        h = _moe(h, hb, w_router_group[l], b_router_group[l], w_router_expert[l], b_router_expert[l],
                 w_gate, w_up, w_down, l, ln2_g[l], ln2_b[l], tm)
    return h.reshape(bsz, s_len, d)
```

```python
import functools

import numpy as np
import jax
import jax.numpy as jnp
from jax import lax
from jax.experimental import pallas as pl
from jax.experimental.pallas import tpu as pltpu

F32 = jnp.float32
BF16 = jnp.bfloat16

D_MODEL = 1024
DEPTH = 2
GDN_HEADS = 8
GDN_DK = 128
GDN_DV = 128
CONV_K = 5
CHUNK = 64
MLA_HEADS = 8
Q_LORA = 384
KV_LORA = 256
QK_NOPE = 128
QK_ROPE = 64
V_HEAD = 128
ROPE_BASE = 10000.0
N_GROUPS = 8
EXPERTS_PER_GROUP = 8
N_EXPERTS = N_GROUPS * EXPERTS_PER_GROUP
TOP_K = 2
D_EXPERT = 512
MOE_BLOCK = 256
DN_ALPHA = (2 * DEPTH) ** 0.25
LN_EPS = 1e-5
RMS_EPS = 1e-6

ROWS_MATMUL = 256
ROWS_MEMBOUND = 512
LANES = 128
VMEM_LIMIT = 56 * 1024 * 1024

QKVZ_W = 4 * GDN_HEADS * GDN_DK
KAB_W = 2 * LANES
AB_W = LANES
GATES_W = 2 * D_MODEL
PROJ_SEGS = (QKVZ_W, Q_LORA, KV_LORA, KAB_W, AB_W, GATES_W)
PROJ_W = sum(PROJ_SEGS)


def _params(sem):
    return pltpu.CompilerParams(dimension_semantics=sem, vmem_limit_bytes=VMEM_LIMIT)


def _layer_norm(x, g, b):
    mu = jnp.mean(x, -1, keepdims=True)
    xc = x - mu
    var = jnp.mean(xc * xc, -1, keepdims=True)
    return xc * lax.rsqrt(var + LN_EPS) * g + b


def _rms_norm(x, w):
    return x * lax.rsqrt(jnp.mean(x * x, -1, keepdims=True) + RMS_EPS) * w


def _silu(x):
    return x / (1.0 + jnp.exp(-x))


def _sigmoid(x):
    return 1.0 / (1.0 + jnp.exp(-x))


def _dot(a, b):
    return jnp.dot(a, b, preferred_element_type=F32)


def _dot_nt(a, b):
    return lax.dot_general(a, b, (((1,), (1,)), ((), ())), preferred_element_type=F32)


def _ds(i, n):
    return pl.ds(i * n, n) if isinstance(i, int) else pl.ds(pl.multiple_of(i * n, n), n)


def _dot_tn(a, b):
    return lax.dot_general(a, b, (((0,), (0,)), ((), ())), preferred_element_type=F32)


def _rope_kernel(pos_ref, freq_ref, mq_ref, mk1_ref, mk2_ref):
    ang = pos_ref[...].astype(F32) * freq_ref[...]
    c = jnp.cos(ang)
    s = jnp.sin(ang)
    quarter = lax.broadcasted_iota(jnp.int32, ang.shape, 1) >> 5
    even = (quarter & 1) == 0
    mq_ref[...] = jnp.where((quarter == 0) | (quarter == 3), c, s)
    mk1_ref[...] = jnp.where(even, c, s)
    mk2_ref[...] = jnp.where(even, s, c)


def _rope_tables(positions, tm):
    t = positions.size
    half = QK_ROPE // 2
    inv_freq = jnp.power(ROPE_BASE, -jnp.arange(half, dtype=F32) / half)
    freq = jnp.tile(inv_freq, LANES // half)[None, :]
    out = jax.ShapeDtypeStruct((t, LANES), F32)
    row = pl.BlockSpec((tm, LANES), lambda i: (i, 0))
    return pl.pallas_call(
        _rope_kernel, out_shape=(out, out, out), grid=(t // tm,),
        in_specs=[pl.BlockSpec((tm, 1), lambda i: (i, 0)), pl.BlockSpec((1, LANES), lambda i: (0, 0))],
        out_specs=(row, row, row), compiler_params=_params(("parallel",)), name="rope_tables",
    )(positions.reshape(t, 1), freq)


def _ln_kernel(x_ref, g_ref, b_ref, o_ref):
    o_ref[...] = _layer_norm(x_ref[...], g_ref[...], b_ref[...])


def _ln_in(x2, g, b, tm):
    t, d = x2.shape
    row = pl.BlockSpec((tm, d), lambda i: (i, 0))
    vec = pl.BlockSpec((1, d), lambda i: (0, 0))
    return pl.pallas_call(
        _ln_kernel, out_shape=jax.ShapeDtypeStruct((t, d), F32), grid=(t // tm,),
        in_specs=[row, vec, vec], out_specs=row, compiler_params=_params(("parallel",)), name="ln_in",
    )(x2, g.reshape(1, d), b.reshape(1, d))


PROJ_CHUNK = 512


def _proj_kernel(h_ref, w_ref, *out_refs):
    a = h_ref[...].astype(BF16)
    off = 0
    for ref, width in zip(out_refs, PROJ_SEGS):
        for c in range(0, width, PROJ_CHUNK):
            wc = min(PROJ_CHUNK, width - c)
            ref[:, c:c + wc] = _dot(a, w_ref[:, off + c:off + c + wc]).astype(ref.dtype)
        off += width


def _proj(h, w_all, tm):
    t, d = h.shape
    dts = (BF16, F32, F32, F32, F32, BF16)
    outs = tuple(jax.ShapeDtypeStruct((t, w), dt) for w, dt in zip(PROJ_SEGS, dts))
    return pl.pallas_call(
        _proj_kernel, out_shape=outs, grid=(t // tm,),
        in_specs=[pl.BlockSpec((tm, d), lambda i: (i, 0)), pl.BlockSpec((d, PROJ_W), lambda i: (0, 0))],
        out_specs=tuple(pl.BlockSpec((tm, w), lambda i: (i, 0)) for w in PROJ_SEGS),
        compiler_params=_params(("parallel",)), name="in_proj",
    )(h, w_all)


def _pack_w_in(w_in):
    hq = GDN_HEADS * GDN_DK
    o_a = 4 * hq
    o_bt = o_a + 2 * GDN_HEADS
    o_cq = o_bt + 2 * GDN_HEADS
    o_ckv = o_cq + Q_LORA
    o_kr = o_ckv + KV_LORA
    o_g = o_kr + QK_ROPE
    half = QK_ROPE // 2
    k1 = w_in[:, o_kr:o_kr + half]
    k2 = w_in[:, o_kr + half:o_g]
    d = w_in.shape[0]
    ab = jnp.concatenate([w_in[:, o_a:o_cq], jnp.zeros((d, AB_W - 4 * GDN_HEADS), w_in.dtype)], 1)
    return jnp.concatenate(
        [w_in[:, :o_a], w_in[:, o_cq:o_ckv], w_in[:, o_ckv:o_kr],
         k1, k1, k1, k1, -k2, k2, -k2, k2, ab, w_in[:, o_g:]], axis=1).astype(BF16)


GDN_HPS = 2
CONV_PAD = 8
PREP_CHUNKS = 8


def _gdn_kernel(q_ref, k_ref, v_ref, z_ref, ab_ref, cwq_ref, cwk_ref, cwv_ref, alog_ref, dtb_ref, nw_ref,
                o_ref, qs, ks, vs, gs, bs, brs, bns, mqs, ges, os_, xp):
    s_len = q_ref.shape[0]
    nc = s_len // CHUNK
    first_head = pl.program_id(1) * GDN_HPS

    xp[:CONV_PAD, :] = jnp.zeros((CONV_PAD, LANES), F32)
    xp[CONV_PAD + s_len:, :] = jnp.zeros((CONV_PAD, LANES), F32)

    def conv_silu(x_ref, cw_ref, cols):
        xp[CONV_PAD:CONV_PAD + s_len, :] = x_ref[:, cols].astype(F32)
        acc = None
        for j in range(CONV_K):
            term = xp[CONV_PAD + j - CONV_K // 2:CONV_PAD + j - CONV_K // 2 + s_len, :] * cw_ref[j:j + 1, cols]
            acc = term if acc is None else acc + term
        return _silu(acc)

    def l2n(x):
        return x * lax.rsqrt(jnp.sum(x * x, -1, keepdims=True) + RMS_EPS)

    lane = lax.broadcasted_iota(jnp.int32, (1, LANES), 1)

    def pick(vals, idx):
        col = jnp.sum(jnp.where(lane == idx, vals, 0.0), -1, keepdims=True)
        return jnp.broadcast_to(col, (s_len, LANES))

    def stage(hh):
        cols = slice(hh * LANES, (hh + 1) * LANES)
        qs[...] = l2n(conv_silu(q_ref, cwq_ref, cols)) * (GDN_DK ** -0.5)
        ks[...] = l2n(conv_silu(k_ref, cwk_ref, cols))
        vs[...] = conv_silu(v_ref, cwv_ref, cols)
        ab = ab_ref[...]
        x = ab + dtb_ref[...]
        softplus = jnp.maximum(x, 0.0) + jnp.log(1.0 + jnp.exp(-jnp.abs(x)))
        g_all = -jnp.exp(alog_ref[...]) * softplus
        b_all = _sigmoid(ab)
        for d in range(2):
            gs[d] = pick(g_all, first_head + hh + d * GDN_HEADS)
            bs[d] = pick(b_all, first_head + hh + (2 + d) * GDN_HEADS)

    ci = lax.broadcasted_iota(jnp.int32, (CHUNK, 2 * CHUNK), 0)
    cj2 = lax.broadcasted_iota(jnp.int32, (CHUNK, 2 * CHUNK), 1)
    right = cj2 >= CHUNK
    cj = cj2 & (CHUNK - 1)
    eye = jnp.where(ci == cj, 1.0, 0.0)
    incl = (ci >= cj, ci <= cj)
    strict = (ci > cj, ci < cj)
    tri3 = tuple(jnp.concatenate([jnp.where(m[:, :CHUNK], 1.0, 0.0).astype(BF16)] * 3, axis=1) for m in incl)
    ones3 = jnp.ones((CHUNK, 3 * CHUNK), BF16)
    last_row = (CHUNK - 1, 0)

    unroll = PREP_CHUNKS if nc % PREP_CHUNKS == 0 else 1

    def cumulate(c, carry):
        rows = [pl.ds(pl.multiple_of((c * unroll + j) * CHUNK, CHUNK), CHUNK) for j in range(unroll)]
        chains = [(j, d) for j in range(unroll) for d in range(2)]
        def split3(x):
            x1 = x.astype(BF16)
            r1 = x - x1.astype(F32)
            x2 = r1.astype(BF16)
            return jnp.concatenate([x1, x2, (r1 - x2.astype(F32)).astype(BF16)], axis=0)

        parts = [split3(gs[d, rows[j], :]) for j, d in chains]
        b = [_dot(tri3[d], p) for (j, d), p in zip(chains, parts)]
        diag = [split3(x * eye) for x in b]
        b_row = [_dot(ones3, x) for x in diag]
        for (j, d), x, y in zip(chains, b, b_row):
            gs[d, rows[j], :] = x
            brs[d, rows[j], :] = y
        return carry

    def prep(hh, it, carry):
        unroll = PREP_CHUNKS if nc % PREP_CHUNKS == 0 else 1
        cs = [it * unroll + j for j in range(unroll)]
        rows = [pl.ds(pl.multiple_of(c * CHUNK, CHUNK), CHUNK) for c in cs]
        q = [qs[r, :] for r in rows]
        k = [ks[r, :] for r in rows]
        kb = [x.astype(BF16) for x in k]
        kk = [_dot_nt(x, jnp.concatenate([x, x], axis=0)) for x in kb]
        qk = [_dot_nt(x.astype(BF16), y) for x, y in zip(q, kb)]
        chains = [(j, d) for j in range(unroll) for d in range(2)]
        b = [gs[d, rows[j], :] for j, d in chains]
        beta = [bs[d, rows[j], :] for j, d in chains]
        gamma = [jnp.exp(jnp.where(incl[d], b[n] - brs[d, rows[j], :], -jnp.inf))
                 for n, (j, d) in enumerate(chains)]
        pt = [jnp.where(right, eye, jnp.where(strict[d], -beta[n] * kk[j] * gamma[n], 0.0))
              for n, (j, d) in enumerate(chains)]
        for _ in range(int(np.log2(CHUNK))):
            ptb = [x.astype(BF16) for x in pt]
            prod = [_dot(x[:, :CHUNK], x) for x in ptb]
            pt = [y + jnp.where(right, x, 0.0) for x, y in zip(pt, prod)]
        eb = [jnp.exp(x) for x in b]
        rhs = [jnp.concatenate([jnp.zeros((CHUNK, GDN_DV + GDN_DK), F32), jnp.concatenate(
                   [beta[n] * vs[rows[j], :], beta[n] * eb[n] * k[j]], axis=1)], axis=0).astype(BF16)
               for n, (j, d) in enumerate(chains)]
        wk = [_dot(t.astype(BF16), r).astype(BF16) for t, r in zip(pt, rhs)]
        b_last = [b[n][last_row[d]:last_row[d] + 1, :] for n, (j, d) in enumerate(chains)]
        kdt = [jnp.transpose(k[j] * jnp.exp(b_last[n] - b[n])).astype(BF16)
               for n, (j, d) in enumerate(chains)]
        bm = [_dot(x, y) for x, y in zip(kdt, wk)]
        pw = [_dot((qk[j] * gamma[n][:, :CHUNK]).astype(BF16), wk[n])
              for n, (j, d) in enumerate(chains)]
        for n, (j, d) in enumerate(chains):
            c = cs[j]
            bns[hh, d, pl.ds(pl.multiple_of(c * GDN_DK, GDN_DK), GDN_DK), :] = bm[n][:, :GDN_DV]
            r3 = pl.multiple_of(c * (GDN_DK + CHUNK), CHUNK)
            mqs[hh, d, pl.ds(r3, GDN_DK), :] = bm[n][:, GDN_DV:].astype(BF16)
            mqs[hh, d, pl.ds(r3 + GDN_DK, CHUNK), :] = (q[j] * eb[n] - pw[n][:, GDN_DV:]).astype(BF16)
            ges[hh, d, pl.ds(pl.multiple_of(c * 8, 8), 8), :] = jnp.broadcast_to(jnp.exp(b_last[n]), (8, LANES))
        for j in range(unroll):
            os_[hh, rows[j], :] = pw[2 * j][:, :GDN_DV] + pw[2 * j + 1][:, :GDN_DV]
        return carry

    for hh in range(GDN_HPS):
        stage(hh)
        lax.fori_loop(0, nc // unroll, cumulate, 0)
        lax.fori_loop(0, nc // (PREP_CHUNKS if nc % PREP_CHUNKS == 0 else 1), functools.partial(prep, hh), 0)

    chains = [(hh, d) for hh in range(GDN_HPS) for d in range(2)]

    def step(i, states):
        cs = (i, nc - 1 - i)
        sb = [x.astype(BF16) for x in states]
        r = [_dot(mqs[hh, d, pl.ds(pl.multiple_of(cs[d] * (GDN_DK + CHUNK), CHUNK), GDN_DK + CHUNK), :], sb[n])
             for n, (hh, d) in enumerate(chains)]
        new = []
        for n, (hh, d) in enumerate(chains):
            c = cs[d]
            os_[hh, pl.ds(pl.multiple_of(c * CHUNK, CHUNK), CHUNK), :] += r[n][GDN_DK:]
            ge = ges[hh, d, pl.ds(pl.multiple_of(c * 8, 8), 1), :]
            bn = bns[hh, d, pl.ds(pl.multiple_of(c * GDN_DK, GDN_DK), GDN_DK), :]
            new.append(ge * states[n] - r[n][:GDN_DK] + bn)
        return tuple(new)

    s0 = jnp.zeros((GDN_DK, GDN_DV), F32)
    lax.fori_loop(0, nc, step, (s0,) * len(chains))

    for hh in range(GDN_HPS):
        cols = slice(hh * LANES, (hh + 1) * LANES)
        o_ref[:, cols] = (_rms_norm(os_[hh], nw_ref[...]) * _silu(z_ref[:, cols].astype(F32))).astype(o_ref.dtype)


def _gdn(qkvz, ab, conv_w, a_log, dt_bias, norm_w, bsz, s_len):
    nh = GDN_HEADS
    qkvz3 = qkvz.reshape(bsz, s_len, QKVZ_W)
    ab3 = ab.reshape(bsz, s_len, AB_W)
    pad = jnp.zeros((AB_W - 2 * nh,), F32)
    alog = jnp.concatenate([a_log.reshape(-1), pad])[None, :]
    dtb = jnp.concatenate([dt_bias.reshape(-1), pad])[None, :]

    hps = GDN_HPS
    ng = nh // hps

    def col(k):
        return pl.BlockSpec((None, s_len, hps * LANES), lambda b, h: (b, 0, k * ng + h))

    def cw(k):
        return pl.BlockSpec((CONV_K, hps * LANES), lambda b, h: (0, k * ng + h))

    vec = pl.BlockSpec((1, LANES), lambda b, h: (0, 0))
    nc = s_len // CHUNK
    scratch = [
        pltpu.VMEM((s_len, LANES), F32), pltpu.VMEM((s_len, LANES), F32), pltpu.VMEM((s_len, LANES), F32),
        pltpu.VMEM((2, s_len, LANES), F32), pltpu.VMEM((2, s_len, LANES), F32),
        pltpu.VMEM((2, s_len, 2 * CHUNK), F32), pltpu.VMEM((hps, 2, nc * GDN_DK, GDN_DV), F32),
        pltpu.VMEM((hps, 2, nc * (GDN_DK + CHUNK), GDN_DV), BF16),
        pltpu.VMEM((hps, 2, nc * 8, LANES), F32), pltpu.VMEM((hps, s_len, GDN_DV), F32),
        pltpu.VMEM((s_len + 2 * CONV_PAD, LANES), F32),
    ]
    out = pl.pallas_call(
        _gdn_kernel, out_shape=jax.ShapeDtypeStruct((bsz, s_len, nh * GDN_DV), BF16), grid=(bsz, ng),
        in_specs=[col(0), col(1), col(2), col(3),
                  pl.BlockSpec((None, s_len, AB_W), lambda b, h: (b, 0, 0)),
                  cw(0), cw(1), cw(2), vec, vec, vec],
        out_specs=pl.BlockSpec((None, s_len, hps * GDN_DV), lambda b, h: (b, 0, h)),
        scratch_shapes=scratch, compiler_params=_params(("parallel", "parallel")), name="gdn",
    )(qkvz3, qkvz3, qkvz3, qkvz3, ab3, conv_w, conv_w, conv_w, alog, dtb, norm_w.reshape(1, GDN_DV))
    return out.reshape(bsz * s_len, nh * GDN_DV)


MLA_TQ = 256
MLA_KC = 1024


def _mla_kernel(cq_ref, ckv_ref, kab_ref, mq_ref, mk1_ref, mk2_ref, qnw_ref, kvnw_ref, wq_ref, wkv_ref,
                o_ref, q_s, k_s, vt_s, sta_s, stb_s, cqn_s, ckvn_s):
    s_len = cq_ref.shape[0]
    scale = (QK_NOPE + QK_ROPE) ** -0.5

    @pl.when(pl.program_id(1) == 0)
    def _():
        cqn_s[...] = _rms_norm(cq_ref[...], qnw_ref[...]).astype(BF16)
        ckvn_s[...] = _rms_norm(ckv_ref[...], kvnw_ref[...]).astype(BF16)
        kab = kab_ref[...]
        k_s[:, QK_NOPE:] = (kab[:, :LANES] * mk1_ref[...] + kab[:, LANES:] * mk2_ref[...]).astype(BF16)
        vt_s[V_HEAD:, :] = jnp.ones((vt_s.shape[0] - V_HEAD, s_len), BF16)

    q = _dot(cqn_s[...], wq_ref[...]) * scale
    q_s[:, :QK_NOPE] = q[:, :QK_NOPE].astype(BF16)
    q_s[:, QK_NOPE:] = (q[:, QK_NOPE:] * mq_ref[...]).astype(BF16)
    kv = _dot(ckvn_s[...], wkv_ref[...])
    k_s[:, :QK_NOPE] = kv[:, :QK_NOPE].astype(BF16)
    vt_s[:V_HEAD, :] = jnp.transpose(kv[:, QK_NOPE:]).astype(BF16)

    tq = min(MLA_TQ, s_len)

    nq = s_len // tq
    kc = min(MLA_KC, s_len)

    def scores(i, st):
        st[...] = _dot_nt(k_s[...], q_s[_ds(i, tq), :])

    def attend(i, st, nxt):
        m = jnp.max(st[...], 0, keepdims=True)
        ot = jnp.zeros((vt_s.shape[0], tq), F32)
        for c in range(0, s_len, kc):
            if nxt is not None:
                nxt[c:c + kc, :] = _dot_nt(k_s[c:c + kc, :], q_s[_ds(i + 1, tq), :])
            p = jnp.exp(st[c:c + kc, :] - m).astype(BF16)
            ot = ot + _dot(vt_s[:, c:c + kc], p)
        o = jnp.transpose(ot[:V_HEAD] / ot[V_HEAD:V_HEAD + 1])
        o_ref[_ds(i, tq), :] = o.astype(o_ref.dtype)

    def pair(j, prefetch):
        attend(2 * j, sta_s, stb_s)
        attend(2 * j + 1, stb_s, sta_s if prefetch else None)

    assert nq == 1 or nq % 2 == 0
    scores(0, sta_s)
    if nq == 1:
        attend(0, sta_s, None)
    else:
        def body(j, carry):
            pair(j, True)
            return carry
        lax.fori_loop(0, nq // 2 - 1, body, 0)
        pair(nq // 2 - 1, False)


def _pack_w_uq(w_uq):
    w = w_uq.reshape(Q_LORA, MLA_HEADS, QK_NOPE + QK_ROPE)
    half = QK_ROPE // 2
    r1 = w[..., QK_NOPE:QK_NOPE + half]
    r2 = w[..., QK_NOPE + half:]
    w = jnp.concatenate([w[..., :QK_NOPE], r1, r1, -r2, r2], axis=-1)
    return w.reshape(Q_LORA, MLA_HEADS * 2 * LANES).astype(BF16)


def _mla(cq, ckv, kab, tables, qnw, kvnw, wq, wkv, bsz, s_len):
    nh = MLA_HEADS

    def per_b(width):
        return pl.BlockSpec((None, s_len, width), lambda b, h: (b, 0, 0))

    def vec(width):
        return pl.BlockSpec((1, width), lambda b, h: (0, 0))

    mq, mk1, mk2 = (t.reshape(bsz, s_len, LANES) for t in tables)
    out = pl.pallas_call(
        _mla_kernel, out_shape=jax.ShapeDtypeStruct((bsz, s_len, nh * V_HEAD), BF16), grid=(bsz, nh),
        in_specs=[per_b(Q_LORA), per_b(KV_LORA), per_b(KAB_W), per_b(LANES), per_b(LANES), per_b(LANES),
                  vec(Q_LORA), vec(KV_LORA),
                  pl.BlockSpec((Q_LORA, 2 * LANES), lambda b, h: (0, h)),
                  pl.BlockSpec((KV_LORA, QK_NOPE + V_HEAD), lambda b, h: (0, h))],
        out_specs=pl.BlockSpec((None, s_len, V_HEAD), lambda b, h: (b, 0, h)),
        scratch_shapes=[pltpu.VMEM((s_len, 2 * LANES), BF16), pltpu.VMEM((s_len, 2 * LANES), BF16),
                        pltpu.VMEM((V_HEAD + 16, s_len), BF16),
                        pltpu.VMEM((s_len, min(MLA_TQ, s_len)), F32),
                        pltpu.VMEM((s_len, min(MLA_TQ, s_len)), F32),
                        pltpu.VMEM((s_len, Q_LORA), BF16), pltpu.VMEM((s_len, KV_LORA), BF16)],
        compiler_params=_params(("parallel", "arbitrary")), name="mla",
    )(cq.reshape(bsz, s_len, Q_LORA), ckv.reshape(bsz, s_len, KV_LORA), kab.reshape(bsz, s_len, KAB_W),
      mq, mk1, mk2, qnw.reshape(1, Q_LORA), kvnw.reshape(1, KV_LORA), wq, wkv)
    return out.reshape(bsz * s_len, nh * V_HEAD)


def _tail_kernel(og_ref, om_ref, gates_ref, h_ref, wog_ref, wom_ref, wout_ref, g_ref, b_ref, o_ref, ob_ref):
    d = h_ref.shape[1]
    y_gdn = _dot(og_ref[...], wog_ref[...])
    y_mla = _dot(om_ref[...], wom_ref[...])
    y = (_sigmoid(gates_ref[:, :d].astype(F32)) * y_gdn + _sigmoid(gates_ref[:, d:].astype(F32)) * y_mla)
    m = _dot(y.astype(BF16), wout_ref[...])
    out = _layer_norm(DN_ALPHA * h_ref[...] + m, g_ref[...], b_ref[...])
    o_ref[...] = out
    ob_ref[...] = out.astype(BF16)


def _mixer_tail(og, om, gates, h, wog, wom, wout, g, b, tm):
    t, d = h.shape
    row = lambda w: pl.BlockSpec((tm, w), lambda i: (i, 0))
    full = pl.BlockSpec((d, d), lambda i: (0, 0))
    vec = pl.BlockSpec((1, d), lambda i: (0, 0))
    return pl.pallas_call(
        _tail_kernel, out_shape=(jax.ShapeDtypeStruct((t, d), F32), jax.ShapeDtypeStruct((t, d), BF16)),
        grid=(t // tm,),
        in_specs=[row(d), row(d), row(2 * d), row(d), full, full, full, vec, vec],
        out_specs=(row(d), row(d)), compiler_params=_params(("parallel",)), name="mixer_tail",
    )(og, om, gates, h, wog, wom, wout, g.reshape(1, d), b.reshape(1, d))


def _router_kernel(h_ref, wh_ref, wl_ref, b_ref, eid_ref, gate_ref, cnt_ref, run):
    @pl.when(pl.program_id(0) == 0)
    def _():
        run[...] = jnp.zeros_like(run)

    x = h_ref[...]
    xh = x.astype(BF16)
    xl = (x - xh.astype(F32)).astype(BF16)
    logits = _dot(xh, wh_ref[...]) + (_dot(xl, wh_ref[...]) + _dot(xh, wl_ref[...])) + b_ref[...]
    tm = logits.shape[0]
    lane = lax.broadcasted_iota(jnp.int32, logits.shape, 1)
    lane_f = lane.astype(F32)
    neg = -jnp.inf

    def first_max(vals):
        m = jnp.max(vals, -1, keepdims=True)
        idx = jnp.min(jnp.where(vals == m, lane_f, float(LANES)), -1, keepdims=True)
        return m, idx

    is_grp = lane < N_GROUPS
    mg, grp = first_max(jnp.where(is_grp, logits, neg))
    p_grp = 1.0 / jnp.sum(jnp.where(is_grp, jnp.exp(logits - mg), 0.0), -1, keepdims=True)
    in_grp = ((lane - N_GROUPS) >> 3).astype(F32) == grp
    le = jnp.where(in_grp, logits, neg)
    l1, i1 = first_max(le)
    l2, i2 = first_max(jnp.where(lane_f == i1, neg, le))
    t = jnp.exp(l2 - l1)
    g1 = p_grp / (1.0 + t)
    g2 = p_grp * t / (1.0 + t)
    hit1 = lane_f == i1
    hit2 = lane_f == i2
    onehot = jnp.where(hit1 | hit2, 1.0, 0.0)
    ri = lax.broadcasted_iota(jnp.int32, (tm, tm), 0)
    rj = lax.broadcasted_iota(jnp.int32, (tm, tm), 1)
    before = jnp.where(ri > rj, 1.0, 0.0).astype(BF16)
    prefix = _dot(before, onehot.astype(BF16)) + run[...]
    rank1 = jnp.sum(jnp.where(hit1, prefix, 0.0), -1, keepdims=True)
    rank2 = jnp.sum(jnp.where(hit2, prefix, 0.0), -1, keepdims=True)
    run[...] += jnp.sum(onehot, 0, keepdims=True)
    cnt_ref[...] = run[...]
    ids = jnp.where(lane == 0, i1 - float(N_GROUPS), jnp.where(lane == 1, i2 - float(N_GROUPS),
                    jnp.where(lane == 2, rank1, jnp.where(lane == 3, rank2, 0.0))))
    eid_ref[...] = ids.astype(jnp.int32)
    gate_ref[...] = jnp.where(lane == 0, g1, jnp.where(lane == 1, g2, 0.0))


def _router(h, w_rg, b_rg, w_re, b_re, tm):
    t, d = h.shape
    padw = LANES - N_GROUPS - N_EXPERTS
    w = jnp.concatenate([w_rg, w_re, jnp.zeros((d, padw), F32)], axis=1)
    b = jnp.concatenate([b_rg, b_re, jnp.zeros((padw,), F32)])[None, :]
    wh = w.astype(BF16)
    wl = (w - wh.astype(F32)).astype(BF16)
    row = pl.BlockSpec((tm, LANES), lambda i: (i, 0))
    one = pl.BlockSpec((1, LANES), lambda i: (0, 0))
    wspec = pl.BlockSpec((d, LANES), lambda i: (0, 0))
    return pl.pallas_call(
        _router_kernel,
        out_shape=(jax.ShapeDtypeStruct((t, LANES), jnp.int32), jax.ShapeDtypeStruct((t, LANES), F32),
                   jax.ShapeDtypeStruct((1, LANES), F32)),
        grid=(t // tm,),
        in_specs=[pl.BlockSpec((tm, d), lambda i: (i, 0)), wspec, wspec, one],
        out_specs=(row, row, one), scratch_shapes=[pltpu.VMEM((1, LANES), F32)],
        compiler_params=_params(("arbitrary",)), name="router",
    )(h, wh, wl, b)


def _dest_kernel(ids_ref, start_ref, o_ref):
    ids = ids_ref[...]
    lane = lax.broadcasted_iota(jnp.int32, ids.shape, 1)
    start = start_ref[...]

    def row_of(k):
        hit = lane == ids[:, k:k + 1] + N_GROUPS
        return jnp.sum(jnp.where(hit, start, 0.0), -1, keepdims=True).astype(jnp.int32) + ids[:, TOP_K + k:TOP_K + k + 1]

    o_ref[...] = jnp.where(lane == 0, row_of(0), jnp.where(lane == 1, row_of(1), 0))


def _dest_rows(ids, pad_start, tm):
    t = ids.shape[0]
    start = jnp.zeros((1, LANES), F32).at[0, N_GROUPS:N_GROUPS + N_EXPERTS].set(pad_start.astype(F32))
    row = pl.BlockSpec((tm, LANES), lambda i: (i, 0))
    return pl.pallas_call(
        _dest_kernel, out_shape=jax.ShapeDtypeStruct((t, LANES), jnp.int32), grid=(t // tm,),
        in_specs=[row, pl.BlockSpec((1, LANES), lambda i: (0, 0))], out_specs=row,
        compiler_params=_params(("parallel",)), name="dest_rows",
    )(ids, start)


def _expert_kernel(be_ref, nu_ref, x_ref, wg_ref, wu_ref, wd_ref, o_ref, wgb, wub, wdb):
    i = pl.program_id(0)

    @pl.when((i == 0) | (be_ref[i] != be_ref[jnp.maximum(i - 1, 0)]))
    def _():
        wgb[...] = wg_ref[...].astype(BF16)
        wub[...] = wu_ref[...].astype(BF16)
        wdb[...] = wd_ref[...].astype(BF16)

    @pl.when(i < nu_ref[0])
    def _():
        x = x_ref[...]
        hid = _silu(_dot(x, wgb[...])) * _dot(x, wub[...])
        o_ref[...] = _dot(hid.astype(BF16), wdb[...]).astype(o_ref.dtype)

    @pl.when(i >= nu_ref[0])
    def _():
        o_ref[...] = jnp.zeros_like(o_ref)


def _experts(xb, blk_expert, n_used, w_gate, w_up, w_down, layer):
    n_rows, d = xb.shape
    n_blocks = n_rows // MOE_BLOCK
    wspec = lambda shape: pl.BlockSpec((None, None) + shape, lambda i, be, nu: (layer, be[i], 0, 0))
    grid_spec = pltpu.PrefetchScalarGridSpec(
        num_scalar_prefetch=2, grid=(n_blocks,),
        in_specs=[pl.BlockSpec((MOE_BLOCK, d), lambda i, be, nu: (i, 0)),
                  wspec((d, D_EXPERT)), wspec((d, D_EXPERT)), wspec((D_EXPERT, d))],
        out_specs=pl.BlockSpec((MOE_BLOCK, d), lambda i, be, nu: (i, 0)),
        scratch_shapes=[pltpu.VMEM((d, D_EXPERT), BF16), pltpu.VMEM((d, D_EXPERT), BF16),
                        pltpu.VMEM((D_EXPERT, d), BF16)])
    return pl.pallas_call(
        _expert_kernel, out_shape=jax.ShapeDtypeStruct((n_rows, d), BF16), grid_spec=grid_spec,
        compiler_params=_params(("arbitrary",)), name="experts",
    )(blk_expert, n_used, xb, w_gate, w_up, w_down)


def _combine_kernel(h_ref, y1_ref, y2_ref, gate_ref, g_ref, b_ref, o_ref):
    f = gate_ref[:, 0:1] * y1_ref[...].astype(F32) + gate_ref[:, 1:2] * y2_ref[...].astype(F32)
    o_ref[...] = _layer_norm(DN_ALPHA * h_ref[...] + f, g_ref[...], b_ref[...])


def _combine(h, y1, y2, gate, g, b, tm):
    t, d = h.shape
    row = pl.BlockSpec((tm, d), lambda i: (i, 0))
    vec = pl.BlockSpec((1, d), lambda i: (0, 0))
    return pl.pallas_call(
        _combine_kernel, out_shape=jax.ShapeDtypeStruct((t, d), F32), grid=(t // tm,),
        in_specs=[row, row, row, pl.BlockSpec((tm, LANES), lambda i: (i, 0)), vec, vec], out_specs=row,
        compiler_params=_params(("parallel",)), name="combine_ln",
    )(h, y1, y2, gate, g.reshape(1, d), b.reshape(1, d))


def _dispatch(ids, cnt, tm):
    t = ids.shape[0]
    n_assign = t * TOP_K
    counts = cnt[0, N_GROUPS:N_GROUPS + N_EXPERTS].astype(jnp.int32)
    padded = (counts + MOE_BLOCK - 1) // MOE_BLOCK * MOE_BLOCK
    pad_end = jnp.cumsum(padded)
    pad_start = pad_end - padded
    start = jnp.cumsum(counts) - counts
    eid = ids[:, :TOP_K]
    dest = _dest_rows(ids, pad_start, tm)[:, :TOP_K]
    order = jnp.argsort(eid.reshape(n_assign))
    n_blocks = -(-(n_assign + N_EXPERTS * (MOE_BLOCK - 1)) // MOE_BLOCK)
    blk_first = (jnp.arange(n_blocks) * MOE_BLOCK)[:, None]
    blk_expert = jnp.minimum(jnp.sum(pad_end[None, :] <= blk_first, axis=1), N_EXPERTS - 1).astype(jnp.int32)
    shift = jnp.repeat(jnp.take(pad_start - start, blk_expert), MOE_BLOCK)
    end = jnp.repeat(jnp.take(start + counts, blk_expert), MOE_BLOCK)
    rows = jnp.arange(n_blocks * MOE_BLOCK)
    src = rows - shift
    row_tok = jnp.where(src < end, jnp.take(order, jnp.clip(src, 0, n_assign - 1)) // TOP_K, rows % t).astype(jnp.int32)
    n_used = (pad_end[-1:] // MOE_BLOCK).astype(jnp.int32)
    return row_tok, dest, blk_expert, n_used


def _moe(h, hb, w_rg, b_rg, w_re, b_re, w_gate, w_up, w_down, layer, g, b, tm):
    ids, gate, cnt = _router(h, w_rg, b_rg, w_re, b_re, tm)
    row_tok, dest, blk_expert, n_used = _dispatch(ids, cnt, tm)
    xb = jnp.take(hb, row_tok, axis=0)
    yb = _experts(xb, blk_expert, n_used, w_gate, w_up, w_down, layer)
    y1 = jnp.take(yb, dest[:, 0], axis=0)
    y2 = jnp.take(yb, dest[:, 1], axis=0)
    return _combine(h, y1, y2, gate, g, b, min(ROWS_MEMBOUND, h.shape[0]))


def kernel(x, positions, ln_in_g, ln_in_b, w_in, conv_w, a_log, dt_bias, gdn_norm_w, w_o_gdn, mla_q_norm_w, w_uq, mla_kv_norm_w, w_ukv, w_o_mla, w_out, ln1_g, ln1_b, w_router_group, b_router_group, w_router_expert, b_router_expert, w_gate, w_up, w_down, ln2_g, ln2_b):
    bsz, s_len, d = x.shape
    t = bsz * s_len
    tm = min(ROWS_MATMUL, t)
    tables = _rope_tables(positions, min(ROWS_MEMBOUND, t))
    h = _ln_in(x.reshape(t, d), ln_in_g, ln_in_b, min(ROWS_MEMBOUND, t))
    for l in range(DEPTH):
        qkvz, cq, ckv, kab, ab, gates = _proj(h, _pack_w_in(w_in[l]), tm)
        og = _gdn(qkvz, ab, conv_w[l], a_log[l], dt_bias[l], gdn_norm_w[l], bsz, s_len)
        om = _mla(cq, ckv, kab, tables, mla_q_norm_w[l], mla_kv_norm_w[l],
                  _pack_w_uq(w_uq[l]), w_ukv[l].astype(BF16), bsz, s_len)
        h, hb = _mixer_tail(og, om, gates, h, w_o_gdn[l].astype(BF16), w_o_mla[l].astype(BF16),
                            w_out[l].astype(BF16), ln1_g[l], ln1_b[l], tm)
        h = _moe(h, hb, w_router_group[l], b_router_group[l], w_router_expert[l], b_router_expert[l],
                 w_gate, w_up, w_down, l, ln2_g[l], ln2_b[l], tm)
    return h.reshape(bsz, s_len, d)
```

```python
import functools

import numpy as np
import jax
import jax.numpy as jnp
from jax import lax
from jax.experimental import pallas as pl
from jax.experimental.pallas import tpu as pltpu

F32 = jnp.float32
BF16 = jnp.bfloat16

D_MODEL = 1024
DEPTH = 2
GDN_HEADS = 8
GDN_DK = 128
GDN_DV = 128
CONV_K = 5
CHUNK = 64
MLA_HEADS = 8
Q_LORA = 384
KV_LORA = 256
QK_NOPE = 128
QK_ROPE = 64
V_HEAD = 128
ROPE_BASE = 10000.0
N_GROUPS = 8
EXPERTS_PER_GROUP = 8
N_EXPERTS = N_GROUPS * EXPERTS_PER_GROUP
TOP_K = 2
D_EXPERT = 512
MOE_BLOCK = 256
DN_ALPHA = (2 * DEPTH) ** 0.25
LN_EPS = 1e-5
RMS_EPS = 1e-6

ROWS_MATMUL = 256
ROWS_MEMBOUND = 512
LANES = 128
VMEM_LIMIT = 56 * 1024 * 1024

QKVZ_W = 4 * GDN_HEADS * GDN_DK
KAB_W = 2 * LANES
AB_W = LANES
GATES_W = 2 * D_MODEL
PROJ_SEGS = (QKVZ_W, Q_LORA, KV_LORA, KAB_W, AB_W, GATES_W)
PROJ_W = sum(PROJ_SEGS)


def _params(sem):
    return pltpu.CompilerParams(dimension_semantics=sem, vmem_limit_bytes=VMEM_LIMIT)


def _layer_norm(x, g, b):
    mu = jnp.mean(x, -1, keepdims=True)
    xc = x - mu
    var = jnp.mean(xc * xc, -1, keepdims=True)
    return xc * lax.rsqrt(var + LN_EPS) * g + b


def _rms_norm(x, w):
    return x * lax.rsqrt(jnp.mean(x * x, -1, keepdims=True) + RMS_EPS) * w


def _silu(x):
    return (0.5 * x) * (1.0 + jnp.tanh(0.5 * x))


def _sigmoid(x):
    return 1.0 / (1.0 + jnp.exp(-x))


def _dot(a, b):
    return jnp.dot(a, b, preferred_element_type=F32)


def _dot_nt(a, b):
    return lax.dot_general(a, b, (((1,), (1,)), ((), ())), preferred_element_type=F32)


def _ds(i, n):
    return pl.ds(i * n, n) if isinstance(i, int) else pl.ds(pl.multiple_of(i * n, n), n)


def _dot_tn(a, b):
    return lax.dot_general(a, b, (((0,), (0,)), ((), ())), preferred_element_type=F32)


def _rope_kernel(pos_ref, freq_ref, mq_ref, mk1_ref, mk2_ref):
    ang = pos_ref[...].astype(F32) * freq_ref[...]
    c = jnp.cos(ang)
    s = jnp.sin(ang)
    quarter = lax.broadcasted_iota(jnp.int32, ang.shape, 1) >> 5
    even = (quarter & 1) == 0
    mq_ref[...] = jnp.where((quarter == 0) | (quarter == 3), c, s)
    mk1_ref[...] = jnp.where(even, c, s)
    mk2_ref[...] = jnp.where(even, s, c)


def _rope_tables(positions, tm):
    t = positions.size
    half = QK_ROPE // 2
    inv_freq = jnp.power(ROPE_BASE, -jnp.arange(half, dtype=F32) / half)
    freq = jnp.tile(inv_freq, LANES // half)[None, :]
    out = jax.ShapeDtypeStruct((t, LANES), F32)
    row = pl.BlockSpec((tm, LANES), lambda i: (i, 0))
    return pl.pallas_call(
        _rope_kernel, out_shape=(out, out, out), grid=(t // tm,),
        in_specs=[pl.BlockSpec((tm, 1), lambda i: (i, 0)), pl.BlockSpec((1, LANES), lambda i: (0, 0))],
        out_specs=(row, row, row), compiler_params=_params(("parallel",)), name="rope_tables",
    )(positions.reshape(t, 1), freq)


def _ln_kernel(x_ref, g_ref, b_ref, o_ref):
    o_ref[...] = _layer_norm(x_ref[...], g_ref[...], b_ref[...])


def _ln_in(x2, g, b, tm):
    t, d = x2.shape
    row = pl.BlockSpec((tm, d), lambda i: (i, 0))
    vec = pl.BlockSpec((1, d), lambda i: (0, 0))
    return pl.pallas_call(
        _ln_kernel, out_shape=jax.ShapeDtypeStruct((t, d), F32), grid=(t // tm,),
        in_specs=[row, vec, vec], out_specs=row, compiler_params=_params(("parallel",)), name="ln_in",
    )(x2, g.reshape(1, d), b.reshape(1, d))


PROJ_CHUNK = 512


def _proj_kernel(h_ref, w_ref, *out_refs):
    a = h_ref[...].astype(BF16)
    off = 0
    for ref, width in zip(out_refs, PROJ_SEGS):
        for c in range(0, width, PROJ_CHUNK):
            wc = min(PROJ_CHUNK, width - c)
            ref[:, c:c + wc] = _dot(a, w_ref[:, off + c:off + c + wc]).astype(ref.dtype)
        off += width


def _proj(h, w_all, tm):
    t, d = h.shape
    dts = (BF16, F32, F32, F32, F32, BF16)
    outs = tuple(jax.ShapeDtypeStruct((t, w), dt) for w, dt in zip(PROJ_SEGS, dts))
    return pl.pallas_call(
        _proj_kernel, out_shape=outs, grid=(t // tm,),
        in_specs=[pl.BlockSpec((tm, d), lambda i: (i, 0)), pl.BlockSpec((d, PROJ_W), lambda i: (0, 0))],
        out_specs=tuple(pl.BlockSpec((tm, w), lambda i: (i, 0)) for w in PROJ_SEGS),
        compiler_params=_params(("parallel",)), name="in_proj",
    )(h, w_all)


def _pack_w_in(w_in):
    hq = GDN_HEADS * GDN_DK
    o_a = 4 * hq
    o_bt = o_a + 2 * GDN_HEADS
    o_cq = o_bt + 2 * GDN_HEADS
    o_ckv = o_cq + Q_LORA
    o_kr = o_ckv + KV_LORA
    o_g = o_kr + QK_ROPE
    half = QK_ROPE // 2
    k1 = w_in[:, o_kr:o_kr + half]
    k2 = w_in[:, o_kr + half:o_g]
    d = w_in.shape[0]
    ab = jnp.concatenate([w_in[:, o_a:o_cq], jnp.zeros((d, AB_W - 4 * GDN_HEADS), w_in.dtype)], 1)
    return jnp.concatenate(
        [w_in[:, :o_a], w_in[:, o_cq:o_ckv], w_in[:, o_ckv:o_kr],
         k1, k1, k1, k1, -k2, k2, -k2, k2, ab, w_in[:, o_g:]], axis=1).astype(BF16)


GDN_HPS = 2
CONV_PAD = 8
CUM_CHUNKS = 8
PREP_CHUNKS = 16


def _gdn_kernel(q_ref, k_ref, v_ref, z_ref, ab_ref, cwq_ref, cwk_ref, cwv_ref, alog_ref, dtb_ref, nw_ref,
                o_ref, qs, ks, vs, gs, bs, brs, bns, mqs, ges, os_, xp, gates_s):
    s_len = q_ref.shape[0]
    nc = s_len // CHUNK
    first_head = pl.program_id(1) * GDN_HPS

    xp[:CONV_PAD, :] = jnp.zeros((CONV_PAD, LANES), F32)
    xp[CONV_PAD + s_len:, :] = jnp.zeros((CONV_PAD, LANES), F32)

    def conv_silu(x_ref, cw_ref, cols):
        xp[CONV_PAD:CONV_PAD + s_len, :] = x_ref[:, cols].astype(F32)
        acc = None
        for j in range(CONV_K):
            term = xp[CONV_PAD + j - CONV_K // 2:CONV_PAD + j - CONV_K // 2 + s_len, :] * cw_ref[j:j + 1, cols]
            acc = term if acc is None else acc + term
        return _silu(acc)

    def l2n(x, scale=1.0):
        return x * (lax.rsqrt(jnp.sum(x * x, -1, keepdims=True) + RMS_EPS) * scale)

    lane = lax.broadcasted_iota(jnp.int32, (1, LANES), 1)

    def pick(vals, idx):
        col = jnp.sum(jnp.where(lane == idx, vals, 0.0), -1, keepdims=True)
        return jnp.broadcast_to(col, (s_len, LANES))

    def stage(hh):
        cols = slice(hh * LANES, (hh + 1) * LANES)
        qs[...] = l2n(conv_silu(q_ref, cwq_ref, cols), GDN_DK ** -0.5)
        ks[...] = l2n(conv_silu(k_ref, cwk_ref, cols))
        vs[...] = conv_silu(v_ref, cwv_ref, cols)
        gates = gates_s[...]
        for d in range(2):
            gs[d] = pick(gates, first_head + hh + d * GDN_HEADS)
            bs[d] = pick(gates, first_head + hh + (2 + d) * GDN_HEADS)

    ab = ab_ref[...]
    x = ab + dtb_ref[...]
    softplus = jnp.maximum(x, 0.0) + jnp.log(1.0 + jnp.exp(-jnp.abs(x)))
    gates_s[...] = jnp.where(lane < 2 * GDN_HEADS, -jnp.exp(alog_ref[...]) * softplus, _sigmoid(ab))

    ci = lax.broadcasted_iota(jnp.int32, (CHUNK, 2 * CHUNK), 0)
    cj2 = lax.broadcasted_iota(jnp.int32, (CHUNK, 2 * CHUNK), 1)
    right = cj2 >= CHUNK
    cj = cj2 & (CHUNK - 1)
    eye = jnp.where(ci == cj, 1.0, 0.0)
    incl = (ci >= cj, ci <= cj)
    strict = (ci > cj, ci < cj)
    tri3 = tuple(jnp.concatenate([jnp.where(m[:, :CHUNK], 1.0, 0.0).astype(BF16)] * 3, axis=1) for m in incl)
    ones3 = jnp.ones((CHUNK, 3 * CHUNK), BF16)
    last_row = (CHUNK - 1, 0)

    unroll = CUM_CHUNKS if nc % CUM_CHUNKS == 0 else 1

    def cumulate(c, carry):
        rows = [pl.ds(pl.multiple_of((c * unroll + j) * CHUNK, CHUNK), CHUNK) for j in range(unroll)]
        chains = [(j, d) for j in range(unroll) for d in range(2)]
        def split3(x):
            x1 = x.astype(BF16)
            r1 = x - x1.astype(F32)
            x2 = r1.astype(BF16)
            return jnp.concatenate([x1, x2, (r1 - x2.astype(F32)).astype(BF16)], axis=0)

        parts = [split3(gs[d, rows[j], :]) for j, d in chains]
        b = [_dot(tri3[d], p) for (j, d), p in zip(chains, parts)]
        diag = [split3(x * eye) for x in b]
        b_row = [_dot(ones3, x) for x in diag]
        for (j, d), x, y in zip(chains, b, b_row):
            gs[d, rows[j], :] = x
            brs[d, rows[j], :] = y
        return carry

    def prep(hh, it, carry):
        unroll = PREP_CHUNKS if nc % PREP_CHUNKS == 0 else 1
        cs = [it * unroll + j for j in range(unroll)]
        rows = [pl.ds(pl.multiple_of(c * CHUNK, CHUNK), CHUNK) for c in cs]
        q = [qs[r, :] for r in rows]
        k = [ks[r, :] for r in rows]
        kb = [x.astype(BF16) for x in k]
        kk = [_dot_nt(x, jnp.concatenate([x, x], axis=0)) for x in kb]
        qk = [_dot_nt(x.astype(BF16), y) for x, y in zip(q, kb)]
        chains = [(j, d) for j in range(unroll) for d in range(2)]
        b = [gs[d, rows[j], :] for j, d in chains]
        beta = [bs[d, rows[j], :] for j, d in chains]
        gamma = [jnp.exp(jnp.where(incl[d], b[n] - brs[d, rows[j], :], -jnp.inf))
                 for n, (j, d) in enumerate(chains)]
        pt = [jnp.where(right, eye, jnp.where(strict[d], -beta[n] * kk[j] * gamma[n], 0.0))
              for n, (j, d) in enumerate(chains)]
        for _ in range(int(np.log2(CHUNK))):
            ptb = [x.astype(BF16) for x in pt]
            prod = [_dot(x[:, :CHUNK], x) for x in ptb]
            pt = [y + jnp.where(right, x, 0.0) for x, y in zip(pt, prod)]
        eb = [jnp.exp(x) for x in b]
        rhs = [jnp.concatenate([jnp.zeros((CHUNK, GDN_DV + GDN_DK), F32), jnp.concatenate(
                   [beta[n] * vs[rows[j], :], beta[n] * eb[n] * k[j]], axis=1)], axis=0).astype(BF16)
               for n, (j, d) in enumerate(chains)]
        wk = [_dot(t.astype(BF16), r).astype(BF16) for t, r in zip(pt, rhs)]
        b_last = [b[n][last_row[d]:last_row[d] + 1, :] for n, (j, d) in enumerate(chains)]
        kdt = [jnp.transpose(k[j] * jnp.exp(b_last[n] - b[n])).astype(BF16)
               for n, (j, d) in enumerate(chains)]
        bm = [_dot(x, y) for x, y in zip(kdt, wk)]
        pw = [_dot((qk[j] * gamma[n][:, :CHUNK]).astype(BF16), wk[n])
              for n, (j, d) in enumerate(chains)]
        for n, (j, d) in enumerate(chains):
            c = cs[j]
            bns[hh, d, pl.ds(pl.multiple_of(c * GDN_DK, GDN_DK), GDN_DK), :] = bm[n][:, :GDN_DV]
            r3 = pl.multiple_of(c * (GDN_DK + CHUNK), CHUNK)
            mqs[hh, d, pl.ds(r3, GDN_DK), :] = bm[n][:, GDN_DV:].astype(BF16)
            mqs[hh, d, pl.ds(r3 + GDN_DK, CHUNK), :] = (q[j] * eb[n] - pw[n][:, GDN_DV:]).astype(BF16)
            ges[hh, d, pl.ds(pl.multiple_of(c * 8, 8), 8), :] = jnp.broadcast_to(jnp.exp(b_last[n]), (8, LANES))
        for j in range(unroll):
            os_[hh, rows[j], :] = pw[2 * j][:, :GDN_DV] + pw[2 * j + 1][:, :GDN_DV]
        return carry

    for hh in range(GDN_HPS):
        stage(hh)
        lax.fori_loop(0, nc // unroll, cumulate, 0)
        lax.fori_loop(0, nc // (PREP_CHUNKS if nc % PREP_CHUNKS == 0 else 1), functools.partial(prep, hh), 0)

    chains = [(hh, d) for hh in range(GDN_HPS) for d in range(2)]

    def step(i, states):
        cs = (i, nc - 1 - i)
        sb = [x.astype(BF16) for x in states]
        r = [_dot(mqs[hh, d, pl.ds(pl.multiple_of(cs[d] * (GDN_DK + CHUNK), CHUNK), GDN_DK + CHUNK), :], sb[n])
             for n, (hh, d) in enumerate(chains)]
        new = []
        for n, (hh, d) in enumerate(chains):
            c = cs[d]
            os_[hh, pl.ds(pl.multiple_of(c * CHUNK, CHUNK), CHUNK), :] += r[n][GDN_DK:]
            ge = ges[hh, d, pl.ds(pl.multiple_of(c * 8, 8), 1), :]
            bn = bns[hh, d, pl.ds(pl.multiple_of(c * GDN_DK, GDN_DK), GDN_DK), :]
            new.append(ge * states[n] - r[n][:GDN_DK] + bn)
        return tuple(new)

    s0 = jnp.zeros((GDN_DK, GDN_DV), F32)
    lax.fori_loop(0, nc, step, (s0,) * len(chains))

    for hh in range(GDN_HPS):
        cols = slice(hh * LANES, (hh + 1) * LANES)
        o_ref[:, cols] = (_rms_norm(os_[hh], nw_ref[...]) * _silu(z_ref[:, cols].astype(F32))).astype(o_ref.dtype)


def _gdn(qkvz, ab, conv_w, a_log, dt_bias, norm_w, bsz, s_len):
    nh = GDN_HEADS
    qkvz3 = qkvz.reshape(bsz, s_len, QKVZ_W)
    ab3 = ab.reshape(bsz, s_len, AB_W)
    pad = jnp.zeros((AB_W - 2 * nh,), F32)
    alog = jnp.concatenate([a_log.reshape(-1), pad])[None, :]
    dtb = jnp.concatenate([dt_bias.reshape(-1), pad])[None, :]

    hps = GDN_HPS
    ng = nh // hps

    def col(k):
        return pl.BlockSpec((None, s_len, hps * LANES), lambda b, h: (b, 0, k * ng + h))

    def cw(k):
        return pl.BlockSpec((CONV_K, hps * LANES), lambda b, h: (0, k * ng + h))

    vec = pl.BlockSpec((1, LANES), lambda b, h: (0, 0))
    nc = s_len // CHUNK
    scratch = [
        pltpu.VMEM((s_len, LANES), F32), pltpu.VMEM((s_len, LANES), F32), pltpu.VMEM((s_len, LANES), F32),
        pltpu.VMEM((2, s_len, LANES), F32), pltpu.VMEM((2, s_len, LANES), F32),
        pltpu.VMEM((2, s_len, 2 * CHUNK), F32), pltpu.VMEM((hps, 2, nc * GDN_DK, GDN_DV), F32),
        pltpu.VMEM((hps, 2, nc * (GDN_DK + CHUNK), GDN_DV), BF16),
        pltpu.VMEM((hps, 2, nc * 8, LANES), F32), pltpu.VMEM((hps, s_len, GDN_DV), F32),
        pltpu.VMEM((s_len + 2 * CONV_PAD, LANES), F32), pltpu.VMEM((s_len, AB_W), F32),
    ]
    out = pl.pallas_call(
        _gdn_kernel, out_shape=jax.ShapeDtypeStruct((bsz, s_len, nh * GDN_DV), BF16), grid=(bsz, ng),
        in_specs=[col(0), col(1), col(2), col(3),
                  pl.BlockSpec((None, s_len, AB_W), lambda b, h: (b, 0, 0)),
                  cw(0), cw(1), cw(2), vec, vec, vec],
        out_specs=pl.BlockSpec((None, s_len, hps * GDN_DV), lambda b, h: (b, 0, h)),
        scratch_shapes=scratch, compiler_params=_params(("parallel", "parallel")), name="gdn",
    )(qkvz3, qkvz3, qkvz3, qkvz3, ab3, conv_w, conv_w, conv_w, alog, dtb, norm_w.reshape(1, GDN_DV))
    return out.reshape(bsz * s_len, nh * GDN_DV)


MLA_TQ = 256
MLA_KC = 1024


def _mla_kernel(cq_ref, ckv_ref, kab_ref, mq_ref, mk1_ref, mk2_ref, qnw_ref, kvnw_ref, wq_ref, wkv_ref,
                o_ref, q_s, k_s, vt_s, sta_s, stb_s, cqn_s, ckvn_s):
    s_len = cq_ref.shape[0]
    scale = (QK_NOPE + QK_ROPE) ** -0.5

    @pl.when(pl.program_id(1) == 0)
    def _():
        cqn_s[...] = _rms_norm(cq_ref[...], qnw_ref[...]).astype(BF16)
        ckvn_s[...] = _rms_norm(ckv_ref[...], kvnw_ref[...]).astype(BF16)
        kab = kab_ref[...]
        k_s[:, QK_NOPE:] = (kab[:, :LANES] * mk1_ref[...] + kab[:, LANES:] * mk2_ref[...]).astype(BF16)
        vt_s[V_HEAD:, :] = jnp.ones((vt_s.shape[0] - V_HEAD, s_len), BF16)

    q = _dot(cqn_s[...], wq_ref[...]) * scale
    q_s[:, :QK_NOPE] = q[:, :QK_NOPE].astype(BF16)
    q_s[:, QK_NOPE:] = (q[:, QK_NOPE:] * mq_ref[...]).astype(BF16)
    kv = _dot(ckvn_s[...], wkv_ref[...])
    k_s[:, :QK_NOPE] = kv[:, :QK_NOPE].astype(BF16)
    vt_s[:V_HEAD, :] = jnp.transpose(kv[:, QK_NOPE:]).astype(BF16)

    tq = min(MLA_TQ, s_len)

    nq = s_len // tq
    kc = min(MLA_KC, s_len)

    def scores(i, st):
        st[...] = _dot_nt(k_s[...], q_s[_ds(i, tq), :])

    def attend(i, st, nxt):
        m = jnp.max(st[...], 0, keepdims=True)
        ot = jnp.zeros((vt_s.shape[0], tq), F32)
        for c in range(0, s_len, kc):
            if nxt is not None:
                nxt[c:c + kc, :] = _dot_nt(k_s[c:c + kc, :], q_s[_ds(i + 1, tq), :])
            p = jnp.exp(st[c:c + kc, :] - m).astype(BF16)
            ot = ot + _dot(vt_s[:, c:c + kc], p)
        o = jnp.transpose(ot[:V_HEAD] / ot[V_HEAD:V_HEAD + 1])
        o_ref[_ds(i, tq), :] = o.astype(o_ref.dtype)

    def pair(j, prefetch):
        attend(2 * j, sta_s, stb_s)
        attend(2 * j + 1, stb_s, sta_s if prefetch else None)

    assert nq == 1 or nq % 2 == 0
    scores(0, sta_s)
    if nq == 1:
        attend(0, sta_s, None)
    else:
        def body(j, carry):
            pair(j, True)
            return carry
        lax.fori_loop(0, nq // 2 - 1, body, 0)
        pair(nq // 2 - 1, False)


def _pack_w_uq(w_uq):
    w = w_uq.reshape(Q_LORA, MLA_HEADS, QK_NOPE + QK_ROPE)
    half = QK_ROPE // 2
    r1 = w[..., QK_NOPE:QK_NOPE + half]
    r2 = w[..., QK_NOPE + half:]
    w = jnp.concatenate([w[..., :QK_NOPE], r1, r1, -r2, r2], axis=-1)
    return w.reshape(Q_LORA, MLA_HEADS * 2 * LANES).astype(BF16)


def _mla(cq, ckv, kab, tables, qnw, kvnw, wq, wkv, bsz, s_len):
    nh = MLA_HEADS

    def per_b(width):
        return pl.BlockSpec((None, s_len, width), lambda b, h: (b, 0, 0))

    def vec(width):
        return pl.BlockSpec((1, width), lambda b, h: (0, 0))

    mq, mk1, mk2 = (t.reshape(bsz, s_len, LANES) for t in tables)
    out = pl.pallas_call(
        _mla_kernel, out_shape=jax.ShapeDtypeStruct((bsz, s_len, nh * V_HEAD), BF16), grid=(bsz, nh),
        in_specs=[per_b(Q_LORA), per_b(KV_LORA), per_b(KAB_W), per_b(LANES), per_b(LANES), per_b(LANES),
                  vec(Q_LORA), vec(KV_LORA),
                  pl.BlockSpec((Q_LORA, 2 * LANES), lambda b, h: (0, h)),
                  pl.BlockSpec((KV_LORA, QK_NOPE + V_HEAD), lambda b, h: (0, h))],
        out_specs=pl.BlockSpec((None, s_len, V_HEAD), lambda b, h: (b, 0, h)),
        scratch_shapes=[pltpu.VMEM((s_len, 2 * LANES), BF16), pltpu.VMEM((s_len, 2 * LANES), BF16),
                        pltpu.VMEM((V_HEAD + 16, s_len), BF16),
                        pltpu.VMEM((s_len, min(MLA_TQ, s_len)), F32),
                        pltpu.VMEM((s_len, min(MLA_TQ, s_len)), F32),
                        pltpu.VMEM((s_len, Q_LORA), BF16), pltpu.VMEM((s_len, KV_LORA), BF16)],
        compiler_params=_params(("parallel", "arbitrary")), name="mla",
    )(cq.reshape(bsz, s_len, Q_LORA), ckv.reshape(bsz, s_len, KV_LORA), kab.reshape(bsz, s_len, KAB_W),
      mq, mk1, mk2, qnw.reshape(1, Q_LORA), kvnw.reshape(1, KV_LORA), wq, wkv)
    return out.reshape(bsz * s_len, nh * V_HEAD)


def _tail_kernel(og_ref, om_ref, gates_ref, h_ref, wog_ref, wom_ref, wout_ref, g_ref, b_ref, o_ref, ob_ref):
    d = h_ref.shape[1]
    y_gdn = _dot(og_ref[...], wog_ref[...])
    y_mla = _dot(om_ref[...], wom_ref[...])
    y = (_sigmoid(gates_ref[:, :d].astype(F32)) * y_gdn + _sigmoid(gates_ref[:, d:].astype(F32)) * y_mla)
    m = _dot(y.astype(BF16), wout_ref[...])
    out = _layer_norm(DN_ALPHA * h_ref[...] + m, g_ref[...], b_ref[...])
    o_ref[...] = out
    ob_ref[...] = out.astype(BF16)


def _mixer_tail(og, om, gates, h, wog, wom, wout, g, b, tm):
    t, d = h.shape
    row = lambda w: pl.BlockSpec((tm, w), lambda i: (i, 0))
    full = pl.BlockSpec((d, d), lambda i: (0, 0))
    vec = pl.BlockSpec((1, d), lambda i: (0, 0))
    return pl.pallas_call(
        _tail_kernel, out_shape=(jax.ShapeDtypeStruct((t, d), F32), jax.ShapeDtypeStruct((t, d), BF16)),
        grid=(t // tm,),
        in_specs=[row(d), row(d), row(2 * d), row(d), full, full, full, vec, vec],
        out_specs=(row(d), row(d)), compiler_params=_params(("parallel",)), name="mixer_tail",
    )(og, om, gates, h, wog, wom, wout, g.reshape(1, d), b.reshape(1, d))


def _router_kernel(h_ref, wh_ref, wl_ref, b_ref, eid_ref, gate_ref, cnt_ref, run):
    @pl.when(pl.program_id(0) == 0)
    def _():
        run[...] = jnp.zeros_like(run)

    x = h_ref[...]
    xh = x.astype(BF16)
    xl = (x - xh.astype(F32)).astype(BF16)
    logits = _dot(xh, wh_ref[...]) + (_dot(xl, wh_ref[...]) + _dot(xh, wl_ref[...])) + b_ref[...]
    tm = logits.shape[0]
    lane = lax.broadcasted_iota(jnp.int32, logits.shape, 1)
    lane_f = lane.astype(F32)
    neg = -jnp.inf

    def first_max(vals):
        m = jnp.max(vals, -1, keepdims=True)
        idx = jnp.min(jnp.where(vals == m, lane_f, float(LANES)), -1, keepdims=True)
        return m, idx

    is_grp = lane < N_GROUPS
    mg, grp = first_max(jnp.where(is_grp, logits, neg))
    p_grp = 1.0 / jnp.sum(jnp.where(is_grp, jnp.exp(logits - mg), 0.0), -1, keepdims=True)
    in_grp = ((lane - N_GROUPS) >> 3).astype(F32) == grp
    le = jnp.where(in_grp, logits, neg)
    l1, i1 = first_max(le)
    l2, i2 = first_max(jnp.where(lane_f == i1, neg, le))
    t = jnp.exp(l2 - l1)
    g1 = p_grp / (1.0 + t)
    g2 = p_grp * t / (1.0 + t)
    hit1 = lane_f == i1
    hit2 = lane_f == i2
    onehot = jnp.where(hit1 | hit2, 1.0, 0.0)
    ri = lax.broadcasted_iota(jnp.int32, (tm, tm), 0)
    rj = lax.broadcasted_iota(jnp.int32, (tm, tm), 1)
    before = jnp.where(ri > rj, 1.0, 0.0).astype(BF16)
    prefix = _dot(before, onehot.astype(BF16)) + run[...]
    rank1 = jnp.sum(jnp.where(hit1, prefix, 0.0), -1, keepdims=True)
    rank2 = jnp.sum(jnp.where(hit2, prefix, 0.0), -1, keepdims=True)
    run[...] += jnp.sum(onehot, 0, keepdims=True)
    cnt_ref[...] = run[...]
    ids = jnp.where(lane == 0, i1 - float(N_GROUPS), jnp.where(lane == 1, i2 - float(N_GROUPS),
                    jnp.where(lane == 2, rank1, jnp.where(lane == 3, rank2, 0.0))))
    eid_ref[...] = ids.astype(jnp.int32)
    gate_ref[...] = jnp.where(lane == 0, g1, jnp.where(lane == 1, g2, 0.0))


def _router(h, w_rg, b_rg, w_re, b_re, tm):
    t, d = h.shape
    padw = LANES - N_GROUPS - N_EXPERTS
    w = jnp.concatenate([w_rg, w_re, jnp.zeros((d, padw), F32)], axis=1)
    b = jnp.concatenate([b_rg, b_re, jnp.zeros((padw,), F32)])[None, :]
    wh = w.astype(BF16)
    wl = (w - wh.astype(F32)).astype(BF16)
    row = pl.BlockSpec((tm, LANES), lambda i: (i, 0))
    one = pl.BlockSpec((1, LANES), lambda i: (0, 0))
    wspec = pl.BlockSpec((d, LANES), lambda i: (0, 0))
    return pl.pallas_call(
        _router_kernel,
        out_shape=(jax.ShapeDtypeStruct((t, LANES), jnp.int32), jax.ShapeDtypeStruct((t, LANES), F32),
                   jax.ShapeDtypeStruct((1, LANES), F32)),
        grid=(t // tm,),
        in_specs=[pl.BlockSpec((tm, d), lambda i: (i, 0)), wspec, wspec, one],
        out_specs=(row, row, one), scratch_shapes=[pltpu.VMEM((1, LANES), F32)],
        compiler_params=_params(("arbitrary",)), name="router",
    )(h, wh, wl, b)


def _dest_kernel(ids_ref, start_ref, o_ref):
    ids = ids_ref[...]
    lane = lax.broadcasted_iota(jnp.int32, ids.shape, 1)
    start = start_ref[...]

    def row_of(k):
        hit = lane == ids[:, k:k + 1] + N_GROUPS
        return jnp.sum(jnp.where(hit, start, 0.0), -1, keepdims=True).astype(jnp.int32) + ids[:, TOP_K + k:TOP_K + k + 1]

    o_ref[...] = jnp.where(lane == 0, row_of(0), jnp.where(lane == 1, row_of(1), 0))


def _dest_rows(ids, pad_start, tm):
    t = ids.shape[0]
    start = jnp.zeros((1, LANES), F32).at[0, N_GROUPS:N_GROUPS + N_EXPERTS].set(pad_start.astype(F32))
    row = pl.BlockSpec((tm, LANES), lambda i: (i, 0))
    return pl.pallas_call(
        _dest_kernel, out_shape=jax.ShapeDtypeStruct((t, LANES), jnp.int32), grid=(t // tm,),
        in_specs=[row, pl.BlockSpec((1, LANES), lambda i: (0, 0))], out_specs=row,
        compiler_params=_params(("parallel",)), name="dest_rows",
    )(ids, start)


def _expert_kernel(be_ref, nu_ref, x_ref, wg_ref, wu_ref, wd_ref, o_ref, wgb, wub, wdb):
    i = pl.program_id(0)

    @pl.when((i == 0) | (be_ref[i] != be_ref[jnp.maximum(i - 1, 0)]))
    def _():
        wgb[...] = wg_ref[...].astype(BF16)
        wub[...] = wu_ref[...].astype(BF16)
        wdb[...] = wd_ref[...].astype(BF16)

    @pl.when(i < nu_ref[0])
    def _():
        x = x_ref[...]
        hid = _silu(_dot(x, wgb[...])) * _dot(x, wub[...])
        o_ref[...] = _dot(hid.astype(BF16), wdb[...]).astype(o_ref.dtype)

    @pl.when(i >= nu_ref[0])
    def _():
        o_ref[...] = jnp.zeros_like(o_ref)


def _experts(xb, blk_expert, n_used, w_gate, w_up, w_down, layer):
    n_rows, d = xb.shape
    n_blocks = n_rows // MOE_BLOCK
    wspec = lambda shape: pl.BlockSpec((None, None) + shape, lambda i, be, nu: (layer, be[i], 0, 0))
    grid_spec = pltpu.PrefetchScalarGridSpec(
        num_scalar_prefetch=2, grid=(n_blocks,),
        in_specs=[pl.BlockSpec((MOE_BLOCK, d), lambda i, be, nu: (i, 0)),
                  wspec((d, D_EXPERT)), wspec((d, D_EXPERT)), wspec((D_EXPERT, d))],
        out_specs=pl.BlockSpec((MOE_BLOCK, d), lambda i, be, nu: (i, 0)),
        scratch_shapes=[pltpu.VMEM((d, D_EXPERT), BF16), pltpu.VMEM((d, D_EXPERT), BF16),
                        pltpu.VMEM((D_EXPERT, d), BF16)])
    return pl.pallas_call(
        _expert_kernel, out_shape=jax.ShapeDtypeStruct((n_rows, d), BF16), grid_spec=grid_spec,
        compiler_params=_params(("arbitrary",)), name="experts",
    )(blk_expert, n_used, xb, w_gate, w_up, w_down)


def _combine_kernel(h_ref, y1_ref, y2_ref, gate_ref, g_ref, b_ref, o_ref):
    f = gate_ref[:, 0:1] * y1_ref[...].astype(F32) + gate_ref[:, 1:2] * y2_ref[...].astype(F32)
    o_ref[...] = _layer_norm(DN_ALPHA * h_ref[...] + f, g_ref[...], b_ref[...])


def _combine(h, y1, y2, gate, g, b, tm):
    t, d = h.shape
    row = pl.BlockSpec((tm, d), lambda i: (i, 0))
    vec = pl.BlockSpec((1, d), lambda i: (0, 0))
    return pl.pallas_call(
        _combine_kernel, out_shape=jax.ShapeDtypeStruct((t, d), F32), grid=(t // tm,),
        in_specs=[row, row, row, pl.BlockSpec((tm, LANES), lambda i: (i, 0)), vec, vec], out_specs=row,
        compiler_params=_params(("parallel",)), name="combine_ln",
    )(h, y1, y2, gate, g.reshape(1, d), b.reshape(1, d))


def _dispatch(ids, cnt, tm):
    t = ids.shape[0]
    n_assign = t * TOP_K
    counts = cnt[0, N_GROUPS:N_GROUPS + N_EXPERTS].astype(jnp.int32)
    padded = (counts + MOE_BLOCK - 1) // MOE_BLOCK * MOE_BLOCK
    pad_end = jnp.cumsum(padded)
    pad_start = pad_end - padded
    start = jnp.cumsum(counts) - counts
    eid = ids[:, :TOP_K]
    dest = _dest_rows(ids, pad_start, tm)[:, :TOP_K]
    order = jnp.argsort(eid.reshape(n_assign))
    n_blocks = -(-(n_assign + N_EXPERTS * (MOE_BLOCK - 1)) // MOE_BLOCK)
    blk_first = (jnp.arange(n_blocks) * MOE_BLOCK)[:, None]
    ended = jnp.dot((pad_end[None, :] <= blk_first).astype(F32), jnp.ones((N_EXPERTS,), F32))
    blk_expert = jnp.minimum(ended.astype(jnp.int32), N_EXPERTS - 1)
    shift = jnp.repeat(jnp.take(pad_start - start, blk_expert, mode="clip"), MOE_BLOCK)
    end = jnp.repeat(jnp.take(start + counts, blk_expert, mode="clip"), MOE_BLOCK)
    rows = jnp.arange(n_blocks * MOE_BLOCK)
    src = rows - shift
    row_tok = jnp.where(src < end, jnp.take(order, jnp.clip(src, 0, n_assign - 1), mode="clip") // TOP_K,
                        rows % t).astype(jnp.int32)
    n_used = (pad_end[-1:] // MOE_BLOCK).astype(jnp.int32)
    return row_tok, dest, blk_expert, n_used


def _moe(h, hb, w_rg, b_rg, w_re, b_re, w_gate, w_up, w_down, layer, g, b, tm):
    ids, gate, cnt = _router(h, w_rg, b_rg, w_re, b_re, tm)
    row_tok, dest, blk_expert, n_used = _dispatch(ids, cnt, tm)
    xb = jnp.take(hb, row_tok, axis=0, mode="clip")
    yb = _experts(xb, blk_expert, n_used, w_gate, w_up, w_down, layer)
    y1 = jnp.take(yb, dest[:, 0], axis=0, mode="clip")
    y2 = jnp.take(yb, dest[:, 1], axis=0, mode="clip")
    return _combine(h, y1, y2, gate, g, b, min(ROWS_MEMBOUND, h.shape[0]))


def kernel(x, positions, ln_in_g, ln_in_b, w_in, conv_w, a_log, dt_bias, gdn_norm_w, w_o_gdn, mla_q_norm_w, w_uq, mla_kv_norm_w, w_ukv, w_o_mla, w_out, ln1_g, ln1_b, w_router_group, b_router_group, w_router_expert, b_router_expert, w_gate, w_up, w_down, ln2_g, ln2_b):
    bsz, s_len, d = x.shape
    t = bsz * s_len
    tm = min(ROWS_MATMUL, t)
    tables = _rope_tables(positions, min(ROWS_MEMBOUND, t))
    h = _ln_in(x.reshape(t, d), ln_in_g, ln_in_b, min(ROWS_MEMBOUND, t))
    for l in range(DEPTH):
        qkvz, cq, ckv, kab, ab, gates = _proj(h, _pack_w_in(w_in[l]), tm)
        og = _gdn(qkvz, ab, conv_w[l], a_log[l], dt_bias[l], gdn_norm_w[l], bsz, s_len)
        om = _mla(cq, ckv, kab, tables, mla_q_norm_w[l], mla_kv_norm_w[l],
                  _pack_w_uq(w_uq[l]), w_ukv[l].astype(BF16), bsz, s_len)
        h, hb = _mixer_tail(og, om, gates, h, w_o_gdn[l].astype(BF16), w_o_mla[l].astype(BF16),
                            w_out[l].astype(BF16), ln1_g[l], ln1_b[l], tm)
        h = _moe(h, hb, w_router_group[l], b_router_group[l], w_router_expert[l], b_router_expert[l],
                 w_gate, w_up, w_down, l, ln2_g[l], ln2_b[l], tm)
    return h.reshape(bsz, s_len, d)
```

```python
import functools

import numpy as np
import jax
import jax.numpy as jnp
from jax import lax
from jax.experimental import pallas as pl
from jax.experimental.pallas import tpu as pltpu

F32 = jnp.float32
BF16 = jnp.bfloat16

D_MODEL = 1024
DEPTH = 2
GDN_HEADS = 8
GDN_DK = 128
GDN_DV = 128
CONV_K = 5
CHUNK = 64
MLA_HEADS = 8
Q_LORA = 384
KV_LORA = 256
QK_NOPE = 128
QK_ROPE = 64
V_HEAD = 128
ROPE_BASE = 10000.0
N_GROUPS = 8
EXPERTS_PER_GROUP = 8
N_EXPERTS = N_GROUPS * EXPERTS_PER_GROUP
TOP_K = 2
D_EXPERT = 512
MOE_BLOCK = 256
DN_ALPHA = (2 * DEPTH) ** 0.25
LN_EPS = 1e-5
RMS_EPS = 1e-6

ROWS_MATMUL = 256
ROWS_MEMBOUND = 512
LANES = 128
VMEM_LIMIT = 56 * 1024 * 1024

QKVZ_W = 4 * GDN_HEADS * GDN_DK
KAB_W = 2 * LANES
AB_W = LANES
GATES_W = 2 * D_MODEL
PROJ_SEGS = (QKVZ_W, Q_LORA, KV_LORA, KAB_W, AB_W, GATES_W)
PROJ_W = sum(PROJ_SEGS)


def _params(sem):
    return pltpu.CompilerParams(dimension_semantics=sem, vmem_limit_bytes=VMEM_LIMIT)


def _layer_norm(x, g, b):
    mu = jnp.mean(x, -1, keepdims=True)
    xc = x - mu
    var = jnp.mean(xc * xc, -1, keepdims=True)
    return xc * lax.rsqrt(var + LN_EPS) * g + b


def _rms_norm(x, w):
    return x * lax.rsqrt(jnp.mean(x * x, -1, keepdims=True) + RMS_EPS) * w


def _silu(x):
    return (0.5 * x) * (1.0 + jnp.tanh(0.5 * x))


def _sigmoid(x):
    return 1.0 / (1.0 + jnp.exp(-x))


def _dot(a, b):
    return jnp.dot(a, b, preferred_element_type=F32)


def _dot_nt(a, b):
    return lax.dot_general(a, b, (((1,), (1,)), ((), ())), preferred_element_type=F32)


def _ds(i, n):
    return pl.ds(i * n, n) if isinstance(i, int) else pl.ds(pl.multiple_of(i * n, n), n)


def _dot_tn(a, b):
    return lax.dot_general(a, b, (((0,), (0,)), ((), ())), preferred_element_type=F32)


def _rope_kernel(pos_ref, freq_ref, mq_ref, mk1_ref, mk2_ref):
    ang = pos_ref[...].astype(F32) * freq_ref[...]
    c = jnp.cos(ang)
    s = jnp.sin(ang)
    quarter = lax.broadcasted_iota(jnp.int32, ang.shape, 1) >> 5
    even = (quarter & 1) == 0
    mq_ref[...] = jnp.where((quarter == 0) | (quarter == 3), c, s)
    mk1_ref[...] = jnp.where(even, c, s)
    mk2_ref[...] = jnp.where(even, s, c)


def _rope_tables(positions, tm):
    t = positions.size
    half = QK_ROPE // 2
    inv_freq = jnp.power(ROPE_BASE, -jnp.arange(half, dtype=F32) / half)
    freq = jnp.tile(inv_freq, LANES // half)[None, :]
    out = jax.ShapeDtypeStruct((t, LANES), F32)
    row = pl.BlockSpec((tm, LANES), lambda i: (i, 0))
    return pl.pallas_call(
        _rope_kernel, out_shape=(out, out, out), grid=(t // tm,),
        in_specs=[pl.BlockSpec((tm, 1), lambda i: (i, 0)), pl.BlockSpec((1, LANES), lambda i: (0, 0))],
        out_specs=(row, row, row), compiler_params=_params(("parallel",)), name="rope_tables",
    )(positions.reshape(t, 1), freq)


def _ln_kernel(x_ref, g_ref, b_ref, o_ref):
    o_ref[...] = _layer_norm(x_ref[...], g_ref[...], b_ref[...])


def _ln_in(x2, g, b, tm):
    t, d = x2.shape
    row = pl.BlockSpec((tm, d), lambda i: (i, 0))
    vec = pl.BlockSpec((1, d), lambda i: (0, 0))
    return pl.pallas_call(
        _ln_kernel, out_shape=jax.ShapeDtypeStruct((t, d), F32), grid=(t // tm,),
        in_specs=[row, vec, vec], out_specs=row, compiler_params=_params(("parallel",)), name="ln_in",
    )(x2, g.reshape(1, d), b.reshape(1, d))


PROJ_CHUNK = 512


def _proj_kernel(h_ref, w_ref, *out_refs):
    a = h_ref[...].astype(BF16)
    off = 0
    for ref, width in zip(out_refs, PROJ_SEGS):
        for c in range(0, width, PROJ_CHUNK):
            wc = min(PROJ_CHUNK, width - c)
            ref[:, c:c + wc] = _dot(a, w_ref[:, off + c:off + c + wc]).astype(ref.dtype)
        off += width


def _proj(h, w_all, tm):
    t, d = h.shape
    dts = (BF16, F32, F32, F32, F32, BF16)
    outs = tuple(jax.ShapeDtypeStruct((t, w), dt) for w, dt in zip(PROJ_SEGS, dts))
    return pl.pallas_call(
        _proj_kernel, out_shape=outs, grid=(t // tm,),
        in_specs=[pl.BlockSpec((tm, d), lambda i: (i, 0)), pl.BlockSpec((d, PROJ_W), lambda i: (0, 0))],
        out_specs=tuple(pl.BlockSpec((tm, w), lambda i: (i, 0)) for w in PROJ_SEGS),
        compiler_params=_params(("parallel",)), name="in_proj",
    )(h, w_all)


def _pack_w_in(w_in):
    hq = GDN_HEADS * GDN_DK
    o_a = 4 * hq
    o_bt = o_a + 2 * GDN_HEADS
    o_cq = o_bt + 2 * GDN_HEADS
    o_ckv = o_cq + Q_LORA
    o_kr = o_ckv + KV_LORA
    o_g = o_kr + QK_ROPE
    half = QK_ROPE // 2
    k1 = w_in[:, o_kr:o_kr + half]
    k2 = w_in[:, o_kr + half:o_g]
    d = w_in.shape[0]
    ab = jnp.concatenate([w_in[:, o_a:o_cq], jnp.zeros((d, AB_W - 4 * GDN_HEADS), w_in.dtype)], 1)
    return jnp.concatenate(
        [w_in[:, :o_a], w_in[:, o_cq:o_ckv], w_in[:, o_ckv:o_kr],
         k1, k1, k1, k1, -k2, k2, -k2, k2, ab, w_in[:, o_g:]], axis=1).astype(BF16)


GDN_HPS = 2
CONV_PAD = 8
CUM_CHUNKS = 8
PREP_CHUNKS = 16


def _gdn_kernel(q_ref, k_ref, v_ref, z_ref, ab_ref, cwq_ref, cwk_ref, cwv_ref, alog_ref, dtb_ref, nw_ref,
                o_ref, qs, ks, vs, gs, bs, brs, bns, mqs, ges, os_, xp, gates_s):
    s_len = q_ref.shape[0]
    nc = s_len // CHUNK
    first_head = pl.program_id(1) * GDN_HPS

    xp[:CONV_PAD, :] = jnp.zeros((CONV_PAD, LANES), F32)
    xp[CONV_PAD + s_len:, :] = jnp.zeros((CONV_PAD, LANES), F32)

    def conv_silu(x_ref, cw_ref, cols):
        xp[CONV_PAD:CONV_PAD + s_len, :] = x_ref[:, cols].astype(F32)
        acc = None
        for j in range(CONV_K):
            term = xp[CONV_PAD + j - CONV_K // 2:CONV_PAD + j - CONV_K // 2 + s_len, :] * cw_ref[j:j + 1, cols]
            acc = term if acc is None else acc + term
        return _silu(acc)

    def l2n(x, scale=1.0):
        return x * (lax.rsqrt(jnp.sum(x * x, -1, keepdims=True) + RMS_EPS) * scale)

    lane = lax.broadcasted_iota(jnp.int32, (1, LANES), 1)

    def pick(vals, idx):
        col = jnp.sum(jnp.where(lane == idx, vals, 0.0), -1, keepdims=True)
        return jnp.broadcast_to(col, (s_len, LANES))

    def stage(hh):
        cols = slice(hh * LANES, (hh + 1) * LANES)
        qs[...] = l2n(conv_silu(q_ref, cwq_ref, cols), GDN_DK ** -0.5)
        ks[...] = l2n(conv_silu(k_ref, cwk_ref, cols))
        vs[...] = conv_silu(v_ref, cwv_ref, cols)
        gates = gates_s[...]
        for d in range(2):
            gs[d] = pick(gates, first_head + hh + d * GDN_HEADS)
            bs[d] = pick(gates, first_head + hh + (2 + d) * GDN_HEADS)

    ab = ab_ref[...]
    x = ab + dtb_ref[...]
    softplus = jnp.maximum(x, 0.0) + jnp.log(1.0 + jnp.exp(-jnp.abs(x)))
    gates_s[...] = jnp.where(lane < 2 * GDN_HEADS, -jnp.exp(alog_ref[...]) * softplus, _sigmoid(ab))

    ci = lax.broadcasted_iota(jnp.int32, (CHUNK, 2 * CHUNK), 0)
    cj2 = lax.broadcasted_iota(jnp.int32, (CHUNK, 2 * CHUNK), 1)
    right = cj2 >= CHUNK
    cj = cj2 & (CHUNK - 1)
    eye = jnp.where(ci == cj, 1.0, 0.0)
    incl = (ci >= cj, ci <= cj)
    strict = (ci > cj, ci < cj)
    tri3 = tuple(jnp.concatenate([jnp.where(m[:, :CHUNK], 1.0, 0.0).astype(BF16)] * 3, axis=1) for m in incl)
    ones3 = jnp.ones((CHUNK, 3 * CHUNK), BF16)
    last_row = (CHUNK - 1, 0)

    unroll = CUM_CHUNKS if nc % CUM_CHUNKS == 0 else 1

    def cumulate(c, carry):
        rows = [pl.ds(pl.multiple_of((c * unroll + j) * CHUNK, CHUNK), CHUNK) for j in range(unroll)]
        chains = [(j, d) for j in range(unroll) for d in range(2)]
        def split3(x):
            x1 = x.astype(BF16)
            r1 = x - x1.astype(F32)
            x2 = r1.astype(BF16)
            return jnp.concatenate([x1, x2, (r1 - x2.astype(F32)).astype(BF16)], axis=0)

        parts = [split3(gs[d, rows[j], :]) for j, d in chains]
        b = [_dot(tri3[d], p) for (j, d), p in zip(chains, parts)]
        diag = [split3(x * eye) for x in b]
        b_row = [_dot(ones3, x) for x in diag]
        for (j, d), x, y in zip(chains, b, b_row):
            gs[d, rows[j], :] = x
            brs[d, rows[j], :] = y
        return carry

    def prep(hh, it, carry):
        unroll = PREP_CHUNKS if nc % PREP_CHUNKS == 0 else 1
        cs = [it * unroll + j for j in range(unroll)]
        rows = [pl.ds(pl.multiple_of(c * CHUNK, CHUNK), CHUNK) for c in cs]
        q = [qs[r, :] for r in rows]
        k = [ks[r, :] for r in rows]
        kb = [x.astype(BF16) for x in k]
        kk = [_dot_nt(x, jnp.concatenate([x, x], axis=0)) for x in kb]
        qk = [_dot_nt(x.astype(BF16), y) for x, y in zip(q, kb)]
        chains = [(j, d) for j in range(unroll) for d in range(2)]
        b = [gs[d, rows[j], :] for j, d in chains]
        beta = [bs[d, rows[j], :] for j, d in chains]
        gamma = [jnp.exp(jnp.where(incl[d], b[n] - brs[d, rows[j], :], -jnp.inf))
                 for n, (j, d) in enumerate(chains)]
        pt = [jnp.where(right, eye, jnp.where(strict[d], -beta[n] * kk[j] * gamma[n], 0.0))
              for n, (j, d) in enumerate(chains)]
        for _ in range(int(np.log2(CHUNK))):
            ptb = [x.astype(BF16) for x in pt]
            prod = [_dot(x[:, :CHUNK], x) for x in ptb]
            pt = [y + jnp.where(right, x, 0.0) for x, y in zip(pt, prod)]
        eb = [jnp.exp(x) for x in b]
        rhs = [jnp.concatenate([jnp.zeros((CHUNK, GDN_DV + GDN_DK), F32), jnp.concatenate(
                   [beta[n] * vs[rows[j], :], beta[n] * eb[n] * k[j]], axis=1)], axis=0).astype(BF16)
               for n, (j, d) in enumerate(chains)]
        wk = [_dot(t.astype(BF16), r).astype(BF16) for t, r in zip(pt, rhs)]
        b_last = [b[n][last_row[d]:last_row[d] + 1, :] for n, (j, d) in enumerate(chains)]
        kdt = [jnp.transpose(k[j] * jnp.exp(b_last[n] - b[n])).astype(BF16)
               for n, (j, d) in enumerate(chains)]
        bm = [_dot(x, y) for x, y in zip(kdt, wk)]
        pw = [_dot((qk[j] * gamma[n][:, :CHUNK]).astype(BF16), wk[n])
              for n, (j, d) in enumerate(chains)]
        for n, (j, d) in enumerate(chains):
            c = cs[j]
            bns[hh, d, pl.ds(pl.multiple_of(c * GDN_DK, GDN_DK), GDN_DK), :] = bm[n][:, :GDN_DV]
            r3 = pl.multiple_of(c * (GDN_DK + CHUNK), CHUNK)
            mqs[hh, d, pl.ds(r3, GDN_DK), :] = bm[n][:, GDN_DV:].astype(BF16)
            mqs[hh, d, pl.ds(r3 + GDN_DK, CHUNK), :] = (q[j] * eb[n] - pw[n][:, GDN_DV:]).astype(BF16)
            ges[hh, d, pl.ds(pl.multiple_of(c * 8, 8), 8), :] = jnp.broadcast_to(jnp.exp(b_last[n]), (8, LANES))
        for j in range(unroll):
            os_[hh, rows[j], :] = pw[2 * j][:, :GDN_DV] + pw[2 * j + 1][:, :GDN_DV]
        return carry

    for hh in range(GDN_HPS):
        stage(hh)
        lax.fori_loop(0, nc // unroll, cumulate, 0)
        lax.fori_loop(0, nc // (PREP_CHUNKS if nc % PREP_CHUNKS == 0 else 1), functools.partial(prep, hh), 0)

    chains = [(hh, d) for hh in range(GDN_HPS) for d in range(2)]

    def step(i, states):
        cs = (i, nc - 1 - i)
        sb = [x.astype(BF16) for x in states]
        r = [_dot(mqs[hh, d, pl.ds(pl.multiple_of(cs[d] * (GDN_DK + CHUNK), CHUNK), GDN_DK + CHUNK), :], sb[n])
             for n, (hh, d) in enumerate(chains)]
        new = []
        for n, (hh, d) in enumerate(chains):
            c = cs[d]
            os_[hh, pl.ds(pl.multiple_of(c * CHUNK, CHUNK), CHUNK), :] += r[n][GDN_DK:]
            ge = ges[hh, d, pl.ds(pl.multiple_of(c * 8, 8), 1), :]
            bn = bns[hh, d, pl.ds(pl.multiple_of(c * GDN_DK, GDN_DK), GDN_DK), :]
            new.append(ge * states[n] - r[n][:GDN_DK] + bn)
        return tuple(new)

    s0 = jnp.zeros((GDN_DK, GDN_DV), F32)
    lax.fori_loop(0, nc, step, (s0,) * len(chains))

    for hh in range(GDN_HPS):
        cols = slice(hh * LANES, (hh + 1) * LANES)
        o_ref[:, cols] = (_rms_norm(os_[hh], nw_ref[...]) * _silu(z_ref[:, cols].astype(F32))).astype(o_ref.dtype)


def _gdn(qkvz, ab, conv_w, a_log, dt_bias, norm_w, bsz, s_len):
    nh = GDN_HEADS
    qkvz3 = qkvz.reshape(bsz, s_len, QKVZ_W)
    ab3 = ab.reshape(bsz, s_len, AB_W)
    pad = jnp.zeros((AB_W - 2 * nh,), F32)
    alog = jnp.concatenate([a_log.reshape(-1), pad])[None, :]
    dtb = jnp.concatenate([dt_bias.reshape(-1), pad])[None, :]

    hps = GDN_HPS
    ng = nh // hps

    def col(k):
        return pl.BlockSpec((None, s_len, hps * LANES), lambda b, h: (b, 0, k * ng + h))

    def cw(k):
        return pl.BlockSpec((CONV_K, hps * LANES), lambda b, h: (0, k * ng + h))

    vec = pl.BlockSpec((1, LANES), lambda b, h: (0, 0))
    nc = s_len // CHUNK
    scratch = [
        pltpu.VMEM((s_len, LANES), F32), pltpu.VMEM((s_len, LANES), F32), pltpu.VMEM((s_len, LANES), F32),
        pltpu.VMEM((2, s_len, LANES), F32), pltpu.VMEM((2, s_len, LANES), F32),
        pltpu.VMEM((2, s_len, 2 * CHUNK), F32), pltpu.VMEM((hps, 2, nc * GDN_DK, GDN_DV), F32),
        pltpu.VMEM((hps, 2, nc * (GDN_DK + CHUNK), GDN_DV), BF16),
        pltpu.VMEM((hps, 2, nc * 8, LANES), F32), pltpu.VMEM((hps, s_len, GDN_DV), F32),
        pltpu.VMEM((s_len + 2 * CONV_PAD, LANES), F32), pltpu.VMEM((s_len, AB_W), F32),
    ]
    out = pl.pallas_call(
        _gdn_kernel, out_shape=jax.ShapeDtypeStruct((bsz, s_len, nh * GDN_DV), BF16), grid=(bsz, ng),
        in_specs=[col(0), col(1), col(2), col(3),
                  pl.BlockSpec((None, s_len, AB_W), lambda b, h: (b, 0, 0)),
                  cw(0), cw(1), cw(2), vec, vec, vec],
        out_specs=pl.BlockSpec((None, s_len, hps * GDN_DV), lambda b, h: (b, 0, h)),
        scratch_shapes=scratch, compiler_params=_params(("parallel", "parallel")), name="gdn",
    )(qkvz3, qkvz3, qkvz3, qkvz3, ab3, conv_w, conv_w, conv_w, alog, dtb, norm_w.reshape(1, GDN_DV))
    return out.reshape(bsz * s_len, nh * GDN_DV)


MLA_TQ = 256
MLA_KC = 1024


def _mla_kernel(cq_ref, ckv_ref, kab_ref, mq_ref, mk1_ref, mk2_ref, qnw_ref, kvnw_ref, wq_ref, wkv_ref,
                o_ref, q_s, k_s, vt_s, sta_s, stb_s, cqn_s, ckvn_s):
    s_len = cq_ref.shape[0]
    scale = (QK_NOPE + QK_ROPE) ** -0.5

    @pl.when(pl.program_id(1) == 0)
    def _():
        cqn_s[...] = _rms_norm(cq_ref[...], qnw_ref[...]).astype(BF16)
        ckvn_s[...] = _rms_norm(ckv_ref[...], kvnw_ref[...]).astype(BF16)
        kab = kab_ref[...]
        k_s[:, QK_NOPE:] = (kab[:, :LANES] * mk1_ref[...] + kab[:, LANES:] * mk2_ref[...]).astype(BF16)
        vt_s[V_HEAD:, :] = jnp.ones((vt_s.shape[0] - V_HEAD, s_len), BF16)

    q = _dot(cqn_s[...], wq_ref[...]) * scale
    q_s[:, :QK_NOPE] = q[:, :QK_NOPE].astype(BF16)
    q_s[:, QK_NOPE:] = (q[:, QK_NOPE:] * mq_ref[...]).astype(BF16)
    kv = _dot(ckvn_s[...], wkv_ref[...])
    k_s[:, :QK_NOPE] = kv[:, :QK_NOPE].astype(BF16)
    vt_s[:V_HEAD, :] = jnp.transpose(kv[:, QK_NOPE:]).astype(BF16)

    tq = min(MLA_TQ, s_len)

    nq = s_len // tq
    kc = min(MLA_KC, s_len)

    def scores(i, st):
        st[...] = _dot_nt(k_s[...], q_s[_ds(i, tq), :])

    def attend(i, st, nxt):
        m = jnp.max(st[...], 0, keepdims=True)
        ot = jnp.zeros((vt_s.shape[0], tq), F32)
        for c in range(0, s_len, kc):
            if nxt is not None:
                nxt[c:c + kc, :] = _dot_nt(k_s[c:c + kc, :], q_s[_ds(i + 1, tq), :])
            p = jnp.exp(st[c:c + kc, :] - m).astype(BF16)
            ot = ot + _dot(vt_s[:, c:c + kc], p)
        o = jnp.transpose(ot[:V_HEAD] / ot[V_HEAD:V_HEAD + 1])
        o_ref[_ds(i, tq), :] = o.astype(o_ref.dtype)

    def pair(j, prefetch):
        attend(2 * j, sta_s, stb_s)
        attend(2 * j + 1, stb_s, sta_s if prefetch else None)

    assert nq == 1 or nq % 2 == 0
    scores(0, sta_s)
    if nq == 1:
        attend(0, sta_s, None)
    else:
        def body(j, carry):
            pair(j, True)
            return carry
        lax.fori_loop(0, nq // 2 - 1, body, 0)
        pair(nq // 2 - 1, False)


def _pack_w_uq(w_uq):
    w = w_uq.reshape(Q_LORA, MLA_HEADS, QK_NOPE + QK_ROPE)
    half = QK_ROPE // 2
    r1 = w[..., QK_NOPE:QK_NOPE + half]
    r2 = w[..., QK_NOPE + half:]
    w = jnp.concatenate([w[..., :QK_NOPE], r1, r1, -r2, r2], axis=-1)
    return w.reshape(Q_LORA, MLA_HEADS * 2 * LANES).astype(BF16)


def _mla(cq, ckv, kab, tables, qnw, kvnw, wq, wkv, bsz, s_len):
    nh = MLA_HEADS

    def per_b(width):
        return pl.BlockSpec((None, s_len, width), lambda b, h: (b, 0, 0))

    def vec(width):
        return pl.BlockSpec((1, width), lambda b, h: (0, 0))

    mq, mk1, mk2 = (t.reshape(bsz, s_len, LANES) for t in tables)
    out = pl.pallas_call(
        _mla_kernel, out_shape=jax.ShapeDtypeStruct((bsz, s_len, nh * V_HEAD), BF16), grid=(bsz, nh),
        in_specs=[per_b(Q_LORA), per_b(KV_LORA), per_b(KAB_W), per_b(LANES), per_b(LANES), per_b(LANES),
                  vec(Q_LORA), vec(KV_LORA),
                  pl.BlockSpec((Q_LORA, 2 * LANES), lambda b, h: (0, h)),
                  pl.BlockSpec((KV_LORA, QK_NOPE + V_HEAD), lambda b, h: (0, h))],
        out_specs=pl.BlockSpec((None, s_len, V_HEAD), lambda b, h: (b, 0, h)),
        scratch_shapes=[pltpu.VMEM((s_len, 2 * LANES), BF16), pltpu.VMEM((s_len, 2 * LANES), BF16),
                        pltpu.VMEM((V_HEAD + 16, s_len), BF16),
                        pltpu.VMEM((s_len, min(MLA_TQ, s_len)), F32),
                        pltpu.VMEM((s_len, min(MLA_TQ, s_len)), F32),
                        pltpu.VMEM((s_len, Q_LORA), BF16), pltpu.VMEM((s_len, KV_LORA), BF16)],
        compiler_params=_params(("parallel", "arbitrary")), name="mla",
    )(cq.reshape(bsz, s_len, Q_LORA), ckv.reshape(bsz, s_len, KV_LORA), kab.reshape(bsz, s_len, KAB_W),
      mq, mk1, mk2, qnw.reshape(1, Q_LORA), kvnw.reshape(1, KV_LORA), wq, wkv)
    return out.reshape(bsz * s_len, nh * V_HEAD)


def _tail_kernel(og_ref, om_ref, gates_ref, h_ref, wog_ref, wom_ref, wout_ref, g_ref, b_ref, o_ref, ob_ref):
    d = h_ref.shape[1]
    y_gdn = _dot(og_ref[...], wog_ref[...])
    y_mla = _dot(om_ref[...], wom_ref[...])
    y = (_sigmoid(gates_ref[:, :d].astype(F32)) * y_gdn + _sigmoid(gates_ref[:, d:].astype(F32)) * y_mla)
    m = _dot(y.astype(BF16), wout_ref[...])
    out = _layer_norm(DN_ALPHA * h_ref[...] + m, g_ref[...], b_ref[...])
    o_ref[...] = out
    ob_ref[...] = out.astype(BF16)


def _mixer_tail(og, om, gates, h, wog, wom, wout, g, b, tm):
    t, d = h.shape
    row = lambda w: pl.BlockSpec((tm, w), lambda i: (i, 0))
    full = pl.BlockSpec((d, d), lambda i: (0, 0))
    vec = pl.BlockSpec((1, d), lambda i: (0, 0))
    return pl.pallas_call(
        _tail_kernel, out_shape=(jax.ShapeDtypeStruct((t, d), F32), jax.ShapeDtypeStruct((t, d), BF16)),
        grid=(t // tm,),
        in_specs=[row(d), row(d), row(2 * d), row(d), full, full, full, vec, vec],
        out_specs=(row(d), row(d)), compiler_params=_params(("parallel",)), name="mixer_tail",
    )(og, om, gates, h, wog, wom, wout, g.reshape(1, d), b.reshape(1, d))


def _router_kernel(h_ref, wh_ref, wl_ref, b_ref, eid_ref, gate_ref, cnt_ref, run):
    @pl.when(pl.program_id(0) == 0)
    def _():
        run[...] = jnp.zeros_like(run)

    x = h_ref[...]
    xh = x.astype(BF16)
    xl = (x - xh.astype(F32)).astype(BF16)
    logits = _dot(xh, wh_ref[...]) + (_dot(xl, wh_ref[...]) + _dot(xh, wl_ref[...])) + b_ref[...]
    tm = logits.shape[0]
    lane = lax.broadcasted_iota(jnp.int32, logits.shape, 1)
    lane_f = lane.astype(F32)
    neg = -jnp.inf

    def first_max(vals):
        m = jnp.max(vals, -1, keepdims=True)
        idx = jnp.min(jnp.where(vals == m, lane_f, float(LANES)), -1, keepdims=True)
        return m, idx

    is_grp = lane < N_GROUPS
    mg, grp = first_max(jnp.where(is_grp, logits, neg))
    p_grp = 1.0 / jnp.sum(jnp.where(is_grp, jnp.exp(logits - mg), 0.0), -1, keepdims=True)
    in_grp = ((lane - N_GROUPS) >> 3).astype(F32) == grp
    le = jnp.where(in_grp, logits, neg)
    l1, i1 = first_max(le)
    l2, i2 = first_max(jnp.where(lane_f == i1, neg, le))
    t = jnp.exp(l2 - l1)
    g1 = p_grp / (1.0 + t)
    g2 = p_grp * t / (1.0 + t)
    hit1 = lane_f == i1
    hit2 = lane_f == i2
    onehot = jnp.where(hit1 | hit2, 1.0, 0.0)
    ri = lax.broadcasted_iota(jnp.int32, (tm, tm), 0)
    rj = lax.broadcasted_iota(jnp.int32, (tm, tm), 1)
    before = jnp.where(ri > rj, 1.0, 0.0).astype(BF16)
    prefix = _dot(before, onehot.astype(BF16)) + run[...]
    rank1 = jnp.sum(jnp.where(hit1, prefix, 0.0), -1, keepdims=True)
    rank2 = jnp.sum(jnp.where(hit2, prefix, 0.0), -1, keepdims=True)
    run[...] += jnp.sum(onehot, 0, keepdims=True)
    cnt_ref[...] = run[...]
    ids = jnp.where(lane == 0, i1 - float(N_GROUPS), jnp.where(lane == 1, i2 - float(N_GROUPS),
                    jnp.where(lane == 2, rank1, jnp.where(lane == 3, rank2, 0.0))))
    eid_ref[...] = ids.astype(jnp.int32)
    gate_ref[...] = jnp.where(lane == 0, g1, jnp.where(lane == 1, g2, 0.0))


def _router(h, w_rg, b_rg, w_re, b_re, tm):
    t, d = h.shape
    padw = LANES - N_GROUPS - N_EXPERTS
    w = jnp.concatenate([w_rg, w_re, jnp.zeros((d, padw), F32)], axis=1)
    b = jnp.concatenate([b_rg, b_re, jnp.zeros((padw,), F32)])[None, :]
    wh = w.astype(BF16)
    wl = (w - wh.astype(F32)).astype(BF16)
    row = pl.BlockSpec((tm, LANES), lambda i: (i, 0))
    one = pl.BlockSpec((1, LANES), lambda i: (0, 0))
    wspec = pl.BlockSpec((d, LANES), lambda i: (0, 0))
    return pl.pallas_call(
        _router_kernel,
        out_shape=(jax.ShapeDtypeStruct((t, LANES), jnp.int32), jax.ShapeDtypeStruct((t, LANES), F32),
                   jax.ShapeDtypeStruct((1, LANES), F32)),
        grid=(t // tm,),
        in_specs=[pl.BlockSpec((tm, d), lambda i: (i, 0)), wspec, wspec, one],
        out_specs=(row, row, one), scratch_shapes=[pltpu.VMEM((1, LANES), F32)],
        compiler_params=_params(("arbitrary",)), name="router",
    )(h, wh, wl, b)


def _dest_kernel(ids_ref, start_ref, o_ref):
    ids = ids_ref[...]
    lane = lax.broadcasted_iota(jnp.int32, ids.shape, 1)
    start = start_ref[...]

    def row_of(k):
        hit = lane == ids[:, k:k + 1] + N_GROUPS
        return jnp.sum(jnp.where(hit, start, 0.0), -1, keepdims=True).astype(jnp.int32) + ids[:, TOP_K + k:TOP_K + k + 1]

    o_ref[...] = jnp.where(lane == 0, row_of(0), jnp.where(lane == 1, row_of(1), 0))


def _dest_rows(ids, pad_start, tm):
    t = ids.shape[0]
    start = jnp.zeros((1, LANES), F32).at[0, N_GROUPS:N_GROUPS + N_EXPERTS].set(pad_start.astype(F32))
    row = pl.BlockSpec((tm, LANES), lambda i: (i, 0))
    return pl.pallas_call(
        _dest_kernel, out_shape=jax.ShapeDtypeStruct((t, LANES), jnp.int32), grid=(t // tm,),
        in_specs=[row, pl.BlockSpec((1, LANES), lambda i: (0, 0))], out_specs=row,
        compiler_params=_params(("parallel",)), name="dest_rows",
    )(ids, start)


def _expert_kernel(layer, be_ref, nx_ref, nu_ref, x_ref, wg_hbm, wu_hbm, wd_hbm, o_ref,
                   wgb, wub, wdb, wgs, wus, wds, sem):
    i = pl.program_id(0)

    def weight_copies(e):
        return [pltpu.make_async_copy(hbm.at[layer, e], stage, sem.at[n])
                for n, (hbm, stage) in enumerate(((wg_hbm, wgs), (wu_hbm, wus), (wd_hbm, wds)))]

    @pl.when(i == 0)
    def _():
        for c in weight_copies(be_ref[0]):
            c.start()

    @pl.when((i == 0) | (be_ref[i] != be_ref[jnp.maximum(i - 1, 0)]))
    def _():
        for c in weight_copies(be_ref[i]):
            c.wait()
        wgb[...] = wgs[...].astype(BF16)
        wub[...] = wus[...].astype(BF16)
        wdb[...] = wds[...].astype(BF16)

        @pl.when(nx_ref[i] >= 0)
        def _():
            for c in weight_copies(nx_ref[i]):
                c.start()

    @pl.when(i < nu_ref[0])
    def _():
        x = x_ref[...]
        hid = _silu(_dot(x, wgb[...])) * _dot(x, wub[...])
        o_ref[...] = _dot(hid.astype(BF16), wdb[...]).astype(o_ref.dtype)

    @pl.when(i >= nu_ref[0])
    def _():
        o_ref[...] = jnp.zeros_like(o_ref)


def _experts(xb, blk_expert, nxt_expert, n_used, w_gate, w_up, w_down, layer):
    n_rows, d = xb.shape
    n_blocks = n_rows // MOE_BLOCK
    hbm = pl.BlockSpec(memory_space=pl.ANY)
    grid_spec = pltpu.PrefetchScalarGridSpec(
        num_scalar_prefetch=3, grid=(n_blocks,),
        in_specs=[pl.BlockSpec((MOE_BLOCK, d), lambda i, be, nx, nu: (i, 0)), hbm, hbm, hbm],
        out_specs=pl.BlockSpec((MOE_BLOCK, d), lambda i, be, nx, nu: (i, 0)),
        scratch_shapes=[pltpu.VMEM((d, D_EXPERT), BF16), pltpu.VMEM((d, D_EXPERT), BF16),
                        pltpu.VMEM((D_EXPERT, d), BF16),
                        pltpu.VMEM((d, D_EXPERT), F32), pltpu.VMEM((d, D_EXPERT), F32),
                        pltpu.VMEM((D_EXPERT, d), F32), pltpu.SemaphoreType.DMA((3,))])
    return pl.pallas_call(
        functools.partial(_expert_kernel, layer), out_shape=jax.ShapeDtypeStruct((n_rows, d), BF16),
        grid_spec=grid_spec, compiler_params=_params(("arbitrary",)), name="experts",
    )(blk_expert, nxt_expert, n_used, xb, w_gate, w_up, w_down)


def _combine_kernel(h_ref, y1_ref, y2_ref, gate_ref, g_ref, b_ref, o_ref):
    f = gate_ref[:, 0:1] * y1_ref[...].astype(F32) + gate_ref[:, 1:2] * y2_ref[...].astype(F32)
    o_ref[...] = _layer_norm(DN_ALPHA * h_ref[...] + f, g_ref[...], b_ref[...])


def _combine(h, y1, y2, gate, g, b, tm):
    t, d = h.shape
    row = pl.BlockSpec((tm, d), lambda i: (i, 0))
    vec = pl.BlockSpec((1, d), lambda i: (0, 0))
    return pl.pallas_call(
        _combine_kernel, out_shape=jax.ShapeDtypeStruct((t, d), F32), grid=(t // tm,),
        in_specs=[row, row, row, pl.BlockSpec((tm, LANES), lambda i: (i, 0)), vec, vec], out_specs=row,
        compiler_params=_params(("parallel",)), name="combine_ln",
    )(h, y1, y2, gate, g.reshape(1, d), b.reshape(1, d))


def _dispatch(ids, cnt, tm):
    t = ids.shape[0]
    n_assign = t * TOP_K
    counts = cnt[0, N_GROUPS:N_GROUPS + N_EXPERTS].astype(jnp.int32)
    padded = (counts + MOE_BLOCK - 1) // MOE_BLOCK * MOE_BLOCK
    pad_end = jnp.cumsum(padded)
    pad_start = pad_end - padded
    start = jnp.cumsum(counts) - counts
    eid = ids[:, :TOP_K]
    dest = _dest_rows(ids, pad_start, tm)[:, :TOP_K]
    order = jnp.argsort(eid.reshape(n_assign))
    n_blocks = -(-(n_assign + N_EXPERTS * (MOE_BLOCK - 1)) // MOE_BLOCK)
    blk_first = (jnp.arange(n_blocks) * MOE_BLOCK)[:, None]
    ended = jnp.dot((pad_end[None, :] <= blk_first).astype(F32), jnp.ones((N_EXPERTS,), F32))
    experts = jnp.arange(N_EXPERTS)
    last_used = jnp.max(jnp.where(counts > 0, experts, 0))
    blk_expert = jnp.minimum(ended.astype(jnp.int32), last_used)
    later = (experts[None, :] > experts[:, None]) & (counts[None, :] > 0)
    nxt = jnp.min(jnp.where(later, experts[None, :], N_EXPERTS), axis=1)
    nxt = jnp.where(nxt == N_EXPERTS, -1, nxt)
    onehot = (blk_expert[:, None] == experts[None, :]).astype(F32)
    table = jnp.stack([pad_start - start, start + counts, nxt], axis=1).astype(F32)
    looked = jnp.dot(onehot, table, precision=lax.Precision.HIGHEST).astype(jnp.int32)
    shift = jnp.repeat(looked[:, 0], MOE_BLOCK)
    end = jnp.repeat(looked[:, 1], MOE_BLOCK)
    nxt_expert = looked[:, 2]
    rows = jnp.arange(n_blocks * MOE_BLOCK)
    src = rows - shift
    row_tok = jnp.where(src < end, jnp.take(order, jnp.clip(src, 0, n_assign - 1), mode="clip") // TOP_K,
                        rows % t).astype(jnp.int32)
    n_used = (pad_end[-1:] // MOE_BLOCK).astype(jnp.int32)
    return row_tok, dest, blk_expert, nxt_expert, n_used


def _moe(h, hb, w_rg, b_rg, w_re, b_re, w_gate, w_up, w_down, layer, g, b, tm):
    ids, gate, cnt = _router(h, w_rg, b_rg, w_re, b_re, tm)
    row_tok, dest, blk_expert, nxt_expert, n_used = _dispatch(ids, cnt, tm)
    xb = jnp.take(hb, row_tok, axis=0, mode="clip")
    yb = _experts(xb, blk_expert, nxt_expert, n_used, w_gate, w_up, w_down, layer)
    y1 = jnp.take(yb, dest[:, 0], axis=0, mode="clip")
    y2 = jnp.take(yb, dest[:, 1], axis=0, mode="clip")
    return _combine(h, y1, y2, gate, g, b, min(ROWS_MEMBOUND, h.shape[0]))


def kernel(x, positions, ln_in_g, ln_in_b, w_in, conv_w, a_log, dt_bias, gdn_norm_w, w_o_gdn, mla_q_norm_w, w_uq, mla_kv_norm_w, w_ukv, w_o_mla, w_out, ln1_g, ln1_b, w_router_group, b_router_group, w_router_expert, b_router_expert, w_gate, w_up, w_down, ln2_g, ln2_b):
    bsz, s_len, d = x.shape
    t = bsz * s_len
    tm = min(ROWS_MATMUL, t)
    tables = _rope_tables(positions, min(ROWS_MEMBOUND, t))
    h = _ln_in(x.reshape(t, d), ln_in_g, ln_in_b, min(ROWS_MEMBOUND, t))
    for l in range(DEPTH):
        qkvz, cq, ckv, kab, ab, gates = _proj(h, _pack_w_in(w_in[l]), tm)
        og = _gdn(qkvz, ab, conv_w[l], a_log[l], dt_bias[l], gdn_norm_w[l], bsz, s_len)
        om = _mla(cq, ckv, kab, tables, mla_q_norm_w[l], mla_kv_norm_w[l],
                  _pack_w_uq(w_uq[l]), w_ukv[l].astype(BF16), bsz, s_len)
        h, hb = _mixer_tail(og, om, gates, h, w_o_gdn[l].astype(BF16), w_o_mla[l].astype(BF16),
                            w_out[l].astype(BF16), ln1_g[l], ln1_b[l], tm)
        h = _moe(h, hb, w_router_group[l], b_router_group[l], w_router_expert[l], b_router_expert[l],
                 w_gate, w_up, w_down, l, ln2_g[l], ln2_b[l], tm)
    return h.reshape(bsz, s_len, d)
```

```python
import functools

import numpy as np
import jax
import jax.numpy as jnp
from jax import lax
from jax.experimental import pallas as pl
from jax.experimental.pallas import tpu as pltpu

F32 = jnp.float32
BF16 = jnp.bfloat16

D_MODEL = 1024
DEPTH = 2
GDN_HEADS = 8
GDN_DK = 128
GDN_DV = 128
CONV_K = 5
CHUNK = 64
MLA_HEADS = 8
Q_LORA = 384
KV_LORA = 256
QK_NOPE = 128
QK_ROPE = 64
V_HEAD = 128
ROPE_BASE = 10000.0
N_GROUPS = 8
EXPERTS_PER_GROUP = 8
N_EXPERTS = N_GROUPS * EXPERTS_PER_GROUP
TOP_K = 2
D_EXPERT = 512
MOE_BLOCK = 256
DN_ALPHA = (2 * DEPTH) ** 0.25
LN_EPS = 1e-5
RMS_EPS = 1e-6

ROWS_MATMUL = 256
ROWS_MEMBOUND = 512
ROWS_TAIL = 512
ROWS_ROUTER = 512
LANES = 128
VMEM_LIMIT = 56 * 1024 * 1024

QKVZ_W = 4 * GDN_HEADS * GDN_DK
KAB_W = 2 * LANES
AB_W = LANES
GATES_W = 2 * D_MODEL
PROJ_SEGS = (QKVZ_W, Q_LORA, KV_LORA, KAB_W, AB_W, GATES_W)
PROJ_W = sum(PROJ_SEGS)


def _params(sem):
    return pltpu.CompilerParams(dimension_semantics=sem, vmem_limit_bytes=VMEM_LIMIT)


def _layer_norm(x, g, b):
    mu = jnp.mean(x, -1, keepdims=True)
    xc = x - mu
    var = jnp.mean(xc * xc, -1, keepdims=True)
    return xc * lax.rsqrt(var + LN_EPS) * g + b


def _rms_norm(x, w):
    return x * lax.rsqrt(jnp.mean(x * x, -1, keepdims=True) + RMS_EPS) * w


def _silu(x):
    return (0.5 * x) * (1.0 + jnp.tanh(0.5 * x))


def _sigmoid(x):
    return 1.0 / (1.0 + jnp.exp(-x))


def _dot(a, b):
    return jnp.dot(a, b, preferred_element_type=F32)


def _dot_nt(a, b):
    return lax.dot_general(a, b, (((1,), (1,)), ((), ())), preferred_element_type=F32)


def _ds(i, n):
    return pl.ds(i * n, n) if isinstance(i, int) else pl.ds(pl.multiple_of(i * n, n), n)


def _dot_tn(a, b):
    return lax.dot_general(a, b, (((0,), (0,)), ((), ())), preferred_element_type=F32)


def _rope_kernel(pos_ref, freq_ref, mq_ref, mk1_ref, mk2_ref):
    ang = pos_ref[...].astype(F32) * freq_ref[...]
    c = jnp.cos(ang)
    s = jnp.sin(ang)
    quarter = lax.broadcasted_iota(jnp.int32, ang.shape, 1) >> 5
    even = (quarter & 1) == 0
    mq_ref[...] = jnp.where((quarter == 0) | (quarter == 3), c, s)
    mk1_ref[...] = jnp.where(even, c, s)
    mk2_ref[...] = jnp.where(even, s, c)


def _rope_tables(positions, tm):
    t = positions.size
    half = QK_ROPE // 2
    inv_freq = jnp.power(ROPE_BASE, -jnp.arange(half, dtype=F32) / half)
    freq = jnp.tile(inv_freq, LANES // half)[None, :]
    out = jax.ShapeDtypeStruct((t, LANES), F32)
    row = pl.BlockSpec((tm, LANES), lambda i: (i, 0))
    return pl.pallas_call(
        _rope_kernel, out_shape=(out, out, out), grid=(t // tm,),
        in_specs=[pl.BlockSpec((tm, 1), lambda i: (i, 0)), pl.BlockSpec((1, LANES), lambda i: (0, 0))],
        out_specs=(row, row, row), compiler_params=_params(("parallel",)), name="rope_tables",
    )(positions.reshape(t, 1), freq)


def _ln_kernel(x_ref, g_ref, b_ref, o_ref):
    o_ref[...] = _layer_norm(x_ref[...], g_ref[...], b_ref[...])


def _ln_in(x2, g, b, tm):
    t, d = x2.shape
    row = pl.BlockSpec((tm, d), lambda i: (i, 0))
    vec = pl.BlockSpec((1, d), lambda i: (0, 0))
    return pl.pallas_call(
        _ln_kernel, out_shape=jax.ShapeDtypeStruct((t, d), F32), grid=(t // tm,),
        in_specs=[row, vec, vec], out_specs=row, compiler_params=_params(("parallel",)), name="ln_in",
    )(x2, g.reshape(1, d), b.reshape(1, d))


PROJ_CHUNK = 512


def _proj_kernel(h_ref, w_ref, *out_refs):
    a = h_ref[...].astype(BF16)
    off = 0
    for ref, width in zip(out_refs, PROJ_SEGS):
        for c in range(0, width, PROJ_CHUNK):
            wc = min(PROJ_CHUNK, width - c)
            ref[:, c:c + wc] = _dot(a, w_ref[:, off + c:off + c + wc]).astype(ref.dtype)
        off += width


def _proj(h, w_all, tm):
    t, d = h.shape
    dts = (BF16, F32, F32, F32, F32, BF16)
    outs = tuple(jax.ShapeDtypeStruct((t, w), dt) for w, dt in zip(PROJ_SEGS, dts))
    return pl.pallas_call(
        _proj_kernel, out_shape=outs, grid=(t // tm,),
        in_specs=[pl.BlockSpec((tm, d), lambda i: (i, 0)), pl.BlockSpec((d, PROJ_W), lambda i: (0, 0))],
        out_specs=tuple(pl.BlockSpec((tm, w), lambda i: (i, 0)) for w in PROJ_SEGS),
        compiler_params=_params(("parallel",)), name="in_proj",
    )(h, w_all)


def _pack_w_in(w_in):
    hq = GDN_HEADS * GDN_DK
    o_a = 4 * hq
    o_bt = o_a + 2 * GDN_HEADS
    o_cq = o_bt + 2 * GDN_HEADS
    o_ckv = o_cq + Q_LORA
    o_kr = o_ckv + KV_LORA
    o_g = o_kr + QK_ROPE
    half = QK_ROPE // 2
    k1 = w_in[:, o_kr:o_kr + half]
    k2 = w_in[:, o_kr + half:o_g]
    d = w_in.shape[0]
    ab = jnp.concatenate([w_in[:, o_a:o_cq], jnp.zeros((d, AB_W - 4 * GDN_HEADS), w_in.dtype)], 1)
    return jnp.concatenate(
        [w_in[:, :o_a], w_in[:, o_cq:o_ckv], w_in[:, o_ckv:o_kr],
         k1, k1, k1, k1, -k2, k2, -k2, k2, ab, w_in[:, o_g:]], axis=1).astype(BF16)


GDN_HPS = 2
CONV_PAD = 8
CUM_CHUNKS = 8
PREP_CHUNKS = 16


def _gdn_kernel(q_ref, k_ref, v_ref, z_ref, ab_ref, cwq_ref, cwk_ref, cwv_ref, alog_ref, dtb_ref, nw_ref,
                o_ref, qs, ks, vs, gs, bs, brs, bns, mqs, ges, os_, xp, gates_s):
    s_len = q_ref.shape[0]
    nc = s_len // CHUNK
    first_head = pl.program_id(1) * GDN_HPS

    xp[:CONV_PAD, :] = jnp.zeros((CONV_PAD, LANES), F32)
    xp[CONV_PAD + s_len:, :] = jnp.zeros((CONV_PAD, LANES), F32)

    def conv_silu(x_ref, cw_ref, cols):
        xp[CONV_PAD:CONV_PAD + s_len, :] = x_ref[:, cols].astype(F32)
        acc = None
        for j in range(CONV_K):
            term = xp[CONV_PAD + j - CONV_K // 2:CONV_PAD + j - CONV_K // 2 + s_len, :] * cw_ref[j:j + 1, cols]
            acc = term if acc is None else acc + term
        return _silu(acc)

    def l2n(x, scale=1.0):
        return x * (lax.rsqrt(jnp.sum(x * x, -1, keepdims=True) + RMS_EPS) * scale)

    lane = lax.broadcasted_iota(jnp.int32, (1, LANES), 1)

    def pick(vals, idx):
        col = jnp.sum(jnp.where(lane == idx, vals, 0.0), -1, keepdims=True)
        return jnp.broadcast_to(col, (s_len, LANES))

    def stage(hh):
        cols = slice(hh * LANES, (hh + 1) * LANES)
        qs[...] = l2n(conv_silu(q_ref, cwq_ref, cols), GDN_DK ** -0.5)
        ks[...] = l2n(conv_silu(k_ref, cwk_ref, cols))
        vs[...] = conv_silu(v_ref, cwv_ref, cols)
        gates = gates_s[...]
        for d in range(2):
            gs[d] = pick(gates, first_head + hh + d * GDN_HEADS)
            bs[d] = pick(gates, first_head + hh + (2 + d) * GDN_HEADS)

    ab = ab_ref[...]
    x = ab + dtb_ref[...]
    softplus = jnp.maximum(x, 0.0) + jnp.log(1.0 + jnp.exp(-jnp.abs(x)))
    gates_s[...] = jnp.where(lane < 2 * GDN_HEADS, -jnp.exp(alog_ref[...]) * softplus, _sigmoid(ab))

    ci = lax.broadcasted_iota(jnp.int32, (CHUNK, 2 * CHUNK), 0)
    cj2 = lax.broadcasted_iota(jnp.int32, (CHUNK, 2 * CHUNK), 1)
    right = cj2 >= CHUNK
    cj = cj2 & (CHUNK - 1)
    eye = jnp.where(ci == cj, 1.0, 0.0)
    incl = (ci >= cj, ci <= cj)
    strict = (ci > cj, ci < cj)
    tri3 = tuple(jnp.concatenate([jnp.where(m[:, :CHUNK], 1.0, 0.0).astype(BF16)] * 3, axis=1) for m in incl)
    ones3 = jnp.ones((CHUNK, 3 * CHUNK), BF16)
    last_row = (CHUNK - 1, 0)

    unroll = CUM_CHUNKS if nc % CUM_CHUNKS == 0 else 1

    def cumulate(c, carry):
        rows = [pl.ds(pl.multiple_of((c * unroll + j) * CHUNK, CHUNK), CHUNK) for j in range(unroll)]
        chains = [(j, d) for j in range(unroll) for d in range(2)]
        def split3(x):
            x1 = x.astype(BF16)
            r1 = x - x1.astype(F32)
            x2 = r1.astype(BF16)
            return jnp.concatenate([x1, x2, (r1 - x2.astype(F32)).astype(BF16)], axis=0)

        parts = [split3(gs[d, rows[j], :]) for j, d in chains]
        b = [_dot(tri3[d], p) for (j, d), p in zip(chains, parts)]
        diag = [split3(x * eye) for x in b]
        b_row = [_dot(ones3, x) for x in diag]
        for (j, d), x, y in zip(chains, b, b_row):
            gs[d, rows[j], :] = x
            brs[d, rows[j], :] = y
        return carry

    def prep(hh, it, carry):
        unroll = PREP_CHUNKS if nc % PREP_CHUNKS == 0 else 1
        cs = [it * unroll + j for j in range(unroll)]
        rows = [pl.ds(pl.multiple_of(c * CHUNK, CHUNK), CHUNK) for c in cs]
        q = [qs[r, :] for r in rows]
        k = [ks[r, :] for r in rows]
        kb = [x.astype(BF16) for x in k]
        kk = [_dot_nt(x, jnp.concatenate([x, x], axis=0)) for x in kb]
        qk = [_dot_nt(x.astype(BF16), y) for x, y in zip(q, kb)]
        chains = [(j, d) for j in range(unroll) for d in range(2)]
        b = [gs[d, rows[j], :] for j, d in chains]
        beta = [bs[d, rows[j], :] for j, d in chains]
        gamma = [jnp.exp(jnp.where(incl[d], b[n] - brs[d, rows[j], :], -jnp.inf))
                 for n, (j, d) in enumerate(chains)]
        pt = [jnp.where(right, eye, jnp.where(strict[d], -beta[n] * kk[j] * gamma[n], 0.0))
              for n, (j, d) in enumerate(chains)]
        for _ in range(int(np.log2(CHUNK))):
            ptb = [x.astype(BF16) for x in pt]
            prod = [_dot(x[:, :CHUNK], x) for x in ptb]
            pt = [y + jnp.where(right, x, 0.0) for x, y in zip(pt, prod)]
        eb = [jnp.exp(x) for x in b]
        rhs = [jnp.concatenate([jnp.zeros((CHUNK, GDN_DV + GDN_DK), F32), jnp.concatenate(
                   [beta[n] * vs[rows[j], :], beta[n] * eb[n] * k[j]], axis=1)], axis=0).astype(BF16)
               for n, (j, d) in enumerate(chains)]
        wk = [_dot(t.astype(BF16), r).astype(BF16) for t, r in zip(pt, rhs)]
        b_last = [b[n][last_row[d]:last_row[d] + 1, :] for n, (j, d) in enumerate(chains)]
        kdt = [jnp.transpose(k[j] * jnp.exp(b_last[n] - b[n])).astype(BF16)
               for n, (j, d) in enumerate(chains)]
        bm = [_dot(x, y) for x, y in zip(kdt, wk)]
        pw = [_dot((qk[j] * gamma[n][:, :CHUNK]).astype(BF16), wk[n])
              for n, (j, d) in enumerate(chains)]
        for n, (j, d) in enumerate(chains):
            c = cs[j]
            bns[hh, d, pl.ds(pl.multiple_of(c * GDN_DK, GDN_DK), GDN_DK), :] = bm[n][:, :GDN_DV]
            r3 = pl.multiple_of(c * (GDN_DK + CHUNK), CHUNK)
            mqs[hh, d, pl.ds(r3, GDN_DK), :] = bm[n][:, GDN_DV:].astype(BF16)
            mqs[hh, d, pl.ds(r3 + GDN_DK, CHUNK), :] = (q[j] * eb[n] - pw[n][:, GDN_DV:]).astype(BF16)
            ges[hh, d, pl.ds(pl.multiple_of(c * 8, 8), 8), :] = jnp.broadcast_to(jnp.exp(b_last[n]), (8, LANES))
        for j in range(unroll):
            os_[hh, rows[j], :] = pw[2 * j][:, :GDN_DV] + pw[2 * j + 1][:, :GDN_DV]
        return carry

    for hh in range(GDN_HPS):
        stage(hh)
        lax.fori_loop(0, nc // unroll, cumulate, 0)
        lax.fori_loop(0, nc // (PREP_CHUNKS if nc % PREP_CHUNKS == 0 else 1), functools.partial(prep, hh), 0)

    chains = [(hh, d) for hh in range(GDN_HPS) for d in range(2)]

    def step(i, states):
        cs = (i, nc - 1 - i)
        sb = [x.astype(BF16) for x in states]
        r = [_dot(mqs[hh, d, pl.ds(pl.multiple_of(cs[d] * (GDN_DK + CHUNK), CHUNK), GDN_DK + CHUNK), :], sb[n])
             for n, (hh, d) in enumerate(chains)]
        new = []
        for n, (hh, d) in enumerate(chains):
            c = cs[d]
            os_[hh, pl.ds(pl.multiple_of(c * CHUNK, CHUNK), CHUNK), :] += r[n][GDN_DK:]
            ge = ges[hh, d, pl.ds(pl.multiple_of(c * 8, 8), 1), :]
            bn = bns[hh, d, pl.ds(pl.multiple_of(c * GDN_DK, GDN_DK), GDN_DK), :]
            new.append(ge * states[n] - r[n][:GDN_DK] + bn)
        return tuple(new)

    s0 = jnp.zeros((GDN_DK, GDN_DV), F32)
    lax.fori_loop(0, nc, step, (s0,) * len(chains))

    for hh in range(GDN_HPS):
        cols = slice(hh * LANES, (hh + 1) * LANES)
        o_ref[:, cols] = (_rms_norm(os_[hh], nw_ref[...]) * _silu(z_ref[:, cols].astype(F32))).astype(o_ref.dtype)


def _gdn(qkvz, ab, conv_w, a_log, dt_bias, norm_w, bsz, s_len):
    nh = GDN_HEADS
    qkvz3 = qkvz.reshape(bsz, s_len, QKVZ_W)
    ab3 = ab.reshape(bsz, s_len, AB_W)
    pad = jnp.zeros((AB_W - 2 * nh,), F32)
    alog = jnp.concatenate([a_log.reshape(-1), pad])[None, :]
    dtb = jnp.concatenate([dt_bias.reshape(-1), pad])[None, :]

    hps = GDN_HPS
    ng = nh // hps

    def col(k):
        return pl.BlockSpec((None, s_len, hps * LANES), lambda b, h: (b, 0, k * ng + h))

    def cw(k):
        return pl.BlockSpec((CONV_K, hps * LANES), lambda b, h: (0, k * ng + h))

    vec = pl.BlockSpec((1, LANES), lambda b, h: (0, 0))
    nc = s_len // CHUNK
    scratch = [
        pltpu.VMEM((s_len, LANES), F32), pltpu.VMEM((s_len, LANES), F32), pltpu.VMEM((s_len, LANES), F32),
        pltpu.VMEM((2, s_len, LANES), F32), pltpu.VMEM((2, s_len, LANES), F32),
        pltpu.VMEM((2, s_len, 2 * CHUNK), F32), pltpu.VMEM((hps, 2, nc * GDN_DK, GDN_DV), F32),
        pltpu.VMEM((hps, 2, nc * (GDN_DK + CHUNK), GDN_DV), BF16),
        pltpu.VMEM((hps, 2, nc * 8, LANES), F32), pltpu.VMEM((hps, s_len, GDN_DV), F32),
        pltpu.VMEM((s_len + 2 * CONV_PAD, LANES), F32), pltpu.VMEM((s_len, AB_W), F32),
    ]
    out = pl.pallas_call(
        _gdn_kernel, out_shape=jax.ShapeDtypeStruct((bsz, s_len, nh * GDN_DV), BF16), grid=(bsz, ng),
        in_specs=[col(0), col(1), col(2), col(3),
                  pl.BlockSpec((None, s_len, AB_W), lambda b, h: (b, 0, 0)),
                  cw(0), cw(1), cw(2), vec, vec, vec],
        out_specs=pl.BlockSpec((None, s_len, hps * GDN_DV), lambda b, h: (b, 0, h)),
        scratch_shapes=scratch, compiler_params=_params(("parallel", "parallel")), name="gdn",
    )(qkvz3, qkvz3, qkvz3, qkvz3, ab3, conv_w, conv_w, conv_w, alog, dtb, norm_w.reshape(1, GDN_DV))
    return out.reshape(bsz * s_len, nh * GDN_DV)


MLA_TQ = 256
MLA_KC = 1024


def _mla_kernel(cq_ref, ckv_ref, kab_ref, mq_ref, mk1_ref, mk2_ref, qnw_ref, kvnw_ref, wq_ref, wkv_ref,
                o_ref, q_s, k_s, vt_s, sta_s, stb_s, cqn_s, ckvn_s):
    s_len = cq_ref.shape[0]
    scale = (QK_NOPE + QK_ROPE) ** -0.5

    @pl.when(pl.program_id(1) == 0)
    def _():
        cqn_s[...] = _rms_norm(cq_ref[...], qnw_ref[...]).astype(BF16)
        ckvn_s[...] = _rms_norm(ckv_ref[...], kvnw_ref[...]).astype(BF16)
        kab = kab_ref[...]
        k_s[:, QK_NOPE:] = (kab[:, :LANES] * mk1_ref[...] + kab[:, LANES:] * mk2_ref[...]).astype(BF16)
        vt_s[V_HEAD:, :] = jnp.ones((vt_s.shape[0] - V_HEAD, s_len), BF16)

    q = _dot(cqn_s[...], wq_ref[...]) * scale
    q_s[:, :QK_NOPE] = q[:, :QK_NOPE].astype(BF16)
    q_s[:, QK_NOPE:] = (q[:, QK_NOPE:] * mq_ref[...]).astype(BF16)
    kv = _dot(ckvn_s[...], wkv_ref[...])
    k_s[:, :QK_NOPE] = kv[:, :QK_NOPE].astype(BF16)
    vt_s[:V_HEAD, :] = jnp.transpose(kv[:, QK_NOPE:]).astype(BF16)

    tq = min(MLA_TQ, s_len)

    nq = s_len // tq
    kc = min(MLA_KC, s_len)

    def scores(i, st):
        st[...] = _dot_nt(k_s[...], q_s[_ds(i, tq), :])

    def attend(i, st, nxt):
        m = jnp.max(st[...], 0, keepdims=True)
        ot = jnp.zeros((vt_s.shape[0], tq), F32)
        for c in range(0, s_len, kc):
            if nxt is not None:
                nxt[c:c + kc, :] = _dot_nt(k_s[c:c + kc, :], q_s[_ds(i + 1, tq), :])
            p = jnp.exp(st[c:c + kc, :] - m).astype(BF16)
            ot = ot + _dot(vt_s[:, c:c + kc], p)
        o = jnp.transpose(ot[:V_HEAD] / ot[V_HEAD:V_HEAD + 1])
        o_ref[_ds(i, tq), :] = o.astype(o_ref.dtype)

    def pair(j, prefetch):
        attend(2 * j, sta_s, stb_s)
        attend(2 * j + 1, stb_s, sta_s if prefetch else None)

    assert nq == 1 or nq % 2 == 0
    scores(0, sta_s)
    if nq == 1:
        attend(0, sta_s, None)
    else:
        def body(j, carry):
            pair(j, True)
            return carry
        lax.fori_loop(0, nq // 2 - 1, body, 0)
        pair(nq // 2 - 1, False)


def _pack_w_uq(w_uq):
    w = w_uq.reshape(Q_LORA, MLA_HEADS, QK_NOPE + QK_ROPE)
    half = QK_ROPE // 2
    r1 = w[..., QK_NOPE:QK_NOPE + half]
    r2 = w[..., QK_NOPE + half:]
    w = jnp.concatenate([w[..., :QK_NOPE], r1, r1, -r2, r2], axis=-1)
    return w.reshape(Q_LORA, MLA_HEADS * 2 * LANES).astype(BF16)


def _mla(cq, ckv, kab, tables, qnw, kvnw, wq, wkv, bsz, s_len):
    nh = MLA_HEADS

    def per_b(width):
        return pl.BlockSpec((None, s_len, width), lambda b, h: (b, 0, 0))

    def vec(width):
        return pl.BlockSpec((1, width), lambda b, h: (0, 0))

    mq, mk1, mk2 = (t.reshape(bsz, s_len, LANES) for t in tables)
    out = pl.pallas_call(
        _mla_kernel, out_shape=jax.ShapeDtypeStruct((bsz, s_len, nh * V_HEAD), BF16), grid=(bsz, nh),
        in_specs=[per_b(Q_LORA), per_b(KV_LORA), per_b(KAB_W), per_b(LANES), per_b(LANES), per_b(LANES),
                  vec(Q_LORA), vec(KV_LORA),
                  pl.BlockSpec((Q_LORA, 2 * LANES), lambda b, h: (0, h)),
                  pl.BlockSpec((KV_LORA, QK_NOPE + V_HEAD), lambda b, h: (0, h))],
        out_specs=pl.BlockSpec((None, s_len, V_HEAD), lambda b, h: (b, 0, h)),
        scratch_shapes=[pltpu.VMEM((s_len, 2 * LANES), BF16), pltpu.VMEM((s_len, 2 * LANES), BF16),
                        pltpu.VMEM((V_HEAD + 16, s_len), BF16),
                        pltpu.VMEM((s_len, min(MLA_TQ, s_len)), F32),
                        pltpu.VMEM((s_len, min(MLA_TQ, s_len)), F32),
                        pltpu.VMEM((s_len, Q_LORA), BF16), pltpu.VMEM((s_len, KV_LORA), BF16)],
        compiler_params=_params(("parallel", "arbitrary")), name="mla",
    )(cq.reshape(bsz, s_len, Q_LORA), ckv.reshape(bsz, s_len, KV_LORA), kab.reshape(bsz, s_len, KAB_W),
      mq, mk1, mk2, qnw.reshape(1, Q_LORA), kvnw.reshape(1, KV_LORA), wq, wkv)
    return out.reshape(bsz * s_len, nh * V_HEAD)


def _tail_kernel(og_ref, om_ref, gates_ref, h_ref, wog_ref, wom_ref, wout_ref, g_ref, b_ref, o_ref, ob_ref):
    tm, d = h_ref.shape
    parts = [slice(k * (tm // 2), (k + 1) * (tm // 2)) for k in range(2)] if tm % 32 == 0 else [slice(0, tm)]
    y_gdn = [_dot(og_ref[r, :], wog_ref[...]) for r in parts]
    y_mla = [_dot(om_ref[r, :], wom_ref[...]) for r in parts]
    m = []
    for r, yg, ym in zip(parts, y_gdn, y_mla):
        y = _sigmoid(gates_ref[r, :d].astype(F32)) * yg + _sigmoid(gates_ref[r, d:].astype(F32)) * ym
        m.append(_dot(y.astype(BF16), wout_ref[...]))
    for r, mk in zip(parts, m):
        out = _layer_norm(DN_ALPHA * h_ref[r, :] + mk, g_ref[...], b_ref[...])
        o_ref[r, :] = out
        ob_ref[r, :] = out.astype(BF16)


def _mixer_tail(og, om, gates, h, wog, wom, wout, g, b, tm):
    t, d = h.shape
    row = lambda w: pl.BlockSpec((tm, w), lambda i: (i, 0))
    full = pl.BlockSpec((d, d), lambda i: (0, 0))
    vec = pl.BlockSpec((1, d), lambda i: (0, 0))
    return pl.pallas_call(
        _tail_kernel, out_shape=(jax.ShapeDtypeStruct((t, d), F32), jax.ShapeDtypeStruct((t, d), BF16)),
        grid=(t // tm,),
        in_specs=[row(d), row(d), row(2 * d), row(d), full, full, full, vec, vec],
        out_specs=(row(d), row(d)), compiler_params=_params(("parallel",)), name="mixer_tail",
    )(og, om, gates, h, wog, wom, wout, g.reshape(1, d), b.reshape(1, d))


def _router_kernel(h_ref, wh_ref, wl_ref, b_ref, eid_ref, gate_ref, cnt_ref, run):
    @pl.when(pl.program_id(0) == 0)
    def _():
        run[...] = jnp.zeros_like(run)

    x = h_ref[...]
    xh = x.astype(BF16)
    xl = (x - xh.astype(F32)).astype(BF16)
    logits = _dot(xh, wh_ref[...]) + (_dot(xl, wh_ref[...]) + _dot(xh, wl_ref[...])) + b_ref[...]
    tm = logits.shape[0]
    lane = lax.broadcasted_iota(jnp.int32, logits.shape, 1)
    lane_f = lane.astype(F32)
    neg = -jnp.inf

    def first_max(vals):
        m = jnp.max(vals, -1, keepdims=True)
        idx = jnp.min(jnp.where(vals == m, lane_f, float(LANES)), -1, keepdims=True)
        return m, idx

    is_grp = lane < N_GROUPS
    mg, grp = first_max(jnp.where(is_grp, logits, neg))
    p_grp = 1.0 / jnp.sum(jnp.where(is_grp, jnp.exp(logits - mg), 0.0), -1, keepdims=True)
    in_grp = ((lane - N_GROUPS) >> 3).astype(F32) == grp
    le = jnp.where(in_grp, logits, neg)
    l1, i1 = first_max(le)
    l2, i2 = first_max(jnp.where(lane_f == i1, neg, le))
    t = jnp.exp(l2 - l1)
    g1 = p_grp / (1.0 + t)
    g2 = p_grp * t / (1.0 + t)
    hit1 = lane_f == i1
    hit2 = lane_f == i2
    onehot = jnp.where(hit1 | hit2, 1.0, 0.0)
    ri = lax.broadcasted_iota(jnp.int32, (tm, tm), 0)
    rj = lax.broadcasted_iota(jnp.int32, (tm, tm), 1)
    before = jnp.where(ri > rj, 1.0, 0.0).astype(BF16)
    prefix = _dot(before, onehot.astype(BF16)) + run[...]
    rank1 = jnp.sum(jnp.where(hit1, prefix, 0.0), -1, keepdims=True)
    rank2 = jnp.sum(jnp.where(hit2, prefix, 0.0), -1, keepdims=True)
    run[...] += jnp.sum(onehot, 0, keepdims=True)
    cnt_ref[...] = run[...]
    ids = jnp.where(lane == 0, i1 - float(N_GROUPS), jnp.where(lane == 1, i2 - float(N_GROUPS),
                    jnp.where(lane == 2, rank1, jnp.where(lane == 3, rank2, 0.0))))
    eid_ref[...] = ids.astype(jnp.int32)
    gate_ref[...] = jnp.where(lane == 0, g1, jnp.where(lane == 1, g2, 0.0))


def _router(h, w_rg, b_rg, w_re, b_re, tm):
    t, d = h.shape
    padw = LANES - N_GROUPS - N_EXPERTS
    w = jnp.concatenate([w_rg, w_re, jnp.zeros((d, padw), F32)], axis=1)
    b = jnp.concatenate([b_rg, b_re, jnp.zeros((padw,), F32)])[None, :]
    wh = w.astype(BF16)
    wl = (w - wh.astype(F32)).astype(BF16)
    row = pl.BlockSpec((tm, LANES), lambda i: (i, 0))
    one = pl.BlockSpec((1, LANES), lambda i: (0, 0))
    wspec = pl.BlockSpec((d, LANES), lambda i: (0, 0))
    return pl.pallas_call(
        _router_kernel,
        out_shape=(jax.ShapeDtypeStruct((t, LANES), jnp.int32), jax.ShapeDtypeStruct((t, LANES), F32),
                   jax.ShapeDtypeStruct((1, LANES), F32)),
        grid=(t // tm,),
        in_specs=[pl.BlockSpec((tm, d), lambda i: (i, 0)), wspec, wspec, one],
        out_specs=(row, row, one), scratch_shapes=[pltpu.VMEM((1, LANES), F32)],
        compiler_params=_params(("arbitrary",)), name="router",
    )(h, wh, wl, b)


def _dest_kernel(ids_ref, start_ref, o_ref):
    ids = ids_ref[...]
    lane = lax.broadcasted_iota(jnp.int32, ids.shape, 1)
    start = start_ref[...]

    def row_of(k):
        hit = lane == ids[:, k:k + 1] + N_GROUPS
        return jnp.sum(jnp.where(hit, start, 0.0), -1, keepdims=True).astype(jnp.int32) + ids[:, TOP_K + k:TOP_K + k + 1]

    o_ref[...] = jnp.where(lane == 0, row_of(0), jnp.where(lane == 1, row_of(1), 0))


def _dest_rows(ids, pad_start, tm):
    t = ids.shape[0]
    start = jnp.zeros((1, LANES), F32).at[0, N_GROUPS:N_GROUPS + N_EXPERTS].set(pad_start.astype(F32))
    row = pl.BlockSpec((tm, LANES), lambda i: (i, 0))
    return pl.pallas_call(
        _dest_kernel, out_shape=jax.ShapeDtypeStruct((t, LANES), jnp.int32), grid=(t // tm,),
        in_specs=[row, pl.BlockSpec((1, LANES), lambda i: (0, 0))], out_specs=row,
        compiler_params=_params(("parallel",)), name="dest_rows",
    )(ids, start)


def _expert_kernel(layer, be_ref, nx_ref, nu_ref, x_ref, wg_hbm, wu_hbm, wd_hbm, o_ref,
                   wgb, wub, wdb, wgs, wus, wds, sem):
    i = pl.program_id(0)

    def weight_copies(e):
        return [pltpu.make_async_copy(hbm.at[layer, e], stage, sem.at[n])
                for n, (hbm, stage) in enumerate(((wg_hbm, wgs), (wu_hbm, wus), (wd_hbm, wds)))]

    @pl.when(i == 0)
    def _():
        for c in weight_copies(be_ref[0]):
            c.start()

    @pl.when((i == 0) | (be_ref[i] != be_ref[jnp.maximum(i - 1, 0)]))
    def _():
        for c in weight_copies(be_ref[i]):
            c.wait()
        wgb[...] = wgs[...].astype(BF16)
        wub[...] = wus[...].astype(BF16)
        wdb[...] = wds[...].astype(BF16)

        @pl.when(nx_ref[i] >= 0)
        def _():
            for c in weight_copies(nx_ref[i]):
                c.start()

    @pl.when(i < nu_ref[0])
    def _():
        x = x_ref[...]
        hid = _silu(_dot(x, wgb[...])) * _dot(x, wub[...])
        o_ref[...] = _dot(hid.astype(BF16), wdb[...]).astype(o_ref.dtype)

    @pl.when(i >= nu_ref[0])
    def _():
        o_ref[...] = jnp.zeros_like(o_ref)


def _experts(xb, blk_expert, nxt_expert, n_used, w_gate, w_up, w_down, layer):
    n_rows, d = xb.shape
    n_blocks = n_rows // MOE_BLOCK
    hbm = pl.BlockSpec(memory_space=pl.ANY)
    grid_spec = pltpu.PrefetchScalarGridSpec(
        num_scalar_prefetch=3, grid=(n_blocks,),
        in_specs=[pl.BlockSpec((MOE_BLOCK, d), lambda i, be, nx, nu: (i, 0)), hbm, hbm, hbm],
        out_specs=pl.BlockSpec((MOE_BLOCK, d), lambda i, be, nx, nu: (i, 0)),
        scratch_shapes=[pltpu.VMEM((d, D_EXPERT), BF16), pltpu.VMEM((d, D_EXPERT), BF16),
                        pltpu.VMEM((D_EXPERT, d), BF16),
                        pltpu.VMEM((d, D_EXPERT), F32), pltpu.VMEM((d, D_EXPERT), F32),
                        pltpu.VMEM((D_EXPERT, d), F32), pltpu.SemaphoreType.DMA((3,))])
    return pl.pallas_call(
        functools.partial(_expert_kernel, layer), out_shape=jax.ShapeDtypeStruct((n_rows, d), BF16),
        grid_spec=grid_spec, compiler_params=_params(("arbitrary",)), name="experts",
    )(blk_expert, nxt_expert, n_used, xb, w_gate, w_up, w_down)


def _combine_kernel(h_ref, y1_ref, y2_ref, gate_ref, g_ref, b_ref, o_ref):
    f = gate_ref[:, 0:1] * y1_ref[...].astype(F32) + gate_ref[:, 1:2] * y2_ref[...].astype(F32)
    o_ref[...] = _layer_norm(DN_ALPHA * h_ref[...] + f, g_ref[...], b_ref[...])


def _combine(h, y1, y2, gate, g, b, tm):
    t, d = h.shape
    row = pl.BlockSpec((tm, d), lambda i: (i, 0))
    vec = pl.BlockSpec((1, d), lambda i: (0, 0))
    return pl.pallas_call(
        _combine_kernel, out_shape=jax.ShapeDtypeStruct((t, d), F32), grid=(t // tm,),
        in_specs=[row, row, row, pl.BlockSpec((tm, LANES), lambda i: (i, 0)), vec, vec], out_specs=row,
        compiler_params=_params(("parallel",)), name="combine_ln",
    )(h, y1, y2, gate, g.reshape(1, d), b.reshape(1, d))


def _dispatch(ids, cnt, tm):
    t = ids.shape[0]
    n_assign = t * TOP_K
    counts = cnt[0, N_GROUPS:N_GROUPS + N_EXPERTS].astype(jnp.int32)
    padded = (counts + MOE_BLOCK - 1) // MOE_BLOCK * MOE_BLOCK
    pad_end = jnp.cumsum(padded)
    pad_start = pad_end - padded
    start = jnp.cumsum(counts) - counts
    eid = ids[:, :TOP_K]
    dest = _dest_rows(ids, pad_start, tm)[:, :TOP_K]
    order = jnp.argsort(eid.reshape(n_assign))
    n_blocks = -(-(n_assign + N_EXPERTS * (MOE_BLOCK - 1)) // MOE_BLOCK)
    blk_first = (jnp.arange(n_blocks) * MOE_BLOCK)[:, None]
    ended = jnp.dot((pad_end[None, :] <= blk_first).astype(F32), jnp.ones((N_EXPERTS,), F32))
    experts = jnp.arange(N_EXPERTS)
    last_used = jnp.max(jnp.where(counts > 0, experts, 0))
    blk_expert = jnp.minimum(ended.astype(jnp.int32), last_used)
    later = (experts[None, :] > experts[:, None]) & (counts[None, :] > 0)
    nxt = jnp.min(jnp.where(later, experts[None, :], N_EXPERTS), axis=1)
    nxt = jnp.where(nxt == N_EXPERTS, -1, nxt)
    onehot = (blk_expert[:, None] == experts[None, :]).astype(F32)
    table = jnp.stack([pad_start - start, start + counts, nxt], axis=1).astype(F32)
    looked = jnp.dot(onehot, table, precision=lax.Precision.HIGHEST).astype(jnp.int32)
    shift = jnp.repeat(looked[:, 0], MOE_BLOCK)
    end = jnp.repeat(looked[:, 1], MOE_BLOCK)
    nxt_expert = looked[:, 2]
    rows = jnp.arange(n_blocks * MOE_BLOCK)
    src = rows - shift
    row_tok = jnp.where(src < end, jnp.take(order, jnp.clip(src, 0, n_assign - 1), mode="clip") // TOP_K,
                        rows % t).astype(jnp.int32)
    n_used = (pad_end[-1:] // MOE_BLOCK).astype(jnp.int32)
    return row_tok, dest, blk_expert, nxt_expert, n_used


def _moe(h, hb, w_rg, b_rg, w_re, b_re, w_gate, w_up, w_down, layer, g, b, tm):
    ids, gate, cnt = _router(h, w_rg, b_rg, w_re, b_re, tm)
    row_tok, dest, blk_expert, nxt_expert, n_used = _dispatch(ids, cnt, tm)
    xb = jnp.take(hb, row_tok, axis=0, mode="clip")
    yb = _experts(xb, blk_expert, nxt_expert, n_used, w_gate, w_up, w_down, layer)
    y1 = jnp.take(yb, dest[:, 0], axis=0, mode="clip")
    y2 = jnp.take(yb, dest[:, 1], axis=0, mode="clip")
    return _combine(h, y1, y2, gate, g, b, min(ROWS_MEMBOUND, h.shape[0]))


def kernel(x, positions, ln_in_g, ln_in_b, w_in, conv_w, a_log, dt_bias, gdn_norm_w, w_o_gdn, mla_q_norm_w, w_uq, mla_kv_norm_w, w_ukv, w_o_mla, w_out, ln1_g, ln1_b, w_router_group, b_router_group, w_router_expert, b_router_expert, w_gate, w_up, w_down, ln2_g, ln2_b):
    bsz, s_len, d = x.shape
    t = bsz * s_len
    tm = min(ROWS_MATMUL, t)
    tables = _rope_tables(positions, min(ROWS_MEMBOUND, t))
    h = _ln_in(x.reshape(t, d), ln_in_g, ln_in_b, min(ROWS_MEMBOUND, t))
    for l in range(DEPTH):
        qkvz, cq, ckv, kab, ab, gates = _proj(h, _pack_w_in(w_in[l]), tm)
        og = _gdn(qkvz, ab, conv_w[l], a_log[l], dt_bias[l], gdn_norm_w[l], bsz, s_len)
        om = _mla(cq, ckv, kab, tables, mla_q_norm_w[l], mla_kv_norm_w[l],
                  _pack_w_uq(w_uq[l]), w_ukv[l].astype(BF16), bsz, s_len)
        h, hb = _mixer_tail(og, om, gates, h, w_o_gdn[l].astype(BF16), w_o_mla[l].astype(BF16),
                            w_out[l].astype(BF16), ln1_g[l], ln1_b[l], min(ROWS_TAIL, t))
        h = _moe(h, hb, w_router_group[l], b_router_group[l], w_router_expert[l], b_router_expert[l],
                 w_gate, w_up, w_down, l, ln2_g[l], ln2_b[l], min(ROWS_ROUTER, t))
    return h.reshape(bsz, s_len, d)
```

```python
import functools

import numpy as np
import jax
import jax.numpy as jnp
from jax import lax
from jax.experimental import pallas as pl
from jax.experimental.pallas import tpu as pltpu

F32 = jnp.float32
BF16 = jnp.bfloat16

D_MODEL = 1024
DEPTH = 2
GDN_HEADS = 8
GDN_DK = 128
GDN_DV = 128
CONV_K = 5
CHUNK = 64
MLA_HEADS = 8
Q_LORA = 384
KV_LORA = 256
QK_NOPE = 128
QK_ROPE = 64
V_HEAD = 128
ROPE_BASE = 10000.0
N_GROUPS = 8
EXPERTS_PER_GROUP = 8
N_EXPERTS = N_GROUPS * EXPERTS_PER_GROUP
TOP_K = 2
D_EXPERT = 512
MOE_BLOCK = 256
DN_ALPHA = (2 * DEPTH) ** 0.25
LN_EPS = 1e-5
RMS_EPS = 1e-6

ROWS_MATMUL = 256
ROWS_MEMBOUND = 1024
ROWS_TAIL = 512
ROWS_ROUTER = 512
LANES = 128
VMEM_LIMIT = 56 * 1024 * 1024

QKVZ_W = 4 * GDN_HEADS * GDN_DK
KAB_W = 2 * LANES
AB_W = LANES
GATES_W = 2 * D_MODEL
PROJ_SEGS = (QKVZ_W, Q_LORA, KV_LORA, KAB_W, AB_W, GATES_W)
PROJ_W = sum(PROJ_SEGS)


def _params(sem):
    return pltpu.CompilerParams(dimension_semantics=sem, vmem_limit_bytes=VMEM_LIMIT)


def _layer_norm(x, g, b):
    mu = jnp.mean(x, -1, keepdims=True)
    xc = x - mu
    var = jnp.mean(xc * xc, -1, keepdims=True)
    return xc * lax.rsqrt(var + LN_EPS) * g + b


def _rms_norm(x, w):
    return x * lax.rsqrt(jnp.mean(x * x, -1, keepdims=True) + RMS_EPS) * w


def _silu(x):
    return (0.5 * x) * (1.0 + jnp.tanh(0.5 * x))


def _sigmoid(x):
    return 1.0 / (1.0 + jnp.exp(-x))


def _dot(a, b):
    return jnp.dot(a, b, preferred_element_type=F32)


def _dot_nt(a, b):
    return lax.dot_general(a, b, (((1,), (1,)), ((), ())), preferred_element_type=F32)


def _ds(i, n):
    return pl.ds(i * n, n) if isinstance(i, int) else pl.ds(pl.multiple_of(i * n, n), n)


def _dot_tn(a, b):
    return lax.dot_general(a, b, (((0,), (0,)), ((), ())), preferred_element_type=F32)


def _rope_kernel(pos_ref, freq_ref, mq_ref, mk1_ref, mk2_ref):
    ang = pos_ref[...].astype(F32) * freq_ref[...]
    c = jnp.cos(ang)
    s = jnp.sin(ang)
    quarter = lax.broadcasted_iota(jnp.int32, ang.shape, 1) >> 5
    even = (quarter & 1) == 0
    mq_ref[...] = jnp.where((quarter == 0) | (quarter == 3), c, s)
    mk1_ref[...] = jnp.where(even, c, s)
    mk2_ref[...] = jnp.where(even, s, c)


def _rope_tables(positions, tm):
    t = positions.size
    half = QK_ROPE // 2
    inv_freq = jnp.power(ROPE_BASE, -jnp.arange(half, dtype=F32) / half)
    freq = jnp.tile(inv_freq, LANES // half)[None, :]
    out = jax.ShapeDtypeStruct((t, LANES), F32)
    row = pl.BlockSpec((tm, LANES), lambda i: (i, 0))
    return pl.pallas_call(
        _rope_kernel, out_shape=(out, out, out), grid=(t // tm,),
        in_specs=[pl.BlockSpec((tm, 1), lambda i: (i, 0)), pl.BlockSpec((1, LANES), lambda i: (0, 0))],
        out_specs=(row, row, row), compiler_params=_params(("parallel",)), name="rope_tables",
    )(positions.reshape(t, 1), freq)


def _ln_kernel(x_ref, g_ref, b_ref, o_ref):
    o_ref[...] = _layer_norm(x_ref[...], g_ref[...], b_ref[...])


def _ln_in(x2, g, b, tm):
    t, d = x2.shape
    row = pl.BlockSpec((tm, d), lambda i: (i, 0))
    vec = pl.BlockSpec((1, d), lambda i: (0, 0))
    return pl.pallas_call(
        _ln_kernel, out_shape=jax.ShapeDtypeStruct((t, d), F32), grid=(t // tm,),
        in_specs=[row, vec, vec], out_specs=row, compiler_params=_params(("parallel",)), name="ln_in",
    )(x2, g.reshape(1, d), b.reshape(1, d))


PROJ_CHUNK = 512


def _proj_kernel(h_ref, w_ref, *out_refs):
    a = h_ref[...].astype(BF16)
    off = 0
    for ref, width in zip(out_refs, PROJ_SEGS):
        for c in range(0, width, PROJ_CHUNK):
            wc = min(PROJ_CHUNK, width - c)
            ref[:, c:c + wc] = _dot(a, w_ref[:, off + c:off + c + wc]).astype(ref.dtype)
        off += width


def _proj(h, w_all, tm):
    t, d = h.shape
    dts = (BF16, F32, F32, F32, F32, BF16)
    outs = tuple(jax.ShapeDtypeStruct((t, w), dt) for w, dt in zip(PROJ_SEGS, dts))
    return pl.pallas_call(
        _proj_kernel, out_shape=outs, grid=(t // tm,),
        in_specs=[pl.BlockSpec((tm, d), lambda i: (i, 0)), pl.BlockSpec((d, PROJ_W), lambda i: (0, 0))],
        out_specs=tuple(pl.BlockSpec((tm, w), lambda i: (i, 0)) for w in PROJ_SEGS),
        compiler_params=_params(("parallel",)), name="in_proj",
    )(h, w_all)


def _pack_w_in(w_in):
    hq = GDN_HEADS * GDN_DK
    o_a = 4 * hq
    o_bt = o_a + 2 * GDN_HEADS
    o_cq = o_bt + 2 * GDN_HEADS
    o_ckv = o_cq + Q_LORA
    o_kr = o_ckv + KV_LORA
    o_g = o_kr + QK_ROPE
    half = QK_ROPE // 2
    k1 = w_in[:, o_kr:o_kr + half]
    k2 = w_in[:, o_kr + half:o_g]
    d = w_in.shape[0]
    ab = jnp.concatenate([w_in[:, o_a:o_cq], jnp.zeros((d, AB_W - 4 * GDN_HEADS), w_in.dtype)], 1)
    return jnp.concatenate(
        [w_in[:, :o_a], w_in[:, o_cq:o_ckv], w_in[:, o_ckv:o_kr],
         k1, k1, k1, k1, -k2, k2, -k2, k2, ab, w_in[:, o_g:]], axis=1).astype(BF16)


GDN_HPS = 2
CONV_PAD = 8
CUM_CHUNKS = 8
PREP_CHUNKS = 16


def _gdn_kernel(q_ref, k_ref, v_ref, z_ref, ab_ref, cwq_ref, cwk_ref, cwv_ref, alog_ref, dtb_ref, nw_ref,
                o_ref, qs, ks, vs, gs, bs, brs, bns, mqs, ges, os_, xp, gates_s):
    s_len = q_ref.shape[0]
    nc = s_len // CHUNK
    first_head = pl.program_id(1) * GDN_HPS

    xp[:CONV_PAD, :] = jnp.zeros((CONV_PAD, LANES), F32)
    xp[CONV_PAD + s_len:, :] = jnp.zeros((CONV_PAD, LANES), F32)

    def conv_silu(x_ref, cw_ref, cols):
        xp[CONV_PAD:CONV_PAD + s_len, :] = x_ref[:, cols].astype(F32)
        acc = None
        for j in range(CONV_K):
            term = xp[CONV_PAD + j - CONV_K // 2:CONV_PAD + j - CONV_K // 2 + s_len, :] * cw_ref[j:j + 1, cols]
            acc = term if acc is None else acc + term
        return _silu(acc)

    def l2n(x, scale=1.0):
        return x * (lax.rsqrt(jnp.sum(x * x, -1, keepdims=True) + RMS_EPS) * scale)

    lane = lax.broadcasted_iota(jnp.int32, (1, LANES), 1)

    def pick(vals, idx):
        col = jnp.sum(jnp.where(lane == idx, vals, 0.0), -1, keepdims=True)
        return jnp.broadcast_to(col, (s_len, LANES))

    def stage(hh):
        cols = slice(hh * LANES, (hh + 1) * LANES)
        qs[...] = l2n(conv_silu(q_ref, cwq_ref, cols), GDN_DK ** -0.5)
        ks[...] = l2n(conv_silu(k_ref, cwk_ref, cols))
        vs[...] = conv_silu(v_ref, cwv_ref, cols)
        gates = gates_s[...]
        for d in range(2):
            gs[d] = pick(gates, first_head + hh + d * GDN_HEADS)
            bs[d] = pick(gates, first_head + hh + (2 + d) * GDN_HEADS)

    @pl.when(pl.program_id(1) == 0)
    def _():
        ab = ab_ref[...]
        x = ab + dtb_ref[...]
        softplus = jnp.maximum(x, 0.0) + jnp.log(1.0 + jnp.exp(-jnp.abs(x)))
        gates_s[...] = jnp.where(lane < 2 * GDN_HEADS, -jnp.exp(alog_ref[...]) * softplus, _sigmoid(ab))

    ci = lax.broadcasted_iota(jnp.int32, (CHUNK, 2 * CHUNK), 0)
    cj2 = lax.broadcasted_iota(jnp.int32, (CHUNK, 2 * CHUNK), 1)
    right = cj2 >= CHUNK
    cj = cj2 & (CHUNK - 1)
    eye = jnp.where(ci == cj, 1.0, 0.0)
    incl = (ci >= cj, ci <= cj)
    strict = (ci > cj, ci < cj)
    tri3 = tuple(jnp.concatenate([jnp.where(m[:, :CHUNK], 1.0, 0.0).astype(BF16)] * 3, axis=1) for m in incl)
    ones3 = jnp.ones((CHUNK, 3 * CHUNK), BF16)
    last_row = (CHUNK - 1, 0)

    unroll = CUM_CHUNKS if nc % CUM_CHUNKS == 0 else 1

    def cumulate(c, carry):
        rows = [pl.ds(pl.multiple_of((c * unroll + j) * CHUNK, CHUNK), CHUNK) for j in range(unroll)]
        chains = [(j, d) for j in range(unroll) for d in range(2)]
        def split3(x):
            x1 = x.astype(BF16)
            r1 = x - x1.astype(F32)
            x2 = r1.astype(BF16)
            return jnp.concatenate([x1, x2, (r1 - x2.astype(F32)).astype(BF16)], axis=0)

        parts = [split3(gs[d, rows[j], :]) for j, d in chains]
        b = [_dot(tri3[d], p) for (j, d), p in zip(chains, parts)]
        diag = [split3(x * eye) for x in b]
        b_row = [_dot(ones3, x) for x in diag]
        for (j, d), x, y in zip(chains, b, b_row):
            gs[d, rows[j], :] = x
            brs[d, rows[j], :] = y
        return carry

    def prep(hh, it, carry):
        unroll = PREP_CHUNKS if nc % PREP_CHUNKS == 0 else 1
        cs = [it * unroll + j for j in range(unroll)]
        rows = [pl.ds(pl.multiple_of(c * CHUNK, CHUNK), CHUNK) for c in cs]
        q = [qs[r, :] for r in rows]
        k = [ks[r, :] for r in rows]
        kb = [x.astype(BF16) for x in k]
        kk = [_dot_nt(x, jnp.concatenate([x, x], axis=0)) for x in kb]
        qk = [_dot_nt(x.astype(BF16), y) for x, y in zip(q, kb)]
        chains = [(j, d) for j in range(unroll) for d in range(2)]
        b = [gs[d, rows[j], :] for j, d in chains]
        beta = [bs[d, rows[j], :] for j, d in chains]
        gamma = [jnp.exp(jnp.where(incl[d], b[n] - brs[d, rows[j], :], -jnp.inf))
                 for n, (j, d) in enumerate(chains)]
        pt = [jnp.where(right, eye, jnp.where(strict[d], -beta[n] * kk[j] * gamma[n], 0.0))
              for n, (j, d) in enumerate(chains)]
        for _ in range(int(np.log2(CHUNK))):
            ptb = [x.astype(BF16) for x in pt]
            prod = [_dot(x[:, :CHUNK], x) for x in ptb]
            pt = [y + jnp.where(right, x, 0.0) for x, y in zip(pt, prod)]
        eb = [jnp.exp(x) for x in b]
        rhs = [jnp.concatenate([jnp.zeros((CHUNK, GDN_DV + GDN_DK), F32), jnp.concatenate(
                   [beta[n] * vs[rows[j], :], beta[n] * eb[n] * k[j]], axis=1)], axis=0).astype(BF16)
               for n, (j, d) in enumerate(chains)]
        wk = [_dot(t.astype(BF16), r).astype(BF16) for t, r in zip(pt, rhs)]
        b_last = [b[n][last_row[d]:last_row[d] + 1, :] for n, (j, d) in enumerate(chains)]
        kdt = [jnp.transpose(k[j] * jnp.exp(b_last[n] - b[n])).astype(BF16)
               for n, (j, d) in enumerate(chains)]
        bm = [_dot(x, y) for x, y in zip(kdt, wk)]
        pw = [_dot((qk[j] * gamma[n][:, :CHUNK]).astype(BF16), wk[n])
              for n, (j, d) in enumerate(chains)]
        for n, (j, d) in enumerate(chains):
            c = cs[j]
            bns[hh, d, pl.ds(pl.multiple_of(c * GDN_DK, GDN_DK), GDN_DK), :] = bm[n][:, :GDN_DV]
            r3 = pl.multiple_of(c * (GDN_DK + CHUNK), CHUNK)
            mqs[hh, d, pl.ds(r3, GDN_DK), :] = bm[n][:, GDN_DV:].astype(BF16)
            mqs[hh, d, pl.ds(r3 + GDN_DK, CHUNK), :] = (q[j] * eb[n] - pw[n][:, GDN_DV:]).astype(BF16)
            ges[hh, d, pl.ds(pl.multiple_of(c * 8, 8), 8), :] = jnp.broadcast_to(jnp.exp(b_last[n]), (8, LANES))
        for j in range(unroll):
            os_[hh, rows[j], :] = pw[2 * j][:, :GDN_DV] + pw[2 * j + 1][:, :GDN_DV]
        return carry

    for hh in range(GDN_HPS):
        stage(hh)
        lax.fori_loop(0, nc // unroll, cumulate, 0)
        lax.fori_loop(0, nc // (PREP_CHUNKS if nc % PREP_CHUNKS == 0 else 1), functools.partial(prep, hh), 0)

    chains = [(hh, d) for hh in range(GDN_HPS) for d in range(2)]

    def step(i, states):
        cs = (i, nc - 1 - i)
        sb = [x.astype(BF16) for x in states]
        r = [_dot(mqs[hh, d, pl.ds(pl.multiple_of(cs[d] * (GDN_DK + CHUNK), CHUNK), GDN_DK + CHUNK), :], sb[n])
             for n, (hh, d) in enumerate(chains)]
        new = []
        for n, (hh, d) in enumerate(chains):
            c = cs[d]
            os_[hh, pl.ds(pl.multiple_of(c * CHUNK, CHUNK), CHUNK), :] += r[n][GDN_DK:]
            ge = ges[hh, d, pl.ds(pl.multiple_of(c * 8, 8), 1), :]
            bn = bns[hh, d, pl.ds(pl.multiple_of(c * GDN_DK, GDN_DK), GDN_DK), :]
            new.append(ge * states[n] - r[n][:GDN_DK] + bn)
        return tuple(new)

    s0 = jnp.zeros((GDN_DK, GDN_DV), F32)
    lax.fori_loop(0, nc, step, (s0,) * len(chains))

    for hh in range(GDN_HPS):
        cols = slice(hh * LANES, (hh + 1) * LANES)
        o_ref[:, cols] = (_rms_norm(os_[hh], nw_ref[...]) * _silu(z_ref[:, cols].astype(F32))).astype(o_ref.dtype)


def _gdn(qkvz, ab, conv_w, a_log, dt_bias, norm_w, bsz, s_len):
    nh = GDN_HEADS
    qkvz3 = qkvz.reshape(bsz, s_len, QKVZ_W)
    ab3 = ab.reshape(bsz, s_len, AB_W)
    pad = jnp.zeros((AB_W - 2 * nh,), F32)
    alog = jnp.concatenate([a_log.reshape(-1), pad])[None, :]
    dtb = jnp.concatenate([dt_bias.reshape(-1), pad])[None, :]

    hps = GDN_HPS
    ng = nh // hps

    def col(k):
        return pl.BlockSpec((None, s_len, hps * LANES), lambda b, h: (b, 0, k * ng + h))

    def cw(k):
        return pl.BlockSpec((CONV_K, hps * LANES), lambda b, h: (0, k * ng + h))

    vec = pl.BlockSpec((1, LANES), lambda b, h: (0, 0))
    nc = s_len // CHUNK
    scratch = [
        pltpu.VMEM((s_len, LANES), F32), pltpu.VMEM((s_len, LANES), F32), pltpu.VMEM((s_len, LANES), F32),
        pltpu.VMEM((2, s_len, LANES), F32), pltpu.VMEM((2, s_len, LANES), F32),
        pltpu.VMEM((2, s_len, 2 * CHUNK), F32), pltpu.VMEM((hps, 2, nc * GDN_DK, GDN_DV), F32),
        pltpu.VMEM((hps, 2, nc * (GDN_DK + CHUNK), GDN_DV), BF16),
        pltpu.VMEM((hps, 2, nc * 8, LANES), F32), pltpu.VMEM((hps, s_len, GDN_DV), F32),
        pltpu.VMEM((s_len + 2 * CONV_PAD, LANES), F32), pltpu.VMEM((s_len, AB_W), F32),
    ]
    out = pl.pallas_call(
        _gdn_kernel, out_shape=jax.ShapeDtypeStruct((bsz, s_len, nh * GDN_DV), BF16), grid=(bsz, ng),
        in_specs=[col(0), col(1), col(2), col(3),
                  pl.BlockSpec((None, s_len, AB_W), lambda b, h: (b, 0, 0)),
                  cw(0), cw(1), cw(2), vec, vec, vec],
        out_specs=pl.BlockSpec((None, s_len, hps * GDN_DV), lambda b, h: (b, 0, h)),
        scratch_shapes=scratch, compiler_params=_params(("parallel", "arbitrary")), name="gdn",
    )(qkvz3, qkvz3, qkvz3, qkvz3, ab3, conv_w, conv_w, conv_w, alog, dtb, norm_w.reshape(1, GDN_DV))
    return out.reshape(bsz * s_len, nh * GDN_DV)


MLA_TQ = 256
MLA_KC = 1024


def _mla_kernel(cq_ref, ckv_ref, kab_ref, mq_ref, mk1_ref, mk2_ref, qnw_ref, kvnw_ref, wq_ref, wkv_ref,
                o_ref, q_s, k_s, vt_s, sta_s, stb_s, cqn_s, ckvn_s):
    s_len = cq_ref.shape[0]
    scale = (QK_NOPE + QK_ROPE) ** -0.5

    @pl.when(pl.program_id(1) == 0)
    def _():
        cqn_s[...] = _rms_norm(cq_ref[...], qnw_ref[...]).astype(BF16)
        ckvn_s[...] = _rms_norm(ckv_ref[...], kvnw_ref[...]).astype(BF16)
        kab = kab_ref[...]
        k_s[:, QK_NOPE:] = (kab[:, :LANES] * mk1_ref[...] + kab[:, LANES:] * mk2_ref[...]).astype(BF16)
        vt_s[V_HEAD:, :] = jnp.ones((vt_s.shape[0] - V_HEAD, s_len), BF16)

    q = _dot(cqn_s[...], wq_ref[...]) * scale
    q_s[:, :QK_NOPE] = q[:, :QK_NOPE].astype(BF16)
    q_s[:, QK_NOPE:] = (q[:, QK_NOPE:] * mq_ref[...]).astype(BF16)
    kv = _dot(ckvn_s[...], wkv_ref[...])
    k_s[:, :QK_NOPE] = kv[:, :QK_NOPE].astype(BF16)
    vt_s[:V_HEAD, :] = jnp.transpose(kv[:, QK_NOPE:]).astype(BF16)

    tq = min(MLA_TQ, s_len)

    nq = s_len // tq
    kc = min(MLA_KC, s_len)

    def scores(i, st):
        st[...] = _dot_nt(k_s[...], q_s[_ds(i, tq), :])

    def attend(i, st, nxt):
        m = jnp.max(st[...], 0, keepdims=True)
        ot = jnp.zeros((vt_s.shape[0], tq), F32)
        for c in range(0, s_len, kc):
            if nxt is not None:
                nxt[c:c + kc, :] = _dot_nt(k_s[c:c + kc, :], q_s[_ds(i + 1, tq), :])
            p = jnp.exp(st[c:c + kc, :] - m).astype(BF16)
            ot = ot + _dot(vt_s[:, c:c + kc], p)
        o = jnp.transpose(ot[:V_HEAD] / ot[V_HEAD:V_HEAD + 1])
        o_ref[_ds(i, tq), :] = o.astype(o_ref.dtype)

    def pair(j, prefetch):
        attend(2 * j, sta_s, stb_s)
        attend(2 * j + 1, stb_s, sta_s if prefetch else None)

    assert nq == 1 or nq % 2 == 0
    scores(0, sta_s)
    if nq == 1:
        attend(0, sta_s, None)
    else:
        def body(j, carry):
            pair(j, True)
            return carry
        lax.fori_loop(0, nq // 2 - 1, body, 0)
        pair(nq // 2 - 1, False)


def _pack_w_uq(w_uq):
    w = w_uq.reshape(Q_LORA, MLA_HEADS, QK_NOPE + QK_ROPE)
    half = QK_ROPE // 2
    r1 = w[..., QK_NOPE:QK_NOPE + half]
    r2 = w[..., QK_NOPE + half:]
    w = jnp.concatenate([w[..., :QK_NOPE], r1, r1, -r2, r2], axis=-1)
    return w.reshape(Q_LORA, MLA_HEADS * 2 * LANES).astype(BF16)


def _mla(cq, ckv, kab, tables, qnw, kvnw, wq, wkv, bsz, s_len):
    nh = MLA_HEADS

    def per_b(width):
        return pl.BlockSpec((None, s_len, width), lambda b, h: (b, 0, 0))

    def vec(width):
        return pl.BlockSpec((1, width), lambda b, h: (0, 0))

    mq, mk1, mk2 = (t.reshape(bsz, s_len, LANES) for t in tables)
    out = pl.pallas_call(
        _mla_kernel, out_shape=jax.ShapeDtypeStruct((bsz, s_len, nh * V_HEAD), BF16), grid=(bsz, nh),
        in_specs=[per_b(Q_LORA), per_b(KV_LORA), per_b(KAB_W), per_b(LANES), per_b(LANES), per_b(LANES),
                  vec(Q_LORA), vec(KV_LORA),
                  pl.BlockSpec((Q_LORA, 2 * LANES), lambda b, h: (0, h)),
                  pl.BlockSpec((KV_LORA, QK_NOPE + V_HEAD), lambda b, h: (0, h))],
        out_specs=pl.BlockSpec((None, s_len, V_HEAD), lambda b, h: (b, 0, h)),
        scratch_shapes=[pltpu.VMEM((s_len, 2 * LANES), BF16), pltpu.VMEM((s_len, 2 * LANES), BF16),
                        pltpu.VMEM((V_HEAD + 16, s_len), BF16),
                        pltpu.VMEM((s_len, min(MLA_TQ, s_len)), F32),
                        pltpu.VMEM((s_len, min(MLA_TQ, s_len)), F32),
                        pltpu.VMEM((s_len, Q_LORA), BF16), pltpu.VMEM((s_len, KV_LORA), BF16)],
        compiler_params=_params(("parallel", "arbitrary")), name="mla",
    )(cq.reshape(bsz, s_len, Q_LORA), ckv.reshape(bsz, s_len, KV_LORA), kab.reshape(bsz, s_len, KAB_W),
      mq, mk1, mk2, qnw.reshape(1, Q_LORA), kvnw.reshape(1, KV_LORA), wq, wkv)
    return out.reshape(bsz * s_len, nh * V_HEAD)


def _tail_kernel(og_ref, om_ref, gates_ref, h_ref, wog_ref, wom_ref, wout_ref, g_ref, b_ref, o_ref, ob_ref):
    tm, d = h_ref.shape
    parts = [slice(k * (tm // 2), (k + 1) * (tm // 2)) for k in range(2)] if tm % 32 == 0 else [slice(0, tm)]
    y_gdn = [_dot(og_ref[r, :], wog_ref[...]) for r in parts]
    y_mla = [_dot(om_ref[r, :], wom_ref[...]) for r in parts]
    m = []
    for r, yg, ym in zip(parts, y_gdn, y_mla):
        y = _sigmoid(gates_ref[r, :d].astype(F32)) * yg + _sigmoid(gates_ref[r, d:].astype(F32)) * ym
        m.append(_dot(y.astype(BF16), wout_ref[...]))
    for r, mk in zip(parts, m):
        out = _layer_norm(DN_ALPHA * h_ref[r, :] + mk, g_ref[...], b_ref[...])
        o_ref[r, :] = out
        ob_ref[r, :] = out.astype(BF16)


def _mixer_tail(og, om, gates, h, wog, wom, wout, g, b, tm):
    t, d = h.shape
    row = lambda w: pl.BlockSpec((tm, w), lambda i: (i, 0))
    full = pl.BlockSpec((d, d), lambda i: (0, 0))
    vec = pl.BlockSpec((1, d), lambda i: (0, 0))
    return pl.pallas_call(
        _tail_kernel, out_shape=(jax.ShapeDtypeStruct((t, d), F32), jax.ShapeDtypeStruct((t, d), BF16)),
        grid=(t // tm,),
        in_specs=[row(d), row(d), row(2 * d), row(d), full, full, full, vec, vec],
        out_specs=(row(d), row(d)), compiler_params=_params(("parallel",)), name="mixer_tail",
    )(og, om, gates, h, wog, wom, wout, g.reshape(1, d), b.reshape(1, d))


def _router_kernel(h_ref, wh_ref, wl_ref, b_ref, eid_ref, gate_ref, cnt_ref, run):
    @pl.when(pl.program_id(0) == 0)
    def _():
        run[...] = jnp.zeros_like(run)

    x = h_ref[...]
    xh = x.astype(BF16)
    xl = (x - xh.astype(F32)).astype(BF16)
    logits = _dot(xh, wh_ref[...]) + (_dot(xl, wh_ref[...]) + _dot(xh, wl_ref[...])) + b_ref[...]
    tm = logits.shape[0]
    lane = lax.broadcasted_iota(jnp.int32, logits.shape, 1)
    lane_f = lane.astype(F32)
    neg = -jnp.inf

    def first_max(vals):
        m = jnp.max(vals, -1, keepdims=True)
        idx = jnp.min(jnp.where(vals == m, lane_f, float(LANES)), -1, keepdims=True)
        return m, idx

    is_grp = lane < N_GROUPS
    mg, grp = first_max(jnp.where(is_grp, logits, neg))
    p_grp = 1.0 / jnp.sum(jnp.where(is_grp, jnp.exp(logits - mg), 0.0), -1, keepdims=True)
    in_grp = ((lane - N_GROUPS) >> 3).astype(F32) == grp
    le = jnp.where(in_grp, logits, neg)
    l1, i1 = first_max(le)
    l2, i2 = first_max(jnp.where(lane_f == i1, neg, le))
    t = jnp.exp(l2 - l1)
    g1 = p_grp / (1.0 + t)
    g2 = p_grp * t / (1.0 + t)
    hit1 = lane_f == i1
    hit2 = lane_f == i2
    onehot = jnp.where(hit1 | hit2, 1.0, 0.0)
    ri = lax.broadcasted_iota(jnp.int32, (tm, tm), 0)
    rj = lax.broadcasted_iota(jnp.int32, (tm, tm), 1)
    before = jnp.where(ri > rj, 1.0, 0.0).astype(BF16)
    prefix = _dot(before, onehot.astype(BF16)) + run[...]
    rank1 = jnp.sum(jnp.where(hit1, prefix, 0.0), -1, keepdims=True)
    rank2 = jnp.sum(jnp.where(hit2, prefix, 0.0), -1, keepdims=True)
    run[...] += jnp.sum(onehot, 0, keepdims=True)
    cnt_ref[...] = run[...]
    ids = jnp.where(lane == 0, i1 - float(N_GROUPS), jnp.where(lane == 1, i2 - float(N_GROUPS),
                    jnp.where(lane == 2, rank1, jnp.where(lane == 3, rank2, 0.0))))
    eid_ref[...] = ids.astype(jnp.int32)
    gate_ref[...] = jnp.where(lane == 0, g1, jnp.where(lane == 1, g2, 0.0))


def _router(h, w_rg, b_rg, w_re, b_re, tm):
    t, d = h.shape
    padw = LANES - N_GROUPS - N_EXPERTS
    w = jnp.concatenate([w_rg, w_re, jnp.zeros((d, padw), F32)], axis=1)
    b = jnp.concatenate([b_rg, b_re, jnp.zeros((padw,), F32)])[None, :]
    wh = w.astype(BF16)
    wl = (w - wh.astype(F32)).astype(BF16)
    row = pl.BlockSpec((tm, LANES), lambda i: (i, 0))
    one = pl.BlockSpec((1, LANES), lambda i: (0, 0))
    wspec = pl.BlockSpec((d, LANES), lambda i: (0, 0))
    return pl.pallas_call(
        _router_kernel,
        out_shape=(jax.ShapeDtypeStruct((t, LANES), jnp.int32), jax.ShapeDtypeStruct((t, LANES), F32),
                   jax.ShapeDtypeStruct((1, LANES), F32)),
        grid=(t // tm,),
        in_specs=[pl.BlockSpec((tm, d), lambda i: (i, 0)), wspec, wspec, one],
        out_specs=(row, row, one), scratch_shapes=[pltpu.VMEM((1, LANES), F32)],
        compiler_params=_params(("arbitrary",)), name="router",
    )(h, wh, wl, b)


def _dest_kernel(ids_ref, start_ref, o_ref):
    ids = ids_ref[...]
    lane = lax.broadcasted_iota(jnp.int32, ids.shape, 1)
    start = start_ref[...]

    def row_of(k):
        hit = lane == ids[:, k:k + 1] + N_GROUPS
        return jnp.sum(jnp.where(hit, start, 0.0), -1, keepdims=True).astype(jnp.int32) + ids[:, TOP_K + k:TOP_K + k + 1]

    o_ref[...] = jnp.where(lane == 0, row_of(0), jnp.where(lane == 1, row_of(1), 0))


def _dest_rows(ids, pad_start, tm):
    t = ids.shape[0]
    start = jnp.zeros((1, LANES), F32).at[0, N_GROUPS:N_GROUPS + N_EXPERTS].set(pad_start.astype(F32))
    row = pl.BlockSpec((tm, LANES), lambda i: (i, 0))
    return pl.pallas_call(
        _dest_kernel, out_shape=jax.ShapeDtypeStruct((t, LANES), jnp.int32), grid=(t // tm,),
        in_specs=[row, pl.BlockSpec((1, LANES), lambda i: (0, 0))], out_specs=row,
        compiler_params=_params(("parallel",)), name="dest_rows",
    )(ids, start)


def _expert_kernel(layer, be_ref, nx_ref, nu_ref, x_ref, wg_hbm, wu_hbm, wd_hbm, o_ref,
                   wgb, wub, wdb, wgs, wus, wds, sem):
    i = pl.program_id(0)

    def weight_copies(e):
        return [pltpu.make_async_copy(hbm.at[layer, e], stage, sem.at[n])
                for n, (hbm, stage) in enumerate(((wg_hbm, wgs), (wu_hbm, wus), (wd_hbm, wds)))]

    @pl.when(i == 0)
    def _():
        for c in weight_copies(be_ref[0]):
            c.start()

    @pl.when((i == 0) | (be_ref[i] != be_ref[jnp.maximum(i - 1, 0)]))
    def _():
        for c in weight_copies(be_ref[i]):
            c.wait()
        wgb[...] = wgs[...].astype(BF16)
        wub[...] = wus[...].astype(BF16)
        wdb[...] = wds[...].astype(BF16)

        @pl.when(nx_ref[i] >= 0)
        def _():
            for c in weight_copies(nx_ref[i]):
                c.start()

    @pl.when(i < nu_ref[0])
    def _():
        x = x_ref[...]
        hid = _silu(_dot(x, wgb[...])) * _dot(x, wub[...])
        o_ref[...] = _dot(hid.astype(BF16), wdb[...]).astype(o_ref.dtype)

    @pl.when(i >= nu_ref[0])
    def _():
        o_ref[...] = jnp.zeros_like(o_ref)


def _experts(xb, blk_expert, nxt_expert, n_used, w_gate, w_up, w_down, layer):
    n_rows, d = xb.shape
    n_blocks = n_rows // MOE_BLOCK
    hbm = pl.BlockSpec(memory_space=pl.ANY)
    grid_spec = pltpu.PrefetchScalarGridSpec(
        num_scalar_prefetch=3, grid=(n_blocks,),
        in_specs=[pl.BlockSpec((MOE_BLOCK, d), lambda i, be, nx, nu: (i, 0)), hbm, hbm, hbm],
        out_specs=pl.BlockSpec((MOE_BLOCK, d), lambda i, be, nx, nu: (i, 0)),
        scratch_shapes=[pltpu.VMEM((d, D_EXPERT), BF16), pltpu.VMEM((d, D_EXPERT), BF16),
                        pltpu.VMEM((D_EXPERT, d), BF16),
                        pltpu.VMEM((d, D_EXPERT), F32), pltpu.VMEM((d, D_EXPERT), F32),
                        pltpu.VMEM((D_EXPERT, d), F32), pltpu.SemaphoreType.DMA((3,))])
    return pl.pallas_call(
        functools.partial(_expert_kernel, layer), out_shape=jax.ShapeDtypeStruct((n_rows, d), BF16),
        grid_spec=grid_spec, compiler_params=_params(("arbitrary",)), name="experts",
    )(blk_expert, nxt_expert, n_used, xb, w_gate, w_up, w_down)


def _combine_kernel(h_ref, y1_ref, y2_ref, gate_ref, g_ref, b_ref, o_ref):
    f = gate_ref[:, 0:1] * y1_ref[...].astype(F32) + gate_ref[:, 1:2] * y2_ref[...].astype(F32)
    o_ref[...] = _layer_norm(DN_ALPHA * h_ref[...] + f, g_ref[...], b_ref[...])


def _combine(h, y1, y2, gate, g, b, tm):
    t, d = h.shape
    row = pl.BlockSpec((tm, d), lambda i: (i, 0))
    vec = pl.BlockSpec((1, d), lambda i: (0, 0))
    return pl.pallas_call(
        _combine_kernel, out_shape=jax.ShapeDtypeStruct((t, d), F32), grid=(t // tm,),
        in_specs=[row, row, row, pl.BlockSpec((tm, LANES), lambda i: (i, 0)), vec, vec], out_specs=row,
        compiler_params=_params(("parallel",)), name="combine_ln",
    )(h, y1, y2, gate, g.reshape(1, d), b.reshape(1, d))


def _dispatch(ids, cnt, tm):
    t = ids.shape[0]
    n_assign = t * TOP_K
    counts = cnt[0, N_GROUPS:N_GROUPS + N_EXPERTS].astype(jnp.int32)
    padded = (counts + MOE_BLOCK - 1) // MOE_BLOCK * MOE_BLOCK
    pad_end = jnp.cumsum(padded)
    pad_start = pad_end - padded
    start = jnp.cumsum(counts) - counts
    eid = ids[:, :TOP_K]
    dest = _dest_rows(ids, pad_start, tm)[:, :TOP_K]
    order = jnp.argsort(eid.reshape(n_assign))
    n_blocks = -(-(n_assign + N_EXPERTS * (MOE_BLOCK - 1)) // MOE_BLOCK)
    blk_first = (jnp.arange(n_blocks) * MOE_BLOCK)[:, None]
    ended = jnp.dot((pad_end[None, :] <= blk_first).astype(F32), jnp.ones((N_EXPERTS,), F32))
    experts = jnp.arange(N_EXPERTS)
    last_used = jnp.max(jnp.where(counts > 0, experts, 0))
    blk_expert = jnp.minimum(ended.astype(jnp.int32), last_used)
    later = (experts[None, :] > experts[:, None]) & (counts[None, :] > 0)
    nxt = jnp.min(jnp.where(later, experts[None, :], N_EXPERTS), axis=1)
    nxt = jnp.where(nxt == N_EXPERTS, -1, nxt)
    onehot = (blk_expert[:, None] == experts[None, :]).astype(F32)
    table = jnp.stack([pad_start - start, start + counts, nxt], axis=1).astype(F32)
    looked = jnp.dot(onehot, table, precision=lax.Precision.HIGHEST).astype(jnp.int32)
    shift = jnp.repeat(looked[:, 0], MOE_BLOCK)
    end = jnp.repeat(looked[:, 1], MOE_BLOCK)
    nxt_expert = looked[:, 2]
    rows = jnp.arange(n_blocks * MOE_BLOCK)
    src = rows - shift
    row_tok = jnp.where(src < end, jnp.take(order, jnp.clip(src, 0, n_assign - 1), mode="clip") // TOP_K,
                        rows % t).astype(jnp.int32)
    n_used = (pad_end[-1:] // MOE_BLOCK).astype(jnp.int32)
    return row_tok, dest, blk_expert, nxt_expert, n_used


def _moe(h, hb, w_rg, b_rg, w_re, b_re, w_gate, w_up, w_down, layer, g, b, tm):
    ids, gate, cnt = _router(h, w_rg, b_rg, w_re, b_re, tm)
    row_tok, dest, blk_expert, nxt_expert, n_used = _dispatch(ids, cnt, tm)
    xb = jnp.take(hb, row_tok, axis=0, mode="clip")
    yb = _experts(xb, blk_expert, nxt_expert, n_used, w_gate, w_up, w_down, layer)
    y1 = jnp.take(yb, dest[:, 0], axis=0, mode="clip")
    y2 = jnp.take(yb, dest[:, 1], axis=0, mode="clip")
    return _combine(h, y1, y2, gate, g, b, min(ROWS_MEMBOUND, h.shape[0]))


def kernel(x, positions, ln_in_g, ln_in_b, w_in, conv_w, a_log, dt_bias, gdn_norm_w, w_o_gdn, mla_q_norm_w, w_uq, mla_kv_norm_w, w_ukv, w_o_mla, w_out, ln1_g, ln1_b, w_router_group, b_router_group, w_router_expert, b_router_expert, w_gate, w_up, w_down, ln2_g, ln2_b):
    bsz, s_len, d = x.shape
    t = bsz * s_len
    tm = min(ROWS_MATMUL, t)
    tables = _rope_tables(positions, min(ROWS_MEMBOUND, t))
    h = _ln_in(x.reshape(t, d), ln_in_g, ln_in_b, min(ROWS_MEMBOUND, t))
    for l in range(DEPTH):
        qkvz, cq, ckv, kab, ab, gates = _proj(h, _pack_w_in(w_in[l]), tm)
        og = _gdn(qkvz, ab, conv_w[l], a_log[l], dt_bias[l], gdn_norm_w[l], bsz, s_len)
        om = _mla(cq, ckv, kab, tables, mla_q_norm_w[l], mla_kv_norm_w[l],
                  _pack_w_uq(w_uq[l]), w_ukv[l].astype(BF16), bsz, s_len)
        h, hb = _mixer_tail(og, om, gates, h, w_o_gdn[l].astype(BF16), w_o_mla[l].astype(BF16),
                            w_out[l].astype(BF16), ln1_g[l], ln1_b[l], min(ROWS_TAIL, t))
        h = _moe(h, hb, w_router_group[l], b_router_group[l], w_router_expert[l], b_router_expert[l],
                 w_gate, w_up, w_down, l, ln2_g[l], ln2_b[l], min(ROWS_ROUTER, t))
    return h.reshape(bsz, s_len, d)
```

```python
import functools

import numpy as np
import jax
import jax.numpy as jnp
from jax import lax
from jax.experimental import pallas as pl
from jax.experimental.pallas import tpu as pltpu

F32 = jnp.float32
BF16 = jnp.bfloat16

D_MODEL = 1024
DEPTH = 2
GDN_HEADS = 8
GDN_DK = 128
GDN_DV = 128
CONV_K = 5
CHUNK = 64
MLA_HEADS = 8
Q_LORA = 384
KV_LORA = 256
QK_NOPE = 128
QK_ROPE = 64
V_HEAD = 128
ROPE_BASE = 10000.0
N_GROUPS = 8
EXPERTS_PER_GROUP = 8
N_EXPERTS = N_GROUPS * EXPERTS_PER_GROUP
TOP_K = 2
D_EXPERT = 512
MOE_BLOCK = 256
DN_ALPHA = (2 * DEPTH) ** 0.25
LN_EPS = 1e-5
RMS_EPS = 1e-6

ROWS_MATMUL = 256
ROWS_MEMBOUND = 1024
ROWS_TAIL = 512
ROWS_ROUTER = 512
LANES = 128
VMEM_LIMIT = 56 * 1024 * 1024

QKVZ_W = 4 * GDN_HEADS * GDN_DK
KAB_W = 2 * LANES
AB_W = LANES
GATES_W = 2 * D_MODEL
PROJ_SEGS = (QKVZ_W, Q_LORA, KV_LORA, KAB_W, AB_W, GATES_W)
PROJ_W = sum(PROJ_SEGS)


def _params(sem):
    return pltpu.CompilerParams(dimension_semantics=sem, vmem_limit_bytes=VMEM_LIMIT)


def _layer_norm(x, g, b):
    mu = jnp.mean(x, -1, keepdims=True)
    xc = x - mu
    var = jnp.mean(xc * xc, -1, keepdims=True)
    return xc * lax.rsqrt(var + LN_EPS) * g + b


def _rms_norm(x, w):
    return x * lax.rsqrt(jnp.mean(x * x, -1, keepdims=True) + RMS_EPS) * w


def _silu(x):
    return (0.5 * x) * (1.0 + jnp.tanh(0.5 * x))


def _sigmoid(x):
    return 1.0 / (1.0 + jnp.exp(-x))


def _dot(a, b):
    return jnp.dot(a, b, preferred_element_type=F32)


def _dot_nt(a, b):
    return lax.dot_general(a, b, (((1,), (1,)), ((), ())), preferred_element_type=F32)


def _ds(i, n):
    return pl.ds(i * n, n) if isinstance(i, int) else pl.ds(pl.multiple_of(i * n, n), n)


def _dot_tn(a, b):
    return lax.dot_general(a, b, (((0,), (0,)), ((), ())), preferred_element_type=F32)


def _rope_kernel(pos_ref, freq_ref, mq_ref, mk1_ref, mk2_ref):
    ang = pos_ref[...].astype(F32) * freq_ref[...]
    c = jnp.cos(ang)
    s = jnp.sin(ang)
    quarter = lax.broadcasted_iota(jnp.int32, ang.shape, 1) >> 5
    even = (quarter & 1) == 0
    mq_ref[...] = jnp.where((quarter == 0) | (quarter == 3), c, s)
    mk1_ref[...] = jnp.where(even, c, s)
    mk2_ref[...] = jnp.where(even, s, c)


def _rope_tables(positions, tm):
    t = positions.size
    half = QK_ROPE // 2
    inv_freq = jnp.power(ROPE_BASE, -jnp.arange(half, dtype=F32) / half)
    freq = jnp.tile(inv_freq, LANES // half)[None, :]
    out = jax.ShapeDtypeStruct((t, LANES), F32)
    row = pl.BlockSpec((tm, LANES), lambda i: (i, 0))
    return pl.pallas_call(
        _rope_kernel, out_shape=(out, out, out), grid=(t // tm,),
        in_specs=[pl.BlockSpec((tm, 1), lambda i: (i, 0)), pl.BlockSpec((1, LANES), lambda i: (0, 0))],
        out_specs=(row, row, row), compiler_params=_params(("parallel",)), name="rope_tables",
    )(positions.reshape(t, 1), freq)


def _ln_kernel(x_ref, g_ref, b_ref, o_ref):
    o_ref[...] = _layer_norm(x_ref[...], g_ref[...], b_ref[...])


def _ln_in(x2, g, b, tm):
    t, d = x2.shape
    row = pl.BlockSpec((tm, d), lambda i: (i, 0))
    vec = pl.BlockSpec((1, d), lambda i: (0, 0))
    return pl.pallas_call(
        _ln_kernel, out_shape=jax.ShapeDtypeStruct((t, d), F32), grid=(t // tm,),
        in_specs=[row, vec, vec], out_specs=row, compiler_params=_params(("parallel",)), name="ln_in",
    )(x2, g.reshape(1, d), b.reshape(1, d))


PROJ_CHUNK = 512


def _proj_kernel(h_ref, w_ref, *out_refs):
    a = h_ref[...].astype(BF16)
    off = 0
    for ref, width in zip(out_refs, PROJ_SEGS):
        for c in range(0, width, PROJ_CHUNK):
            wc = min(PROJ_CHUNK, width - c)
            ref[:, c:c + wc] = _dot(a, w_ref[:, off + c:off + c + wc]).astype(ref.dtype)
        off += width


def _proj(h, w_all, tm):
    t, d = h.shape
    dts = (BF16, F32, F32, F32, F32, BF16)
    outs = tuple(jax.ShapeDtypeStruct((t, w), dt) for w, dt in zip(PROJ_SEGS, dts))
    return pl.pallas_call(
        _proj_kernel, out_shape=outs, grid=(t // tm,),
        in_specs=[pl.BlockSpec((tm, d), lambda i: (i, 0)), pl.BlockSpec((d, PROJ_W), lambda i: (0, 0))],
        out_specs=tuple(pl.BlockSpec((tm, w), lambda i: (i, 0)) for w in PROJ_SEGS),
        compiler_params=_params(("parallel",)), name="in_proj",
    )(h, w_all)


def _pack_w_in(w_in):
    hq = GDN_HEADS * GDN_DK
    o_a = 4 * hq
    o_bt = o_a + 2 * GDN_HEADS
    o_cq = o_bt + 2 * GDN_HEADS
    o_ckv = o_cq + Q_LORA
    o_kr = o_ckv + KV_LORA
    o_g = o_kr + QK_ROPE
    half = QK_ROPE // 2
    k1 = w_in[:, o_kr:o_kr + half]
    k2 = w_in[:, o_kr + half:o_g]
    d = w_in.shape[0]
    ab = jnp.concatenate([w_in[:, o_a:o_cq], jnp.zeros((d, AB_W - 4 * GDN_HEADS), w_in.dtype)], 1)
    return jnp.concatenate(
        [w_in[:, :o_a], w_in[:, o_cq:o_ckv], w_in[:, o_ckv:o_kr],
         k1, k1, k1, k1, -k2, k2, -k2, k2, ab, w_in[:, o_g:]], axis=1).astype(BF16)


GDN_HPS = 2
CONV_PAD = 8
CUM_CHUNKS = 8
PREP_CHUNKS = 16


def _gdn_kernel(q_ref, k_ref, v_ref, z_ref, ab_ref, cwq_ref, cwk_ref, cwv_ref, alog_ref, dtb_ref, nw_ref,
                o_ref, qs, ks, vs, gs, bs, brs, bns, mqs, ges, os_, xp, gates_s):
    s_len = q_ref.shape[0]
    nc = s_len // CHUNK
    first_head = pl.program_id(1) * GDN_HPS

    xp[:CONV_PAD, :] = jnp.zeros((CONV_PAD, LANES), F32)
    xp[CONV_PAD + s_len:, :] = jnp.zeros((CONV_PAD, LANES), F32)

    def conv_silu(x_ref, cw_ref, cols):
        xp[CONV_PAD:CONV_PAD + s_len, :] = x_ref[:, cols].astype(F32)
        acc = None
        for j in range(CONV_K):
            term = xp[CONV_PAD + j - CONV_K // 2:CONV_PAD + j - CONV_K // 2 + s_len, :] * cw_ref[j:j + 1, cols]
            acc = term if acc is None else acc + term
        return _silu(acc)

    def l2n(x, scale=1.0):
        return x * (lax.rsqrt(jnp.sum(x * x, -1, keepdims=True) + RMS_EPS) * scale)

    lane = lax.broadcasted_iota(jnp.int32, (1, LANES), 1)

    def pick(vals, idx):
        col = jnp.sum(jnp.where(lane == idx, vals, 0.0), -1, keepdims=True)
        return jnp.broadcast_to(col, (s_len, LANES))

    def stage(hh):
        cols = slice(hh * LANES, (hh + 1) * LANES)
        qs[...] = l2n(conv_silu(q_ref, cwq_ref, cols), GDN_DK ** -0.5)
        ks[...] = l2n(conv_silu(k_ref, cwk_ref, cols))
        vs[...] = conv_silu(v_ref, cwv_ref, cols)
        gates = gates_s[...]
        for d in range(2):
            gs[d] = pick(gates, first_head + hh + d * GDN_HEADS)
            bs[d] = pick(gates, first_head + hh + (2 + d) * GDN_HEADS)

    @pl.when(pl.program_id(1) == 0)
    def _():
        ab = ab_ref[...]
        x = ab + dtb_ref[...]
        softplus = jnp.maximum(x, 0.0) + jnp.log(1.0 + jnp.exp(-jnp.abs(x)))
        gates_s[...] = jnp.where(lane < 2 * GDN_HEADS, -jnp.exp(alog_ref[...]) * softplus, _sigmoid(ab))

    ci = lax.broadcasted_iota(jnp.int32, (CHUNK, 2 * CHUNK), 0)
    cj2 = lax.broadcasted_iota(jnp.int32, (CHUNK, 2 * CHUNK), 1)
    right = cj2 >= CHUNK
    cj = cj2 & (CHUNK - 1)
    eye = jnp.where(ci == cj, 1.0, 0.0)
    incl = (ci >= cj, ci <= cj)
    strict = (ci > cj, ci < cj)
    tri3 = tuple(jnp.concatenate([jnp.where(m[:, :CHUNK], 1.0, 0.0).astype(BF16)] * 3, axis=1) for m in incl)
    ones3 = jnp.ones((CHUNK, 3 * CHUNK), BF16)
    last_row = (CHUNK - 1, 0)

    unroll = CUM_CHUNKS if nc % CUM_CHUNKS == 0 else 1

    def cumulate(c, carry):
        rows = [pl.ds(pl.multiple_of((c * unroll + j) * CHUNK, CHUNK), CHUNK) for j in range(unroll)]
        chains = [(j, d) for j in range(unroll) for d in range(2)]
        def split3(x):
            x1 = x.astype(BF16)
            r1 = x - x1.astype(F32)
            x2 = r1.astype(BF16)
            return jnp.concatenate([x1, x2, (r1 - x2.astype(F32)).astype(BF16)], axis=0)

        parts = [split3(gs[d, rows[j], :]) for j, d in chains]
        b = [_dot(tri3[d], p) for (j, d), p in zip(chains, parts)]
        diag = [split3(x * eye) for x in b]
        b_row = [_dot(ones3, x) for x in diag]
        for (j, d), x, y in zip(chains, b, b_row):
            gs[d, rows[j], :] = x
            brs[d, rows[j], :] = y
        return carry

    def prep(hh, it, carry):
        unroll = PREP_CHUNKS if nc % PREP_CHUNKS == 0 else 1
        cs = [it * unroll + j for j in range(unroll)]
        rows = [pl.ds(pl.multiple_of(c * CHUNK, CHUNK), CHUNK) for c in cs]
        q = [qs[r, :] for r in rows]
        k = [ks[r, :] for r in rows]
        kb = [x.astype(BF16) for x in k]
        kk = [_dot_nt(x, jnp.concatenate([x, x], axis=0)) for x in kb]
        qk = [_dot_nt(x.astype(BF16), y) for x, y in zip(q, kb)]
        chains = [(j, d) for j in range(unroll) for d in range(2)]
        b = [gs[d, rows[j], :] for j, d in chains]
        beta = [bs[d, rows[j], :] for j, d in chains]
        gamma = [jnp.exp(jnp.where(incl[d], b[n] - brs[d, rows[j], :], -jnp.inf))
                 for n, (j, d) in enumerate(chains)]
        pt = [jnp.where(right, eye, jnp.where(strict[d], -beta[n] * kk[j] * gamma[n], 0.0))
              for n, (j, d) in enumerate(chains)]
        for _ in range(int(np.log2(CHUNK))):
            ptb = [x.astype(BF16) for x in pt]
            prod = [_dot(x[:, :CHUNK], x) for x in ptb]
            pt = [y + jnp.where(right, x, 0.0) for x, y in zip(pt, prod)]
        eb = [jnp.exp(x) for x in b]
        rhs = [jnp.concatenate([jnp.zeros((CHUNK, GDN_DV + GDN_DK), F32), jnp.concatenate(
                   [beta[n] * vs[rows[j], :], beta[n] * eb[n] * k[j]], axis=1)], axis=0).astype(BF16)
               for n, (j, d) in enumerate(chains)]
        wk = [_dot(t.astype(BF16), r).astype(BF16) for t, r in zip(pt, rhs)]
        b_last = [b[n][last_row[d]:last_row[d] + 1, :] for n, (j, d) in enumerate(chains)]
        kdt = [jnp.transpose(k[j] * jnp.exp(b_last[n] - b[n])).astype(BF16)
               for n, (j, d) in enumerate(chains)]
        bm = [_dot(x, y) for x, y in zip(kdt, wk)]
        pw = [_dot((qk[j] * gamma[n][:, :CHUNK]).astype(BF16), wk[n])
              for n, (j, d) in enumerate(chains)]
        for n, (j, d) in enumerate(chains):
            c = cs[j]
            bns[hh, d, pl.ds(pl.multiple_of(c * GDN_DK, GDN_DK), GDN_DK), :] = bm[n][:, :GDN_DV]
            r3 = pl.multiple_of(c * (GDN_DK + CHUNK), CHUNK)
            mqs[hh, d, pl.ds(r3, GDN_DK), :] = bm[n][:, GDN_DV:].astype(BF16)
            mqs[hh, d, pl.ds(r3 + GDN_DK, CHUNK), :] = (q[j] * eb[n] - pw[n][:, GDN_DV:]).astype(BF16)
            ges[hh, d, pl.ds(pl.multiple_of(c * 8, 8), 8), :] = jnp.broadcast_to(jnp.exp(b_last[n]), (8, LANES))
        for j in range(unroll):
            os_[hh, rows[j], :] = pw[2 * j][:, :GDN_DV] + pw[2 * j + 1][:, :GDN_DV]
        return carry

    for hh in range(GDN_HPS):
        stage(hh)
        lax.fori_loop(0, nc // unroll, cumulate, 0)
        lax.fori_loop(0, nc // (PREP_CHUNKS if nc % PREP_CHUNKS == 0 else 1), functools.partial(prep, hh), 0)

    chains = [(hh, d) for hh in range(GDN_HPS) for d in range(2)]

    def step(i, states):
        cs = (i, nc - 1 - i)
        sb = [x.astype(BF16) for x in states]
        r = [_dot(mqs[hh, d, pl.ds(pl.multiple_of(cs[d] * (GDN_DK + CHUNK), CHUNK), GDN_DK + CHUNK), :], sb[n])
             for n, (hh, d) in enumerate(chains)]
        new = []
        for n, (hh, d) in enumerate(chains):
            c = cs[d]
            os_[hh, pl.ds(pl.multiple_of(c * CHUNK, CHUNK), CHUNK), :] += r[n][GDN_DK:]
            ge = ges[hh, d, pl.ds(pl.multiple_of(c * 8, 8), 1), :]
            bn = bns[hh, d, pl.ds(pl.multiple_of(c * GDN_DK, GDN_DK), GDN_DK), :]
            new.append(ge * states[n] - r[n][:GDN_DK] + bn)
        return tuple(new)

    s0 = jnp.zeros((GDN_DK, GDN_DV), F32)
    lax.fori_loop(0, nc, step, (s0,) * len(chains))

    for hh in range(GDN_HPS):
        cols = slice(hh * LANES, (hh + 1) * LANES)
        o_ref[:, cols] = (_rms_norm(os_[hh], nw_ref[...]) * _silu(z_ref[:, cols].astype(F32))).astype(o_ref.dtype)


def _gdn(qkvz, ab, conv_w, a_log, dt_bias, norm_w, bsz, s_len):
    nh = GDN_HEADS
    qkvz3 = qkvz.reshape(bsz, s_len, QKVZ_W)
    ab3 = ab.reshape(bsz, s_len, AB_W)
    pad = jnp.zeros((AB_W - 2 * nh,), F32)
    alog = jnp.concatenate([a_log.reshape(-1), pad])[None, :]
    dtb = jnp.concatenate([dt_bias.reshape(-1), pad])[None, :]

    hps = GDN_HPS
    ng = nh // hps

    def col(k):
        return pl.BlockSpec((None, s_len, hps * LANES), lambda b, h: (b, 0, k * ng + h))

    def cw(k):
        return pl.BlockSpec((CONV_K, hps * LANES), lambda b, h: (0, k * ng + h))

    vec = pl.BlockSpec((1, LANES), lambda b, h: (0, 0))
    nc = s_len // CHUNK
    scratch = [
        pltpu.VMEM((s_len, LANES), F32), pltpu.VMEM((s_len, LANES), F32), pltpu.VMEM((s_len, LANES), F32),
        pltpu.VMEM((2, s_len, LANES), F32), pltpu.VMEM((2, s_len, LANES), F32),
        pltpu.VMEM((2, s_len, 2 * CHUNK), F32), pltpu.VMEM((hps, 2, nc * GDN_DK, GDN_DV), F32),
        pltpu.VMEM((hps, 2, nc * (GDN_DK + CHUNK), GDN_DV), BF16),
        pltpu.VMEM((hps, 2, nc * 8, LANES), F32), pltpu.VMEM((hps, s_len, GDN_DV), F32),
        pltpu.VMEM((s_len + 2 * CONV_PAD, LANES), F32), pltpu.VMEM((s_len, AB_W), F32),
    ]
    out = pl.pallas_call(
        _gdn_kernel, out_shape=jax.ShapeDtypeStruct((bsz, s_len, nh * GDN_DV), BF16), grid=(bsz, ng),
        in_specs=[col(0), col(1), col(2), col(3),
                  pl.BlockSpec((None, s_len, AB_W), lambda b, h: (b, 0, 0)),
                  cw(0), cw(1), cw(2), vec, vec, vec],
        out_specs=pl.BlockSpec((None, s_len, hps * GDN_DV), lambda b, h: (b, 0, h)),
        scratch_shapes=scratch, compiler_params=_params(("parallel", "arbitrary")), name="gdn",
    )(qkvz3, qkvz3, qkvz3, qkvz3, ab3, conv_w, conv_w, conv_w, alog, dtb, norm_w.reshape(1, GDN_DV))
    return out.reshape(bsz * s_len, nh * GDN_DV)


MLA_TQ = 256
MLA_KC = 1024


def _mla_kernel(cq_ref, ckv_ref, kab_ref, mq_ref, mk1_ref, mk2_ref, qnw_ref, kvnw_ref, wq_ref, wkv_ref,
                o_ref, q_s, k_s, vt_s, sta_s, stb_s, cqn_s, ckvn_s):
    s_len = cq_ref.shape[0]
    scale = (QK_NOPE + QK_ROPE) ** -0.5

    @pl.when(pl.program_id(1) == 0)
    def _():
        cqn_s[...] = _rms_norm(cq_ref[...], qnw_ref[...]).astype(BF16)
        ckvn_s[...] = _rms_norm(ckv_ref[...], kvnw_ref[...]).astype(BF16)
        kab = kab_ref[...]
        k_s[:, QK_NOPE:] = (kab[:, :LANES] * mk1_ref[...] + kab[:, LANES:] * mk2_ref[...]).astype(BF16)
        vt_s[V_HEAD:, :] = jnp.ones((vt_s.shape[0] - V_HEAD, s_len), BF16)

    q = _dot(cqn_s[...], wq_ref[...]) * scale
    q_s[:, :QK_NOPE] = q[:, :QK_NOPE].astype(BF16)
    q_s[:, QK_NOPE:] = (q[:, QK_NOPE:] * mq_ref[...]).astype(BF16)
    kv = _dot(ckvn_s[...], wkv_ref[...])
    k_s[:, :QK_NOPE] = kv[:, :QK_NOPE].astype(BF16)
    vt_s[:V_HEAD, :] = jnp.transpose(kv[:, QK_NOPE:]).astype(BF16)

    tq = min(MLA_TQ, s_len)

    nq = s_len // tq
    kc = min(MLA_KC, s_len)

    def scores(i, st):
        st[...] = _dot_nt(k_s[...], q_s[_ds(i, tq), :])

    def attend(i, st, nxt):
        m = jnp.max(st[...], 0, keepdims=True)
        ot = jnp.zeros((vt_s.shape[0], tq), F32)
        for c in range(0, s_len, kc):
            if nxt is not None:
                nxt[c:c + kc, :] = _dot_nt(k_s[c:c + kc, :], q_s[_ds(i + 1, tq), :])
            p = jnp.exp(st[c:c + kc, :] - m).astype(BF16)
            ot = ot + _dot(vt_s[:, c:c + kc], p)
        o = jnp.transpose(ot[:V_HEAD] / ot[V_HEAD:V_HEAD + 1])
        o_ref[_ds(i, tq), :] = o.astype(o_ref.dtype)

    def pair(j, prefetch):
        attend(2 * j, sta_s, stb_s)
        attend(2 * j + 1, stb_s, sta_s if prefetch else None)

    assert nq == 1 or nq % 2 == 0
    scores(0, sta_s)
    if nq == 1:
        attend(0, sta_s, None)
    else:
        def body(j, carry):
            pair(j, True)
            return carry
        lax.fori_loop(0, nq // 2 - 1, body, 0)
        pair(nq // 2 - 1, False)


def _pack_w_uq(w_uq):
    w = w_uq.reshape(Q_LORA, MLA_HEADS, QK_NOPE + QK_ROPE)
    half = QK_ROPE // 2
    r1 = w[..., QK_NOPE:QK_NOPE + half]
    r2 = w[..., QK_NOPE + half:]
    w = jnp.concatenate([w[..., :QK_NOPE], r1, r1, -r2, r2], axis=-1)
    return w.reshape(Q_LORA, MLA_HEADS * 2 * LANES).astype(BF16)


def _mla(cq, ckv, kab, tables, qnw, kvnw, wq, wkv, bsz, s_len):
    nh = MLA_HEADS

    def per_b(width):
        return pl.BlockSpec((None, s_len, width), lambda b, h: (b, 0, 0))

    def vec(width):
        return pl.BlockSpec((1, width), lambda b, h: (0, 0))

    mq, mk1, mk2 = (t.reshape(bsz, s_len, LANES) for t in tables)
    out = pl.pallas_call(
        _mla_kernel, out_shape=jax.ShapeDtypeStruct((bsz, s_len, nh * V_HEAD), BF16), grid=(bsz, nh),
        in_specs=[per_b(Q_LORA), per_b(KV_LORA), per_b(KAB_W), per_b(LANES), per_b(LANES), per_b(LANES),
                  vec(Q_LORA), vec(KV_LORA),
                  pl.BlockSpec((Q_LORA, 2 * LANES), lambda b, h: (0, h)),
                  pl.BlockSpec((KV_LORA, QK_NOPE + V_HEAD), lambda b, h: (0, h))],
        out_specs=pl.BlockSpec((None, s_len, V_HEAD), lambda b, h: (b, 0, h)),
        scratch_shapes=[pltpu.VMEM((s_len, 2 * LANES), BF16), pltpu.VMEM((s_len, 2 * LANES), BF16),
                        pltpu.VMEM((V_HEAD + 16, s_len), BF16),
                        pltpu.VMEM((s_len, min(MLA_TQ, s_len)), F32),
                        pltpu.VMEM((s_len, min(MLA_TQ, s_len)), F32),
                        pltpu.VMEM((s_len, Q_LORA), BF16), pltpu.VMEM((s_len, KV_LORA), BF16)],
        compiler_params=_params(("parallel", "arbitrary")), name="mla",
    )(cq.reshape(bsz, s_len, Q_LORA), ckv.reshape(bsz, s_len, KV_LORA), kab.reshape(bsz, s_len, KAB_W),
      mq, mk1, mk2, qnw.reshape(1, Q_LORA), kvnw.reshape(1, KV_LORA), wq, wkv)
    return out.reshape(bsz * s_len, nh * V_HEAD)


def _tail_kernel(og_ref, om_ref, gates_ref, h_ref, wog_ref, wom_ref, wout_ref, g_ref, b_ref, o_ref, ob_ref):
    tm, d = h_ref.shape
    parts = [slice(k * (tm // 2), (k + 1) * (tm // 2)) for k in range(2)] if tm % 32 == 0 else [slice(0, tm)]
    y_gdn = [_dot(og_ref[r, :], wog_ref[...]) for r in parts]
    y_mla = [_dot(om_ref[r, :], wom_ref[...]) for r in parts]
    m = []
    for r, yg, ym in zip(parts, y_gdn, y_mla):
        y = _sigmoid(gates_ref[r, :d].astype(F32)) * yg + _sigmoid(gates_ref[r, d:].astype(F32)) * ym
        m.append(_dot(y.astype(BF16), wout_ref[...]))
    for r, mk in zip(parts, m):
        out = _layer_norm(DN_ALPHA * h_ref[r, :] + mk, g_ref[...], b_ref[...])
        o_ref[r, :] = out
        ob_ref[r, :] = out.astype(BF16)


def _mixer_tail(og, om, gates, h, wog, wom, wout, g, b, tm):
    t, d = h.shape
    row = lambda w: pl.BlockSpec((tm, w), lambda i: (i, 0))
    full = pl.BlockSpec((d, d), lambda i: (0, 0))
    vec = pl.BlockSpec((1, d), lambda i: (0, 0))
    return pl.pallas_call(
        _tail_kernel, out_shape=(jax.ShapeDtypeStruct((t, d), F32), jax.ShapeDtypeStruct((t, d), BF16)),
        grid=(t // tm,),
        in_specs=[row(d), row(d), row(2 * d), row(d), full, full, full, vec, vec],
        out_specs=(row(d), row(d)), compiler_params=_params(("parallel",)), name="mixer_tail",
    )(og, om, gates, h, wog, wom, wout, g.reshape(1, d), b.reshape(1, d))


def _router_kernel(h_ref, wh_ref, wl_ref, b_ref, eid_ref, gate_ref, cnt_ref, run):
    @pl.when(pl.program_id(0) == 0)
    def _():
        run[...] = jnp.zeros_like(run)

    x = h_ref[...]
    xh = x.astype(BF16)
    xl = (x - xh.astype(F32)).astype(BF16)
    logits = _dot(xh, wh_ref[...]) + (_dot(xl, wh_ref[...]) + _dot(xh, wl_ref[...])) + b_ref[...]
    tm = logits.shape[0]
    lane = lax.broadcasted_iota(jnp.int32, logits.shape, 1)
    lane_f = lane.astype(F32)
    neg = -jnp.inf

    def first_max(vals):
        m = jnp.max(vals, -1, keepdims=True)
        idx = jnp.min(jnp.where(vals == m, lane_f, float(LANES)), -1, keepdims=True)
        return m, idx

    is_grp = lane < N_GROUPS
    mg, grp = first_max(jnp.where(is_grp, logits, neg))
    p_grp = 1.0 / jnp.sum(jnp.where(is_grp, jnp.exp(logits - mg), 0.0), -1, keepdims=True)
    in_grp = ((lane - N_GROUPS) >> 3).astype(F32) == grp
    le = jnp.where(in_grp, logits, neg)
    l1, i1 = first_max(le)
    l2, i2 = first_max(jnp.where(lane_f == i1, neg, le))
    t = jnp.exp(l2 - l1)
    g1 = p_grp / (1.0 + t)
    g2 = p_grp * t / (1.0 + t)
    hit1 = lane_f == i1
    hit2 = lane_f == i2
    onehot = jnp.where(hit1 | hit2, 1.0, 0.0)
    ri = lax.broadcasted_iota(jnp.int32, (tm, tm), 0)
    rj = lax.broadcasted_iota(jnp.int32, (tm, tm), 1)
    before = jnp.where(ri > rj, 1.0, 0.0).astype(BF16)
    prefix = _dot(before, onehot.astype(BF16)) + run[...]
    rank1 = jnp.sum(jnp.where(hit1, prefix, 0.0), -1, keepdims=True)
    rank2 = jnp.sum(jnp.where(hit2, prefix, 0.0), -1, keepdims=True)
    run[...] += jnp.sum(onehot, 0, keepdims=True)
    cnt_ref[...] = run[...]
    ids = jnp.where(lane == 0, i1 - float(N_GROUPS), jnp.where(lane == 1, i2 - float(N_GROUPS),
                    jnp.where(lane == 2, rank1, jnp.where(lane == 3, rank2, 0.0))))
    eid_ref[...] = ids.astype(jnp.int32)
    gate_ref[...] = jnp.where(lane == 0, g1, jnp.where(lane == 1, g2, 0.0))


def _router(h, w_rg, b_rg, w_re, b_re, tm):
    t, d = h.shape
    padw = LANES - N_GROUPS - N_EXPERTS
    w = jnp.concatenate([w_rg, w_re, jnp.zeros((d, padw), F32)], axis=1)
    b = jnp.concatenate([b_rg, b_re, jnp.zeros((padw,), F32)])[None, :]
    wh = w.astype(BF16)
    wl = (w - wh.astype(F32)).astype(BF16)
    row = pl.BlockSpec((tm, LANES), lambda i: (i, 0))
    one = pl.BlockSpec((1, LANES), lambda i: (0, 0))
    wspec = pl.BlockSpec((d, LANES), lambda i: (0, 0))
    return pl.pallas_call(
        _router_kernel,
        out_shape=(jax.ShapeDtypeStruct((t, LANES), jnp.int32), jax.ShapeDtypeStruct((t, LANES), F32),
                   jax.ShapeDtypeStruct((1, LANES), F32)),
        grid=(t // tm,),
        in_specs=[pl.BlockSpec((tm, d), lambda i: (i, 0)), wspec, wspec, one],
        out_specs=(row, row, one), scratch_shapes=[pltpu.VMEM((1, LANES), F32)],
        compiler_params=_params(("arbitrary",)), name="router",
    )(h, wh, wl, b)


def _dest_kernel(ids_ref, start_ref, o_ref):
    ids = ids_ref[...]
    lane = lax.broadcasted_iota(jnp.int32, ids.shape, 1)
    start = start_ref[...]

    def row_of(k):
        hit = lane == ids[:, k:k + 1] + N_GROUPS
        return jnp.sum(jnp.where(hit, start, 0.0), -1, keepdims=True).astype(jnp.int32) + ids[:, TOP_K + k:TOP_K + k + 1]

    o_ref[...] = jnp.where(lane == 0, row_of(0), jnp.where(lane == 1, row_of(1), 0))


def _dest_rows(ids, pad_start, tm):
    t = ids.shape[0]
    start = jnp.zeros((1, LANES), F32).at[0, N_GROUPS:N_GROUPS + N_EXPERTS].set(pad_start.astype(F32))
    row = pl.BlockSpec((tm, LANES), lambda i: (i, 0))
    return pl.pallas_call(
        _dest_kernel, out_shape=jax.ShapeDtypeStruct((t, LANES), jnp.int32), grid=(t // tm,),
        in_specs=[row, pl.BlockSpec((1, LANES), lambda i: (0, 0))], out_specs=row,
        compiler_params=_params(("parallel",)), name="dest_rows",
    )(ids, start)


def _expert_kernel(layer, be_ref, nx_ref, nu_ref, x_ref, wg_hbm, wu_hbm, wd_hbm, o_ref,
                   wgb, wub, wdb, wgs, wus, wds, sem):
    i = pl.program_id(0)

    def weight_copies(e):
        return [pltpu.make_async_copy(hbm.at[layer, e], stage, sem.at[n])
                for n, (hbm, stage) in enumerate(((wg_hbm, wgs), (wu_hbm, wus), (wd_hbm, wds)))]

    @pl.when(i == 0)
    def _():
        for c in weight_copies(be_ref[0]):
            c.start()

    @pl.when((i == 0) | (be_ref[i] != be_ref[jnp.maximum(i - 1, 0)]))
    def _():
        for c in weight_copies(be_ref[i]):
            c.wait()
        wgb[...] = wgs[...].astype(BF16)
        wub[...] = wus[...].astype(BF16)
        wdb[...] = wds[...].astype(BF16)

        @pl.when(nx_ref[i] >= 0)
        def _():
            for c in weight_copies(nx_ref[i]):
                c.start()

    @pl.when(i < nu_ref[0])
    def _():
        x = x_ref[...]
        hid = _silu(_dot(x, wgb[...])) * _dot(x, wub[...])
        o_ref[...] = _dot(hid.astype(BF16), wdb[...]).astype(o_ref.dtype)

    @pl.when(i >= nu_ref[0])
    def _():
        o_ref[...] = jnp.zeros_like(o_ref)


def _experts(xb, blk_expert, nxt_expert, n_used, w_gate, w_up, w_down, layer):
    n_rows, d = xb.shape
    n_blocks = n_rows // MOE_BLOCK
    hbm = pl.BlockSpec(memory_space=pl.ANY)
    grid_spec = pltpu.PrefetchScalarGridSpec(
        num_scalar_prefetch=3, grid=(n_blocks,),
        in_specs=[pl.BlockSpec((MOE_BLOCK, d), lambda i, be, nx, nu: (i, 0)), hbm, hbm, hbm],
        out_specs=pl.BlockSpec((MOE_BLOCK, d), lambda i, be, nx, nu: (i, 0)),
        scratch_shapes=[pltpu.VMEM((d, D_EXPERT), BF16), pltpu.VMEM((d, D_EXPERT), BF16),
                        pltpu.VMEM((D_EXPERT, d), BF16),
                        pltpu.VMEM((d, D_EXPERT), F32), pltpu.VMEM((d, D_EXPERT), F32),
                        pltpu.VMEM((D_EXPERT, d), F32), pltpu.SemaphoreType.DMA((3,))])
    return pl.pallas_call(
        functools.partial(_expert_kernel, layer), out_shape=jax.ShapeDtypeStruct((n_rows, d), BF16),
        grid_spec=grid_spec, compiler_params=_params(("arbitrary",)), name="experts",
    )(blk_expert, nxt_expert, n_used, xb, w_gate, w_up, w_down)


def _combine_kernel(h_ref, y_ref, gate_ref, g_ref, b_ref, o_ref):
    d = h_ref.shape[1]
    f = gate_ref[:, 0:1] * y_ref[:, :d].astype(F32) + gate_ref[:, 1:2] * y_ref[:, d:].astype(F32)
    o_ref[...] = _layer_norm(DN_ALPHA * h_ref[...] + f, g_ref[...], b_ref[...])


def _combine(h, y, gate, g, b, tm):
    t, d = h.shape
    row = pl.BlockSpec((tm, d), lambda i: (i, 0))
    vec = pl.BlockSpec((1, d), lambda i: (0, 0))
    return pl.pallas_call(
        _combine_kernel, out_shape=jax.ShapeDtypeStruct((t, d), F32), grid=(t // tm,),
        in_specs=[row, pl.BlockSpec((tm, TOP_K * d), lambda i: (i, 0)),
                  pl.BlockSpec((tm, LANES), lambda i: (i, 0)), vec, vec], out_specs=row,
        compiler_params=_params(("parallel",)), name="combine_ln",
    )(h, y, gate, g.reshape(1, d), b.reshape(1, d))


def _dispatch(ids, cnt, tm):
    t = ids.shape[0]
    n_assign = t * TOP_K
    counts = cnt[0, N_GROUPS:N_GROUPS + N_EXPERTS].astype(jnp.int32)
    padded = (counts + MOE_BLOCK - 1) // MOE_BLOCK * MOE_BLOCK
    pad_end = jnp.cumsum(padded)
    pad_start = pad_end - padded
    start = jnp.cumsum(counts) - counts
    eid = ids[:, :TOP_K]
    dest = _dest_rows(ids, pad_start, tm)[:, :TOP_K]
    order = jnp.argsort(eid.reshape(n_assign))
    n_blocks = -(-(n_assign + N_EXPERTS * (MOE_BLOCK - 1)) // MOE_BLOCK)
    blk_first = (jnp.arange(n_blocks) * MOE_BLOCK)[:, None]
    ended = jnp.dot((pad_end[None, :] <= blk_first).astype(F32), jnp.ones((N_EXPERTS,), F32))
    experts = jnp.arange(N_EXPERTS)
    last_used = jnp.max(jnp.where(counts > 0, experts, 0))
    blk_expert = jnp.minimum(ended.astype(jnp.int32), last_used)
    later = (experts[None, :] > experts[:, None]) & (counts[None, :] > 0)
    nxt = jnp.min(jnp.where(later, experts[None, :], N_EXPERTS), axis=1)
    nxt = jnp.where(nxt == N_EXPERTS, -1, nxt)
    onehot = (blk_expert[:, None] == experts[None, :]).astype(F32)
    table = jnp.stack([pad_start - start, start + counts, nxt], axis=1).astype(F32)
    looked = jnp.dot(onehot, table, precision=lax.Precision.HIGHEST).astype(jnp.int32)
    shift = jnp.repeat(looked[:, 0], MOE_BLOCK)
    end = jnp.repeat(looked[:, 1], MOE_BLOCK)
    nxt_expert = looked[:, 2]
    rows = jnp.arange(n_blocks * MOE_BLOCK)
    src = rows - shift
    row_tok = jnp.where(src < end, jnp.take(order, jnp.clip(src, 0, n_assign - 1), mode="clip") // TOP_K,
                        rows % t).astype(jnp.int32)
    n_used = (pad_end[-1:] // MOE_BLOCK).astype(jnp.int32)
    return row_tok, dest, blk_expert, nxt_expert, n_used


def _moe(h, hb, w_rg, b_rg, w_re, b_re, w_gate, w_up, w_down, layer, g, b, tm):
    ids, gate, cnt = _router(h, w_rg, b_rg, w_re, b_re, tm)
    row_tok, dest, blk_expert, nxt_expert, n_used = _dispatch(ids, cnt, tm)
    xb = jnp.take(hb, row_tok, axis=0, mode="clip")
    yb = _experts(xb, blk_expert, nxt_expert, n_used, w_gate, w_up, w_down, layer)
    y = jnp.take(yb, dest.reshape(-1), axis=0, mode="clip").reshape(h.shape[0], TOP_K * h.shape[1])
    return _combine(h, y, gate, g, b, min(ROWS_MEMBOUND, h.shape[0]))


def kernel(x, positions, ln_in_g, ln_in_b, w_in, conv_w, a_log, dt_bias, gdn_norm_w, w_o_gdn, mla_q_norm_w, w_uq, mla_kv_norm_w, w_ukv, w_o_mla, w_out, ln1_g, ln1_b, w_router_group, b_router_group, w_router_expert, b_router_expert, w_gate, w_up, w_down, ln2_g, ln2_b):
    bsz, s_len, d = x.shape
    t = bsz * s_len
    tm = min(ROWS_MATMUL, t)
    tables = _rope_tables(positions, min(ROWS_MEMBOUND, t))
    h = _ln_in(x.reshape(t, d), ln_in_g, ln_in_b, min(ROWS_MEMBOUND, t))
    for l in range(DEPTH):
        qkvz, cq, ckv, kab, ab, gates = _proj(h, _pack_w_in(w_in[l]), tm)
        og = _gdn(qkvz, ab, conv_w[l], a_log[l], dt_bias[l], gdn_norm_w[l], bsz, s_len)
        om = _mla(cq, ckv, kab, tables, mla_q_norm_w[l], mla_kv_norm_w[l],
                  _pack_w_uq(w_uq[l]), w_ukv[l].astype(BF16), bsz, s_len)
        h, hb = _mixer_tail(og, om, gates, h, w_o_gdn[l].astype(BF16), w_o_mla[l].astype(BF16),
                            w_out[l].astype(BF16), ln1_g[l], ln1_b[l], min(ROWS_TAIL, t))
        h = _moe(h, hb, w_router_group[l], b_router_group[l], w_router_expert[l], b_router_expert[l],
                 w_gate, w_up, w_down, l, ln2_g[l], ln2_b[l], min(ROWS_ROUTER, t))
    return h.reshape(bsz, s_len, d)
```
